```python
import math
import jax
import jax.numpy as jnp
from jax import lax
import numpy as np


D_MODEL = 1024
BATCH = 8
SEQ = 2048
DEPTH = 1
DEC_BATCH = 128
DEC_SEQ = 8
PAST_LEN = 16384
PAGE_SIZE = 128

HEAD_DIM = 128
DN_HEADS = D_MODEL // (2 * HEAD_DIM)
RET_HEADS = D_MODEL // (2 * HEAD_DIM)
DN_DK = HEAD_DIM
DN_DV = HEAD_DIM
RET_DK = HEAD_DIM
RET_DV = HEAD_DIM
DN_QK = DN_HEADS * DN_DK
DN_VD = DN_HEADS * DN_DV
RET_QK = RET_HEADS * RET_DK
RET_VD = RET_HEADS * RET_DV
D_IN = 3 * DN_QK + DN_VD + 2 * DN_HEADS + 2 * RET_QK + 2 * RET_VD
D_MIX = DN_VD + RET_VD
DN_CONV = 4
FFN_CONV = 3
D_FF = 2816
PLE_DIM = 256
CHUNK = 64
ROPE_BASE = 10000.0
EPS = 1e-6

kernel_name = 'hymba_gdn_retnet_convffn_step'


def _rmsnorm(x, w):
    xf = x.astype(jnp.float32)
    y = xf * lax.rsqrt(jnp.mean(xf * xf, axis=-1, keepdims=True) + EPS)
    return y * w.astype(jnp.float32)


def _l2norm(x):
    xf = x.astype(jnp.float32)
    return xf * lax.rsqrt(jnp.sum(xf * xf, axis=-1, keepdims=True) + EPS)


def _chunk_len(L):
    return CHUNK if L % CHUNK == 0 else L


def _causal_dwconv(x, buf, w):
    W = w.shape[0]
    L = x.shape[1]
    xp = jnp.concatenate([buf.astype(x.dtype), x], axis=1)
    y = xp[:, 0:L] * w[0]
    for j in range(1, W):
        y = y + xp[:, j:j + L] * w[j]
    return y, xp[:, -(W - 1):]


def _rotary(x, pos):
    d = x.shape[-1]
    inv = ROPE_BASE ** (-jnp.arange(0, d, 2, dtype=jnp.float32) / d)
    ang = pos.astype(jnp.float32)[:, None] * inv[None, :]
    cos = jnp.cos(ang)[None, :, None, :]
    sin = jnp.sin(ang)[None, :, None, :]
    xf = x.astype(jnp.float32)
    x1 = xf[..., 0::2]
    x2 = xf[..., 1::2]
    return jnp.stack([x1 * cos - x2 * sin, x1 * sin + x2 * cos], axis=-1).reshape(x.shape)


def _to_blocks(t, N, C):
    B, L, H, d = t.shape
    return t.reshape(B, N, C, H, d).transpose(0, 3, 1, 2, 4)


def _from_blocks(o):
    N, B, H, C, d = o.shape
    return o.transpose(1, 0, 3, 2, 4).reshape(B, N * C, H, d)


def _gated_delta_chunked(q, k, v, g, beta, S0):
    B, L, H, dk = q.shape
    dv = v.shape[-1]
    C = _chunk_len(L)
    N = L // C
    qb = _to_blocks(q, N, C) * (dk ** -0.5)
    kb = _to_blocks(k, N, C)
    vb = _to_blocks(v.astype(jnp.float32), N, C)
    gb = g.reshape(B, N, C, H).transpose(0, 3, 1, 2)
    bb = beta.reshape(B, N, C, H).transpose(0, 3, 1, 2)
    gc = jnp.cumsum(gb, axis=-1)
    causal = jnp.tril(jnp.ones((C, C), dtype=bool))
    strict = jnp.tril(jnp.ones((C, C), dtype=bool), -1)
    decay = jnp.exp(jnp.where(causal, gc[..., :, None] - gc[..., None, :], -jnp.inf))
    k_beta = kb * bb[..., None]
    A = jnp.where(strict, jnp.einsum('bhncd,bhnmd->bhncm', k_beta, kb) * decay, 0.0)
    T = A + jnp.eye(C, dtype=jnp.float32)
    rhs = jnp.concatenate([vb * bb[..., None], k_beta * jnp.exp(gc)[..., None]], axis=-1)
    sol = lax.linalg.triangular_solve(T, rhs, left_side=True, lower=True, unit_diagonal=True)
    u = sol[..., :dv]
    w = sol[..., dv:]
    qk = jnp.einsum('bhncd,bhnmd->bhncm', qb, kb) * decay
    q_dec = qb * jnp.exp(gc)[..., None]
    g_last = gc[..., -1]
    k_tail = kb * jnp.exp(g_last[..., None] - gc)[..., None]
    mv = lambda t: jnp.moveaxis(t, 2, 0)

    def step(S, xs):
        u_i, w_i, qk_i, qd_i, kt_i, gl_i = xs
        v_new = u_i - jnp.einsum('bhcd,bhde->bhce', w_i, S)
        o = jnp.einsum('bhcd,bhde->bhce', qd_i, S) + jnp.einsum('bhcm,bhme->bhce', qk_i, v_new)
        S = S * jnp.exp(gl_i)[..., None, None] + jnp.einsum('bhcd,bhce->bhde', kt_i, v_new)
        return S, o

    S, o = lax.scan(step, S0.astype(jnp.float32),
                    (mv(u), mv(w), mv(qk), mv(q_dec), mv(k_tail), jnp.moveaxis(g_last, 2, 0)))
    return _from_blocks(o), S


def _retention_chunked(q, k, v, R0):
    B, L, H, dk = q.shape
    C = _chunk_len(L)
    N = L // C
    log_gamma = jnp.log(1.0 - 2.0 ** (-5.0 - jnp.arange(H, dtype=jnp.float32)))
    bpos = (jnp.arange(C, dtype=jnp.float32) + 1.0)[None, :] * log_gamma[:, None]
    causal = jnp.tril(jnp.ones((C, C), dtype=bool))
    D = jnp.exp(jnp.where(causal, bpos[:, :, None] - bpos[:, None, :], -jnp.inf))
    qb = _to_blocks(q, N, C)
    kb = _to_blocks(k, N, C) * (dk ** -0.5)
    vb = _to_blocks(v.astype(jnp.float32), N, C)
    qk = jnp.einsum('bhncd,bhnmd->bhncm', qb, kb) * D[None, :, None]
    o_intra = jnp.einsum('bhncm,bhnme->bhnce', qk, vb)
    q_dec = qb * jnp.exp(bpos)[None, :, None, :, None]
    k_tail = kb * jnp.exp(bpos[:, -1:] - bpos)[None, :, None, :, None]
    chunk_decay = jnp.exp(bpos[:, -1])
    mv = lambda t: jnp.moveaxis(t, 2, 0)

    def step(R, xs):
        oi, qd, kt, vi = xs
        o = oi + jnp.einsum('bhcd,bhde->bhce', qd, R)
        R = R * chunk_decay[None, :, None, None] + jnp.einsum('bhcd,bhce->bhde', kt, vi)
        return R, o

    R, o = lax.scan(step, R0.astype(jnp.float32), (mv(o_intra), mv(q_dec), mv(k_tail), mv(vb)))
    return _from_blocks(o), R


def _split_points():
    sizes = [3 * DN_QK, DN_VD, DN_HEADS, DN_HEADS, RET_QK, RET_QK, RET_VD, RET_VD]
    return [int(s) for s in np.cumsum(sizes)[:-1]]


def _layer(h, p, conv_buf, S_dn, R_ret, ffn_buf, pos0, attn_norm_w, w_in, dn_conv_w,
           dn_A_log, dn_dt_bias, dn_norm_w, ret_norm_w, w_out, ffn_norm_w, w_up,
           ffn_conv_w, ffn_conv_b, w_down, ple_norm_w, w_ple_gate, w_ple):
    B, L, _ = h.shape
    dt = h.dtype
    a = _rmsnorm(h, attn_norm_w).astype(dt)
    proj = a @ w_in
    dn_qkv, dn_z, dn_b, dn_a, r_q, r_k, r_v, r_g = jnp.split(proj, _split_points(), axis=-1)
    qkv, conv_new = _causal_dwconv(dn_qkv, conv_buf, dn_conv_w)
    qkv = jax.nn.silu(qkv.astype(jnp.float32))
    q, k, v = jnp.split(qkv, [DN_QK, 2 * DN_QK], axis=-1)
    q = _l2norm(q.reshape(B, L, DN_HEADS, DN_DK))
    k = _l2norm(k.reshape(B, L, DN_HEADS, DN_DK))
    v = v.reshape(B, L, DN_HEADS, DN_DV)
    beta = jax.nn.sigmoid(dn_b.astype(jnp.float32))
    g = -jnp.exp(dn_A_log.astype(jnp.float32)) * jax.nn.softplus(
        dn_a.astype(jnp.float32) + dn_dt_bias.astype(jnp.float32))
    o_dn, S_new = _gated_delta_chunked(q, k, v, g, beta, S_dn)
    o_dn = _rmsnorm(o_dn, dn_norm_w) * jax.nn.silu(
        dn_z.astype(jnp.float32).reshape(B, L, DN_HEADS, DN_DV))
    pos = pos0 + jnp.arange(L)
    rq = _rotary(r_q.reshape(B, L, RET_HEADS, RET_DK), pos)
    rk = _rotary(r_k.reshape(B, L, RET_HEADS, RET_DK), pos)
    rv = r_v.reshape(B, L, RET_HEADS, RET_DV)
    o_ret, R_new = _retention_chunked(rq, rk, rv, R_ret)
    mu = jnp.mean(o_ret, axis=-1, keepdims=True)
    var = jnp.mean(jnp.square(o_ret - mu), axis=-1, keepdims=True)
    o_ret = (o_ret - mu) * lax.rsqrt(var + EPS) * ret_norm_w.astype(jnp.float32).reshape(RET_HEADS, RET_DV)
    o_ret = o_ret * jax.nn.silu(r_g.astype(jnp.float32).reshape(B, L, RET_HEADS, RET_DV))
    mix = jnp.concatenate([o_dn.reshape(B, L, DN_VD), o_ret.reshape(B, L, RET_VD)], axis=-1).astype(dt)
    h = h + mix @ w_out
    m = _rmsnorm(h, ffn_norm_w).astype(dt)
    u, ffn_new = _causal_dwconv(m @ w_up, ffn_buf, ffn_conv_w)
    u = u + ffn_conv_b
    ug, uv = jnp.split(u, 2, axis=-1)
    h = h + (jax.nn.silu(ug) * uv) @ w_down
    gate = jax.nn.sigmoid((_rmsnorm(h, ple_norm_w).astype(dt) @ w_ple_gate).astype(jnp.float32))
    h = h + (gate * (p @ w_ple).astype(jnp.float32)).astype(dt)
    return (h, conv_new.astype(dt), S_new.astype(dt), R_new.astype(dt), ffn_new.astype(dt))


def setup_inputs(seed: int = 0) -> dict:
    key = jax.random.key(seed)
    ks = jax.random.split(key, 32)
    f32 = jnp.float32
    nrm = lambda k, shape, s: jax.random.normal(k, shape, f32) * s
    gain = lambda k, shape: 1.0 + 0.05 * jax.random.normal(k, shape, f32)
    return {
        'x_prompt': nrm(ks[0], (BATCH, SEQ, D_MODEL), 1.0),
        'x_sample': nrm(ks[1], (DEC_BATCH, DEC_SEQ, D_MODEL), 1.0),
        'p_prompt': nrm(ks[2], (DEPTH, BATCH, SEQ, PLE_DIM), 1.0),
        'p_sample': nrm(ks[3], (DEPTH, DEC_BATCH, DEC_SEQ, PLE_DIM), 1.0),
        'state_dn_conv': nrm(ks[4], (DEPTH, DEC_BATCH, DN_CONV - 1, 3 * DN_QK), 1.0),
        'state_dn': nrm(ks[5], (DEPTH, DEC_BATCH, DN_HEADS, DN_DK, DN_DV), 0.1),
        'state_ret': nrm(ks[6], (DEPTH, DEC_BATCH, RET_HEADS, RET_DK, RET_DV), 0.5),
        'state_ffn_conv': nrm(ks[7], (DEPTH, DEC_BATCH, FFN_CONV - 1, 2 * D_FF), 1.0),
        'attn_norm_w': gain(ks[8], (DEPTH, D_MODEL)),
        'w_in': nrm(ks[9], (DEPTH, D_MODEL, D_IN), D_MODEL ** -0.5),
        'dn_conv_w': nrm(ks[10], (DEPTH, DN_CONV, 3 * DN_QK), DN_CONV ** -0.5),
        'dn_A_log': jnp.log(jax.random.uniform(ks[11], (DEPTH, DN_HEADS), f32, 1.0, 16.0)),
        'dn_dt_bias': nrm(ks[12], (DEPTH, DN_HEADS), 0.1),
        'dn_norm_w': gain(ks[13], (DEPTH, DN_DV)),
        'ret_norm_w': gain(ks[14], (DEPTH, RET_VD)),
        'w_out': nrm(ks[15], (DEPTH, D_MIX, D_MODEL), D_MIX ** -0.5),
        'ffn_norm_w': gain(ks[16], (DEPTH, D_MODEL)),
        'w_up': nrm(ks[17], (DEPTH, D_MODEL, 2 * D_FF), D_MODEL ** -0.5),
        'ffn_conv_w': nrm(ks[18], (DEPTH, FFN_CONV, 2 * D_FF), FFN_CONV ** -0.5),
        'ffn_conv_b': nrm(ks[19], (DEPTH, 2 * D_FF), 0.02),
        'w_down': nrm(ks[20], (DEPTH, D_FF, D_MODEL), D_FF ** -0.5),
        'ple_norm_w': gain(ks[21], (DEPTH, D_MODEL)),
        'w_ple_gate': nrm(ks[22], (DEPTH, D_MODEL, D_MODEL), D_MODEL ** -0.5),
        'w_ple': nrm(ks[23], (DEPTH, PLE_DIM, D_MODEL), PLE_DIM ** -0.5),
        'final_norm_w': gain(ks[24], (D_MODEL,)),
    }


def reference(x_prompt, x_sample, p_prompt, p_sample, state_dn_conv, state_dn, state_ret,
              state_ffn_conv, attn_norm_w, w_in, dn_conv_w, dn_A_log, dn_dt_bias, dn_norm_w,
              ret_norm_w, w_out, ffn_norm_w, w_up, ffn_conv_w, ffn_conv_b, w_down, ple_norm_w,
              w_ple_gate, w_ple, final_norm_w):
    dt = x_prompt.dtype
    Bp = x_prompt.shape[0]
    hp = x_prompt
    hs = x_sample
    outs_p = ([], [], [], [])
    outs_s = ([], [], [], [])
    for i in range(DEPTH):
        wts = (attn_norm_w[i], w_in[i], dn_conv_w[i], dn_A_log[i], dn_dt_bias[i], dn_norm_w[i],
               ret_norm_w[i], w_out[i], ffn_norm_w[i], w_up[i], ffn_conv_w[i], ffn_conv_b[i],
               w_down[i], ple_norm_w[i], w_ple_gate[i], w_ple[i])
        zc = jnp.zeros((Bp, DN_CONV - 1, 3 * DN_QK), dt)
        zs = jnp.zeros((Bp, DN_HEADS, DN_DK, DN_DV), jnp.float32)
        zr = jnp.zeros((Bp, RET_HEADS, RET_DK, RET_DV), jnp.float32)
        zf = jnp.zeros((Bp, FFN_CONV - 1, 2 * D_FF), dt)
        hp, c1, s1, r1, f1 = _layer(hp, p_prompt[i], zc, zs, zr, zf, 0, *wts)
        hs, c2, s2, r2, f2 = _layer(hs, p_sample[i], state_dn_conv[i], state_dn[i], state_ret[i],
                                    state_ffn_conv[i], PAST_LEN, *wts)
        for lst, val in zip(outs_p, (c1, s1, r1, f1)):
            lst.append(val)
        for lst, val in zip(outs_s, (c2, s2, r2, f2)):
            lst.append(val)
    y_prompt = _rmsnorm(hp, final_norm_w).astype(dt)
    y_sample = _rmsnorm(hs, final_norm_w).astype(dt)
    return (y_prompt, y_sample,
            jnp.stack(outs_p[0]), jnp.stack(outs_p[1]), jnp.stack(outs_p[2]), jnp.stack(outs_p[3]),
            jnp.stack(outs_s[0]), jnp.stack(outs_s[1]), jnp.stack(outs_s[2]), jnp.stack(outs_s[3]))
```

```python
import functools
import math

import jax
import jax.numpy as jnp
from jax import lax
from jax.experimental import pallas as pl
from jax.experimental.pallas import tpu as pltpu

F32 = jnp.float32
BF16 = jnp.bfloat16

D_MODEL = 1024
HEAD_DIM = 128
N_HEADS = 4
QK = N_HEADS * HEAD_DIM
DN_CONV = 4
FFN_CONV = 3
D_FF = 2816
PLE_DIM = 256
CHUNK = 64
PAST_LEN = 16384
ROPE_BASE = 10000.0
EPS = 1e-6
LANES = 128
SUBLANES = 8
GATE_LANE0 = N_HEADS
FF_COLS = 1408
VMEM_LIMIT = 56 * 1024 * 1024


def _mm(a, b):
    return jnp.dot(a.astype(BF16), b.astype(BF16), preferred_element_type=F32)


def _mm_nt(a, b):
    return lax.dot_general(a.astype(BF16), b.astype(BF16), (((1,), (1,)), ((), ())),
                           preferred_element_type=F32)


def _split2(x):
    hi = x.astype(BF16)
    lo = (x - hi.astype(F32)).astype(BF16)
    return hi, lo


def _mm3(a, b):
    ah, al = _split2(a)
    bh, bl = _split2(b)
    d = functools.partial(jnp.dot, preferred_element_type=F32)
    return d(ah, bh) + (d(ah, bl) + d(al, bh))


def _mm_exact_lhs(m_bf16, x):
    hi = x.astype(BF16)
    r1 = x - hi.astype(F32)
    mid = r1.astype(BF16)
    lo = (r1 - mid.astype(F32)).astype(BF16)
    d = functools.partial(jnp.dot, preferred_element_type=F32)
    return d(m_bf16, hi) + (d(m_bf16, mid) + d(m_bf16, lo))


def _rmsnorm(x, w):
    return x * lax.rsqrt(jnp.mean(x * x, axis=-1, keepdims=True) + EPS) * w


def _silu(x):
    return x * jax.nn.sigmoid(x)


def _softplus(x):
    return jnp.maximum(x, 0.0) + jnp.log1p(jnp.exp(-jnp.abs(x)))


def _lane_bcast(x, lane, width=LANES):
    return jnp.broadcast_to(x[:, lane:lane + 1], (x.shape[0], width))


def _unit_lower_inverse(a, eye, chunk):
    p = eye - a
    m = a
    for _ in range(int(math.log2(chunk)) - 1):
        m = _mm3(m, m)
        p = p + _mm3(p, m)
    return p


def _mixer_kernel(cfg, x_ref, cst_ref, s0_ref, r0_ref, cos_ref, sin_ref, dmat_ref, rscale_ref,
                  anw_ref, wmain_ref, wba_ref, convw_ref, gpar_ref, dnw_ref, retw_ref, wout_ref,
                  h_ref, cnew_ref, s_ref, r_ref,
                  e_ref, u_ref, w_ref, qd_ref, vn_ref, oi_ref, qk_ref, ktt_ref, egl_ref,
                  rqd_ref, rktt_ref, roi_ref, rv_ref):
    sb, tt, chunk, cdecay = cfg
    rows = sb * tt
    n_chunks = rows // chunk
    chunks_per_seq = tt // chunk
    shift = int(math.log2(chunk))
    l_idx = pl.program_id(1)

    @pl.when(l_idx == 0)
    def _():
        e_ref[:, SUBLANES - (DN_CONV - 1):SUBLANES, :] = cst_ref[...]
        s_ref[...] = s0_ref[...]
        r_ref[...] = r0_ref[...]

    x = x_ref[...].reshape(rows, D_MODEL)
    a = _rmsnorm(x, anw_ref[...]).astype(BF16)

    e_ref[:, SUBLANES:, :] = jnp.dot(a, wmain_ref[:, 0:3 * QK],
                                     preferred_element_type=F32).reshape(sb, tt, 3 * QK)
    proj = lambda i: jnp.dot(a, wmain_ref[:, (3 + i) * QK:(4 + i) * QK], preferred_element_type=F32)
    z = proj(0)
    rq = proj(1)
    rk = proj(2)
    rv = proj(3)
    rg = proj(4)
    pba = jnp.dot(a, wba_ref[...], preferred_element_type=F32)

    base = SUBLANES - (DN_CONV - 1)
    conv = e_ref[:, base:base + tt, :] * convw_ref[0:1, :]
    for j in range(1, DN_CONV):
        conv = conv + e_ref[:, base + j:base + j + tt, :] * convw_ref[j:j + 1, :]
    tail = e_ref[:, tt + base:tt + SUBLANES, :]
    cnew_ref[...] = tail
    e_ref[:, base:SUBLANES, :] = tail
    qkv = _silu(conv).reshape(rows, 3 * QK)

    beta_all = jax.nn.sigmoid(pba)
    g_all = -jnp.exp(gpar_ref[0:1, :]) * _softplus(pba + gpar_ref[1:2, :])

    ri = lax.broadcasted_iota(jnp.int32, (rows, rows), 0)
    ci = lax.broadcasted_iota(jnp.int32, (rows, rows), 1)
    same = (ri >> shift) == (ci >> shift)
    causal = same & (ci <= ri)
    strict = same & (ci < ri)
    eye = (ri == ci).astype(F32)
    gc_all = _mm_exact_lhs(causal.astype(BF16), g_all)
    gl_all = _mm_exact_lhs(same.astype(BF16), g_all)
    gct_all = gc_all.T

    cos = cos_ref[0]
    sin = sin_ref[0]
    lane = lax.broadcasted_iota(jnp.int32, (rows, LANES), 1)
    even = (lane & 1) == 0

    def rotary(t):
        swapped = jnp.where(even, pltpu.roll(t, LANES - 1, 1), pltpu.roll(t, 1, 1))
        return t * cos + swapped * sin

    for h in range(N_HEADS):
        hs = slice(h * HEAD_DIM, (h + 1) * HEAD_DIM)
        qh = qkv[:, h * HEAD_DIM:(h + 1) * HEAD_DIM]
        kh = qkv[:, QK + h * HEAD_DIM:QK + (h + 1) * HEAD_DIM]
        vh = qkv[:, 2 * QK + h * HEAD_DIM:2 * QK + (h + 1) * HEAD_DIM]
        qb = qh * lax.rsqrt(jnp.sum(qh * qh, axis=-1, keepdims=True) + EPS) * (HEAD_DIM ** -0.5)
        kb = kh * lax.rsqrt(jnp.sum(kh * kh, axis=-1, keepdims=True) + EPS)
        bcol = _lane_bcast(beta_all, h)
        gcc = _lane_bcast(gc_all, GATE_LANE0 + h)
        glc = _lane_bcast(gl_all, GATE_LANE0 + h)
        egc = jnp.exp(gcc)
        delta = (jnp.broadcast_to(gc_all[:, GATE_LANE0 + h:GATE_LANE0 + h + 1], (rows, rows))
                 - jnp.broadcast_to(gct_all[GATE_LANE0 + h:GATE_LANE0 + h + 1, :], (rows, rows)))
        decay = jnp.exp(jnp.where(causal, delta, -jnp.inf))
        kbeta = kb * bcol
        amat = jnp.where(strict, _mm_nt(kbeta, kb) * decay, 0.0)
        tinv = _unit_lower_inverse(amat, eye, chunk)
        sol = _mm3(tinv, jnp.concatenate([vh * bcol, kbeta * egc], axis=1))
        u_ref[:, hs] = sol[:, :HEAD_DIM]
        w_ref[:, hs] = sol[:, HEAD_DIM:]
        qk_ref[h] = _mm_nt(qb, kb) * decay
        qd_ref[:, hs] = qb * egc
        ktt_ref[h] = (kb * jnp.exp(glc - gcc)).T
        egl_ref[h] = jnp.exp(glc)
        rqh = rotary(rq[:, hs])
        rkh = rotary(rk[:, hs]) * (HEAD_DIM ** -0.5)
        rqk = _mm_nt(rqh, rkh) * dmat_ref[h]
        roi_ref[:, hs] = _mm(rqk, rv[:, hs])
        rqd_ref[:, hs] = rqh * rscale_ref[0, :, hs]
        rktt_ref[h] = (rkh * rscale_ref[1, :, hs]).T
    rv_ref[...] = rv

    lane_chunk = lax.broadcasted_iota(jnp.int32, (HEAD_DIM, rows), 1) >> shift

    def chunk_step(c, carry):
        r0 = pl.multiple_of(c * chunk, chunk)
        slot = c // chunks_per_seq
        in_chunk = lane_chunk == c
        for h in range(N_HEADS):
            hs = slice(h * HEAD_DIM, (h + 1) * HEAD_DIM)
            s_old = s_ref[slot, h]
            lhs = jnp.concatenate([w_ref[pl.ds(r0, chunk), hs], qd_ref[pl.ds(r0, chunk), hs]], axis=0)
            res = _mm(lhs, s_old)
            v_new = u_ref[pl.ds(r0, chunk), hs] - res[:chunk]
            oi_ref[pl.ds(r0, chunk), hs] = res[chunk:]
            vn_ref[pl.ds(r0, chunk), hs] = v_new
            egl = jnp.broadcast_to(egl_ref[h, pl.ds(r0, 1), :], (HEAD_DIM, HEAD_DIM))
            s_ref[slot, h] = s_old * egl + _mm(jnp.where(in_chunk, ktt_ref[h], 0.0), vn_ref[:, hs])
            r_old = r_ref[slot, h]
            roi_ref[pl.ds(r0, chunk), hs] += _mm(rqd_ref[pl.ds(r0, chunk), hs], r_old)
            r_ref[slot, h] = r_old * cdecay[h] + _mm(jnp.where(in_chunk, rktt_ref[h], 0.0),
                                                     rv_ref[:, hs])
        return carry

    lax.fori_loop(0, n_chunks, chunk_step, 0)

    outs = []
    for h in range(N_HEADS):
        hs = slice(h * HEAD_DIM, (h + 1) * HEAD_DIM)
        o = oi_ref[:, hs] + _mm(qk_ref[h], vn_ref[:, hs])
        outs.append(_rmsnorm(o, dnw_ref[...]) * _silu(z[:, hs]))
    for h in range(N_HEADS):
        hs = slice(h * HEAD_DIM, (h + 1) * HEAD_DIM)
        o = roi_ref[:, hs]
        mu = jnp.mean(o, axis=-1, keepdims=True)
        var = jnp.mean(jnp.square(o - mu), axis=-1, keepdims=True)
        outs.append((o - mu) * lax.rsqrt(var + EPS) * retw_ref[:, hs] * _silu(rg[:, hs]))
    mix = jnp.concatenate(outs, axis=1).astype(BF16)
    h_new = x + jnp.dot(mix, wout_ref[...], preferred_element_type=F32)
    h_ref[...] = h_new.reshape(sb, tt, D_MODEL)


def _ffn_kernel(cfg, h_ref, p_ref, fst_ref, fnw_ref, wup_ref, fcw_ref, fcb_ref, wdown_ref,
                pnw_ref, wgate_ref, wple_ref, finw_ref,
                y_ref, fnew_ref, e_ref):
    sb, tt, final_norm = cfg
    rows = sb * tt
    l_idx = pl.program_id(1)
    base = SUBLANES - (FFN_CONV - 1)

    @pl.when(l_idx == 0)
    def _():
        e_ref[:, base:SUBLANES, :] = fst_ref[...]

    h = h_ref[...].reshape(rows, D_MODEL)
    m = _rmsnorm(h, fnw_ref[...]).astype(BF16)
    for c in range(2 * D_FF // FF_COLS):
        cs = slice(c * FF_COLS, (c + 1) * FF_COLS)
        e_ref[:, SUBLANES:, cs] = jnp.dot(m, wup_ref[:, cs],
                                          preferred_element_type=F32).reshape(sb, tt, FF_COLS)

    def conv(cs):
        acc = e_ref[:, base:base + tt, cs] * fcw_ref[0:1, cs]
        for j in range(1, FFN_CONV):
            acc = acc + e_ref[:, base + j:base + j + tt, cs] * fcw_ref[j:j + 1, cs]
        return (acc + fcb_ref[:, cs]).reshape(rows, FF_COLS)

    acc = h
    for c in range(D_FF // FF_COLS):
        ug = conv(slice(c * FF_COLS, (c + 1) * FF_COLS))
        uv = conv(slice(D_FF + c * FF_COLS, D_FF + (c + 1) * FF_COLS))
        act = (_silu(ug) * uv).astype(BF16)
        acc = acc + jnp.dot(act, wdown_ref[c * FF_COLS:(c + 1) * FF_COLS, :],
                            preferred_element_type=F32)
    tail = e_ref[:, tt + base:tt + SUBLANES, :]
    fnew_ref[...] = tail
    e_ref[:, base:SUBLANES, :] = tail

    h = acc
    gate = jax.nn.sigmoid(jnp.dot(_rmsnorm(h, pnw_ref[...]).astype(BF16), wgate_ref[...],
                                  preferred_element_type=F32))
    pp = jnp.dot(p_ref[...].reshape(rows, PLE_DIM).astype(BF16), wple_ref[...],
                 preferred_element_type=F32)
    h = h + gate * pp
    if final_norm:
        h = _rmsnorm(h, finw_ref[...])
    y_ref[...] = h.reshape(sb, tt, D_MODEL)


def _const_spec(shape):
    zeros = (0,) * len(shape)
    return pl.BlockSpec(shape, lambda b, l: zeros, pipeline_mode=pl.Buffered(1))


def _retention_tables(chunk, rows):
    hh = jnp.arange(N_HEADS, dtype=F32)
    log_gamma = jnp.log(1.0 - 2.0 ** (-5.0 - hh))
    bpos = (jnp.arange(chunk, dtype=F32) + 1.0)[None, :] * log_gamma[:, None]
    causal = jnp.tril(jnp.ones((chunk, chunk), dtype=bool))
    dmat = jnp.exp(jnp.where(causal, bpos[:, :, None] - bpos[:, None, :], -jnp.inf))
    n = rows // chunk
    dfull = jnp.einsum('ab,hij->haibj', jnp.eye(n, dtype=F32), dmat).reshape(N_HEADS, rows, rows)
    qscale = jnp.exp(bpos)
    kscale = jnp.exp(bpos[:, -1:] - bpos)
    expand = lambda t: jnp.repeat(jnp.tile(t.T, (n, 1)), HEAD_DIM, axis=1)
    rscale = jnp.stack([expand(qscale), expand(kscale)])
    return dfull, rscale


def _rope_tables(pos0, tt, sb, n_l):
    inv = ROPE_BASE ** (-jnp.arange(0, HEAD_DIM, 2, dtype=F32) / HEAD_DIM)
    pos = pos0 + jnp.arange(n_l * tt)
    ang = pos.astype(F32)[:, None] * inv[None, :]
    cos = jnp.repeat(jnp.cos(ang), 2, axis=1)
    sin = jnp.stack([-jnp.sin(ang), jnp.sin(ang)], axis=-1).reshape(n_l * tt, HEAD_DIM)
    tile = lambda t: jnp.tile(t.reshape(n_l, 1, tt, HEAD_DIM), (1, sb, 1, 1)).reshape(
        n_l, sb * tt, HEAD_DIM)
    return tile(cos), tile(sin)


def _mixer(x, cstate, s0, r0, pos0, wts, sb, tt):
    n_b, seq, _ = x.shape
    chunk = CHUNK if seq % CHUNK == 0 else seq
    rows = sb * tt
    n_l = seq // tt
    assert tt % chunk == 0 and n_b % sb == 0 and seq % tt == 0
    anw, wmain, wba, convw, gpar, dnw, retw, wout = wts
    cos, sin = _rope_tables(pos0, tt, sb, n_l)
    dmat, rscale = _retention_tables(chunk, rows)
    cdecay = tuple(float((1.0 - 2.0 ** (-5.0 - h)) ** chunk) for h in range(N_HEADS))
    cfg = (sb, tt, chunk, cdecay)
    state_spec = pl.BlockSpec((sb, N_HEADS, HEAD_DIM, HEAD_DIM), lambda b, l: (b, 0, 0, 0))
    x_spec = pl.BlockSpec((sb, tt, D_MODEL), lambda b, l: (b, l, 0))
    conv_spec = pl.BlockSpec((sb, DN_CONV - 1, 3 * QK), lambda b, l: (b, 0, 0))
    rope_spec = pl.BlockSpec((1, rows, HEAD_DIM), lambda b, l: (l, 0, 0))
    head_rows = pltpu.VMEM((rows, QK), F32)
    head_sq = pltpu.VMEM((N_HEADS, rows, rows), F32)
    head_t = pltpu.VMEM((N_HEADS, HEAD_DIM, rows), F32)
    return pl.pallas_call(
        functools.partial(_mixer_kernel, cfg),
        grid=(n_b // sb, n_l),
        in_specs=[x_spec, conv_spec, state_spec, state_spec, rope_spec, rope_spec,
                  _const_spec(dmat.shape), _const_spec(rscale.shape), _const_spec(anw.shape),
                  _const_spec(wmain.shape), _const_spec(wba.shape), _const_spec(convw.shape),
                  _const_spec(gpar.shape), _const_spec(dnw.shape), _const_spec(retw.shape),
                  _const_spec(wout.shape)],
        out_specs=[x_spec, conv_spec, state_spec, state_spec],
        out_shape=[jax.ShapeDtypeStruct(x.shape, F32),
                   jax.ShapeDtypeStruct(cstate.shape, F32),
                   jax.ShapeDtypeStruct(s0.shape, F32),
                   jax.ShapeDtypeStruct(r0.shape, F32)],
        scratch_shapes=[pltpu.VMEM((sb, SUBLANES + tt, 3 * QK), F32),
                        head_rows, head_rows, head_rows, head_rows, head_rows,
                        head_sq, head_t, pltpu.VMEM((N_HEADS, rows, HEAD_DIM), F32),
                        head_rows, head_t, head_rows, head_rows],
        compiler_params=pltpu.CompilerParams(
            dimension_semantics=("arbitrary", "arbitrary"), vmem_limit_bytes=VMEM_LIMIT),
        name="mixer",
    )(x, cstate, s0, r0, cos, sin, dmat, rscale, anw, wmain, wba, convw, gpar, dnw, retw, wout)


def _ffn(h, p, fstate, wts, final_norm, sb, tt):
    n_b, seq, _ = h.shape
    n_l = seq // tt
    assert n_b % sb == 0 and seq % tt == 0
    fnw, wup, fcw, fcb, wdown, pnw, wgate, wple, finw = wts
    x_spec = pl.BlockSpec((sb, tt, D_MODEL), lambda b, l: (b, l, 0))
    p_spec = pl.BlockSpec((sb, tt, PLE_DIM), lambda b, l: (b, l, 0))
    st_spec = pl.BlockSpec((sb, FFN_CONV - 1, 2 * D_FF), lambda b, l: (b, 0, 0))
    return pl.pallas_call(
        functools.partial(_ffn_kernel, (sb, tt, final_norm)),
        grid=(n_b // sb, n_l),
        in_specs=[x_spec, p_spec, st_spec] + [_const_spec(w.shape) for w in wts],
        out_specs=[x_spec, st_spec],
        out_shape=[jax.ShapeDtypeStruct(h.shape, F32), jax.ShapeDtypeStruct(fstate.shape, F32)],
        scratch_shapes=[pltpu.VMEM((sb, SUBLANES + tt, 2 * D_FF), F32)],
        compiler_params=pltpu.CompilerParams(
            dimension_semantics=("arbitrary", "arbitrary"), vmem_limit_bytes=VMEM_LIMIT),
        name="ffn",
    )(h, p, fstate, *wts)


def kernel(x_prompt, x_sample, p_prompt, p_sample, state_dn_conv, state_dn, state_ret, state_ffn_conv, attn_norm_w, w_in, dn_conv_w, dn_A_log, dn_dt_bias, dn_norm_w, ret_norm_w, w_out, ffn_norm_w, w_up, ffn_conv_w, ffn_conv_b, w_down, ple_norm_w, w_ple_gate, w_ple, final_norm_w):
    depth = w_in.shape[0]
    n_bp = x_prompt.shape[0]
    row = lambda v: v.reshape(1, -1).astype(F32)
    hp, hs = x_prompt, x_sample
    outs_p = ([], [], [], [])
    outs_s = ([], [], [], [])
    for i in range(depth):
        wi = w_in[i]
        wmain = jnp.concatenate([wi[:, :4 * QK], wi[:, 4 * QK + 2 * N_HEADS:]], axis=1).astype(BF16)
        wba = jnp.pad(wi[:, 4 * QK:4 * QK + 2 * N_HEADS],
                      ((0, 0), (0, LANES - 2 * N_HEADS))).astype(BF16)
        gpar = jnp.zeros((2, LANES), F32)
        gpar = gpar.at[0, GATE_LANE0:GATE_LANE0 + N_HEADS].set(dn_A_log[i])
        gpar = gpar.at[1, GATE_LANE0:GATE_LANE0 + N_HEADS].set(dn_dt_bias[i])
        mixer_w = (row(attn_norm_w[i]), wmain, wba, dn_conv_w[i], gpar, row(dn_norm_w[i]),
                   row(ret_norm_w[i]), w_out[i].astype(BF16))
        ffn_w = (row(ffn_norm_w[i]), w_up[i].astype(BF16), ffn_conv_w[i], row(ffn_conv_b[i]),
                 w_down[i].astype(BF16), row(ple_norm_w[i]), w_ple_gate[i].astype(BF16),
                 w_ple[i].astype(BF16), row(final_norm_w))
        last = i == depth - 1
        zc = jnp.zeros((n_bp, DN_CONV - 1, 3 * QK), F32)
        zs = jnp.zeros((n_bp, N_HEADS, HEAD_DIM, HEAD_DIM), F32)
        zf = jnp.zeros((n_bp, FFN_CONV - 1, 2 * D_FF), F32)
        hp, c1, s1, r1 = _mixer(hp, zc, zs, zs, 0, mixer_w, sb=1, tt=256)
        hp, f1 = _ffn(hp, p_prompt[i], zf, ffn_w, last, sb=1, tt=256)
        hs, c2, s2, r2 = _mixer(hs, state_dn_conv[i], state_dn[i], state_ret[i], PAST_LEN,
                                mixer_w, sb=8, tt=x_sample.shape[1])
        hs, f2 = _ffn(hs, p_sample[i], state_ffn_conv[i], ffn_w, last, sb=16, tt=x_sample.shape[1])
        for lst, val in zip(outs_p, (c1, s1, r1, f1)):
            lst.append(val)
        for lst, val in zip(outs_s, (c2, s2, r2, f2)):
            lst.append(val)
    return (hp, hs,
            jnp.stack(outs_p[0]), jnp.stack(outs_p[1]), jnp.stack(outs_p[2]), jnp.stack(outs_p[3]),
            jnp.stack(outs_s[0]), jnp.stack(outs_s[1]), jnp.stack(outs_s[2]), jnp.stack(outs_s[3]))
```

```python
import functools
import math

import jax
import jax.numpy as jnp
from jax import lax
from jax.experimental import pallas as pl
from jax.experimental.pallas import tpu as pltpu

F32 = jnp.float32
BF16 = jnp.bfloat16

D_MODEL = 1024
HEAD_DIM = 128
N_HEADS = 4
QK = N_HEADS * HEAD_DIM
DN_CONV = 4
FFN_CONV = 3
D_FF = 2816
PLE_DIM = 256
CHUNK = 64
PAST_LEN = 16384
ROPE_BASE = 10000.0
EPS = 1e-6
LANES = 128
SUBLANES = 8
GATE_LANE0 = N_HEADS
FF_COLS = 1408
VMEM_LIMIT = 56 * 1024 * 1024


def _mm(a, b):
    return jnp.dot(a.astype(BF16), b.astype(BF16), preferred_element_type=F32)


def _mm_nt(a, b):
    return lax.dot_general(a.astype(BF16), b.astype(BF16), (((1,), (1,)), ((), ())),
                           preferred_element_type=F32)


def _split2(x):
    hi = x.astype(BF16)
    lo = (x - hi.astype(F32)).astype(BF16)
    return hi, lo


def _mm3(a, b):
    ah, al = _split2(a)
    bh, bl = _split2(b)
    d = functools.partial(jnp.dot, preferred_element_type=F32)
    return d(ah, bh) + (d(ah, bl) + d(al, bh))


def _mm_exact_lhs(m_bf16, x):
    hi = x.astype(BF16)
    r1 = x - hi.astype(F32)
    mid = r1.astype(BF16)
    lo = (r1 - mid.astype(F32)).astype(BF16)
    d = functools.partial(jnp.dot, preferred_element_type=F32)
    return d(m_bf16, hi) + (d(m_bf16, mid) + d(m_bf16, lo))


def _rmsnorm(x, w):
    return x * lax.rsqrt(jnp.mean(x * x, axis=-1, keepdims=True) + EPS) * w


def _silu(x):
    return x * jax.nn.sigmoid(x)


def _softplus(x):
    return jnp.maximum(x, 0.0) + jnp.log1p(jnp.exp(-jnp.abs(x)))


def _lane_bcast(x, lane, width=LANES):
    return jnp.broadcast_to(x[:, lane:lane + 1], (x.shape[0], width))


def _unit_lower_inverse(a, eye, chunk):
    p = eye - a
    m = a
    for _ in range(int(math.log2(chunk)) - 1):
        m = _mm3(m, m)
        p = p + _mm3(p, m)
    return p


def _mixer_kernel(cfg, x_ref, cst_ref, s0_ref, r0_ref, cos_ref, sin_ref, dmat_ref, rscale_ref,
                  anw_ref, wmain_ref, wba_ref, convw_ref, gpar_ref, dnw_ref, retw_ref, wout_ref,
                  h_ref, cnew_ref, s_ref, r_ref,
                  e_ref, u_ref, w_ref, qd_ref, vn_ref, oi_ref, qk_ref, ktt_ref, egl_ref,
                  rqd_ref, rktt_ref, roi_ref, rv_ref):
    sb, tt, chunk, cdecay = cfg
    rows = sb * tt
    n_chunks = rows // chunk
    chunks_per_seq = tt // chunk
    shift = int(math.log2(chunk))
    l_idx = pl.program_id(1)

    @pl.when(l_idx == 0)
    def _():
        e_ref[:, SUBLANES - (DN_CONV - 1):SUBLANES, :] = cst_ref[...]
        s_ref[...] = s0_ref[...]
        r_ref[...] = r0_ref[...]

    x = x_ref[...].reshape(rows, D_MODEL)
    a = _rmsnorm(x, anw_ref[...]).astype(BF16)

    e_ref[:, SUBLANES:, :] = jnp.dot(a, wmain_ref[:, 0:3 * QK],
                                     preferred_element_type=F32).reshape(sb, tt, 3 * QK)
    proj = lambda i: jnp.dot(a, wmain_ref[:, (3 + i) * QK:(4 + i) * QK], preferred_element_type=F32)
    z = proj(0)
    rq = proj(1)
    rk = proj(2)
    rv = proj(3)
    rg = proj(4)
    pba = jnp.dot(a, wba_ref[...], preferred_element_type=F32)

    base = SUBLANES - (DN_CONV - 1)
    conv = e_ref[:, base:base + tt, :] * convw_ref[0:1, :]
    for j in range(1, DN_CONV):
        conv = conv + e_ref[:, base + j:base + j + tt, :] * convw_ref[j:j + 1, :]
    tail = e_ref[:, tt + base:tt + SUBLANES, :]
    cnew_ref[...] = tail
    e_ref[:, base:SUBLANES, :] = tail
    qkv = _silu(conv).reshape(rows, 3 * QK)

    beta_all = jax.nn.sigmoid(pba)
    g_all = -jnp.exp(gpar_ref[0:1, :]) * _softplus(pba + gpar_ref[1:2, :])

    ri = lax.broadcasted_iota(jnp.int32, (rows, rows), 0)
    ci = lax.broadcasted_iota(jnp.int32, (rows, rows), 1)
    same = (ri >> shift) == (ci >> shift)
    causal = same & (ci <= ri)
    strict = same & (ci < ri)
    eye = (ri == ci).astype(F32)
    gc_all = _mm_exact_lhs(causal.astype(BF16), g_all)
    gl_all = _mm_exact_lhs(same.astype(BF16), g_all)
    gct_all = gc_all.T

    cos = cos_ref[0]
    sin = sin_ref[0]
    lane = lax.broadcasted_iota(jnp.int32, (rows, LANES), 1)
    even = (lane & 1) == 0

    def rotary(t):
        swapped = jnp.where(even, pltpu.roll(t, LANES - 1, 1), pltpu.roll(t, 1, 1))
        return t * cos + swapped * sin

    for h in range(N_HEADS):
        hs = slice(h * HEAD_DIM, (h + 1) * HEAD_DIM)
        qh = qkv[:, h * HEAD_DIM:(h + 1) * HEAD_DIM]
        kh = qkv[:, QK + h * HEAD_DIM:QK + (h + 1) * HEAD_DIM]
        vh = qkv[:, 2 * QK + h * HEAD_DIM:2 * QK + (h + 1) * HEAD_DIM]
        qb = qh * lax.rsqrt(jnp.sum(qh * qh, axis=-1, keepdims=True) + EPS) * (HEAD_DIM ** -0.5)
        kb = kh * lax.rsqrt(jnp.sum(kh * kh, axis=-1, keepdims=True) + EPS)
        bcol = _lane_bcast(beta_all, h)
        gcc = _lane_bcast(gc_all, GATE_LANE0 + h)
        glc = _lane_bcast(gl_all, GATE_LANE0 + h)
        egc = jnp.exp(gcc)
        delta = (jnp.broadcast_to(gc_all[:, GATE_LANE0 + h:GATE_LANE0 + h + 1], (rows, rows))
                 - jnp.broadcast_to(gct_all[GATE_LANE0 + h:GATE_LANE0 + h + 1, :], (rows, rows)))
        decay = jnp.exp(jnp.where(causal, delta, -jnp.inf))
        kbeta = kb * bcol
        amat = jnp.where(strict, _mm_nt(kbeta, kb) * decay, 0.0)
        tinv = _unit_lower_inverse(amat, eye, chunk)
        sol = _mm3(tinv, jnp.concatenate([vh * bcol, kbeta * egc], axis=1))
        u_ref[:, hs] = sol[:, :HEAD_DIM]
        w_ref[:, hs] = sol[:, HEAD_DIM:]
        qk_ref[h] = _mm_nt(qb, kb) * decay
        qd_ref[:, hs] = qb * egc
        ktt_ref[h] = (kb * jnp.exp(glc - gcc)).T
        egl_ref[h] = jnp.exp(glc)
        rqh = rotary(rq[:, hs])
        rkh = rotary(rk[:, hs]) * (HEAD_DIM ** -0.5)
        rqk = _mm_nt(rqh, rkh) * dmat_ref[h]
        roi_ref[:, hs] = _mm(rqk, rv[:, hs])
        rqd_ref[:, hs] = rqh * rscale_ref[0, :, hs]
        rktt_ref[h] = (rkh * rscale_ref[1, :, hs]).T
    rv_ref[...] = rv
    vn_ref[...] = jnp.zeros_like(vn_ref)

    lane_chunk = lax.broadcasted_iota(jnp.int32, (HEAD_DIM, rows), 1) >> shift

    def chunk_step(c, carry):
        r0 = pl.multiple_of(c * chunk, chunk)
        slot = c // chunks_per_seq
        in_chunk = lane_chunk == c
        for h in range(N_HEADS):
            hs = slice(h * HEAD_DIM, (h + 1) * HEAD_DIM)
            s_old = s_ref[slot, h]
            lhs = jnp.concatenate([w_ref[pl.ds(r0, chunk), hs], qd_ref[pl.ds(r0, chunk), hs]], axis=0)
            res = _mm(lhs, s_old)
            v_new = u_ref[pl.ds(r0, chunk), hs] - res[:chunk]
            oi_ref[pl.ds(r0, chunk), hs] = res[chunk:]
            vn_ref[pl.ds(r0, chunk), hs] = v_new
            egl = jnp.broadcast_to(egl_ref[h, pl.ds(r0, 1), :], (HEAD_DIM, HEAD_DIM))
            s_ref[slot, h] = s_old * egl + _mm(jnp.where(in_chunk, ktt_ref[h], 0.0), vn_ref[:, hs])
            r_old = r_ref[slot, h]
            roi_ref[pl.ds(r0, chunk), hs] += _mm(rqd_ref[pl.ds(r0, chunk), hs], r_old)
            r_ref[slot, h] = r_old * cdecay[h] + _mm(jnp.where(in_chunk, rktt_ref[h], 0.0),
                                                     rv_ref[:, hs])
        return carry

    lax.fori_loop(0, n_chunks, chunk_step, 0)

    outs = []
    for h in range(N_HEADS):
        hs = slice(h * HEAD_DIM, (h + 1) * HEAD_DIM)
        o = oi_ref[:, hs] + _mm(qk_ref[h], vn_ref[:, hs])
        outs.append(_rmsnorm(o, dnw_ref[...]) * _silu(z[:, hs]))
    for h in range(N_HEADS):
        hs = slice(h * HEAD_DIM, (h + 1) * HEAD_DIM)
        o = roi_ref[:, hs]
        mu = jnp.mean(o, axis=-1, keepdims=True)
        var = jnp.mean(jnp.square(o - mu), axis=-1, keepdims=True)
        outs.append((o - mu) * lax.rsqrt(var + EPS) * retw_ref[:, hs] * _silu(rg[:, hs]))
    mix = jnp.concatenate(outs, axis=1).astype(BF16)
    h_new = x + jnp.dot(mix, wout_ref[...], preferred_element_type=F32)
    h_ref[...] = h_new.reshape(sb, tt, D_MODEL)


def _bmm(a, b):
    return lax.dot_general(a, b, (((2,), (1,)), ((0,), (0,))), preferred_element_type=F32)


def _bmm_nt(a, b):
    return lax.dot_general(a, b, (((2,), (2,)), ((0,), (0,))), preferred_element_type=F32)


def _bmm_tn(a, b):
    return lax.dot_general(a, b, (((1,), (1,)), ((0,), (0,))), preferred_element_type=F32)


def _pack_lhs(m):
    hi = m.astype(BF16).astype(F32)
    return jnp.concatenate([hi, m - hi, hi], axis=-1).astype(BF16)


def _pack_rhs(m):
    hi, lo = _split2(m)
    return jnp.concatenate([hi, hi, lo], axis=-2)


def _unit_lower_inverse_packed(a, eye, n_steps):
    c = a.shape[-1]
    upper = lax.broadcasted_iota(jnp.int32, a.shape[:-1] + (2 * c,), a.ndim - 1) >= c
    a2 = _bmm(_pack_lhs(a), _pack_rhs(a))
    mp = jnp.concatenate([a2, eye - a], axis=-1)
    for _ in range(n_steps):
        res = _bmm(_pack_lhs(mp[..., :c]), _pack_rhs(mp))
        mp = res + jnp.where(upper, mp, 0.0)
    return mp[..., c:]


def _mixer_seq_kernel(cfg, x_ref, cst_ref, s0_ref, r0_ref, cos_ref, sin_ref, tri_ref, dmat_ref,
                      rscale_ref, cdec_ref, anw_ref, wmain_ref, wba_ref, convw_ref, gpar_ref,
                      dnw_ref, retw_ref, wout_ref, h_ref, cnew_ref, s_ref, r_ref, e_ref):
    tt, chunk = cfg
    n_chunks = tt // chunk
    l_idx = pl.program_id(1)

    @pl.when(l_idx == 0)
    def _():
        e_ref[:, SUBLANES - (DN_CONV - 1):SUBLANES, :] = cst_ref[...]
        s_ref[...] = s0_ref[...]
        r_ref[...] = r0_ref[...]

    x = x_ref[0]
    a = _rmsnorm(x, anw_ref[...]).astype(BF16)
    e_ref[0, SUBLANES:, :] = jnp.dot(a, wmain_ref[:, 0:3 * QK], preferred_element_type=F32)
    proj = lambda i: jnp.dot(a, wmain_ref[:, (3 + i) * QK:(4 + i) * QK], preferred_element_type=F32)
    z = proj(0)
    rq = proj(1)
    rk = proj(2)
    rv = proj(3)
    rg = proj(4)
    pba = jnp.dot(a, wba_ref[...], preferred_element_type=F32)

    base = SUBLANES - (DN_CONV - 1)
    conv = e_ref[0, base:base + tt, :] * convw_ref[0:1, :]
    for j in range(1, DN_CONV):
        conv = conv + e_ref[0, base + j:base + j + tt, :] * convw_ref[j:j + 1, :]
    tail = e_ref[:, tt + base:tt + SUBLANES, :]
    cnew_ref[...] = tail
    e_ref[:, base:SUBLANES, :] = tail
    qkv = _silu(conv)

    beta_all = jax.nn.sigmoid(pba)
    g_all = -jnp.exp(gpar_ref[0:1, :]) * _softplus(pba + gpar_ref[1:2, :])
    gc_all = _mm_exact_lhs(tri_ref[0], g_all)
    gl_all = _mm_exact_lhs(tri_ref[1], g_all)
    gct_all = gc_all.T

    ri = lax.broadcasted_iota(jnp.int32, (chunk, chunk), 0)
    ci = lax.broadcasted_iota(jnp.int32, (chunk, chunk), 1)
    causal = ci <= ri
    strict = ci < ri
    eye = (ri == ci).astype(F32)
    n_steps = int(math.log2(chunk)) - 1

    cos = cos_ref[0]
    sin = sin_ref[0]
    even = (lax.broadcasted_iota(jnp.int32, (tt, LANES), 1) & 1) == 0

    def rotary(t):
        swapped = jnp.where(even, pltpu.roll(t, LANES - 1, 1), pltpu.roll(t, 1, 1))
        return t * cos + swapped * sin

    per = {k: [] for k in ("qb", "kb", "kbeta", "rhs", "qd", "kt", "gcc", "gct", "egl",
                           "rq", "rk", "rqd", "rkt", "rv")}
    chunks = lambda t: t.reshape(n_chunks, chunk, t.shape[-1])
    for h in range(N_HEADS):
        hs = slice(h * HEAD_DIM, (h + 1) * HEAD_DIM)
        qh = qkv[:, h * HEAD_DIM:(h + 1) * HEAD_DIM]
        kh = qkv[:, QK + h * HEAD_DIM:QK + (h + 1) * HEAD_DIM]
        vh = qkv[:, 2 * QK + h * HEAD_DIM:2 * QK + (h + 1) * HEAD_DIM]
        qb = qh * lax.rsqrt(jnp.sum(qh * qh, axis=-1, keepdims=True) + EPS) * (HEAD_DIM ** -0.5)
        kb = kh * lax.rsqrt(jnp.sum(kh * kh, axis=-1, keepdims=True) + EPS)
        bcol = _lane_bcast(beta_all, h)
        gcc = _lane_bcast(gc_all, GATE_LANE0 + h)
        glc = _lane_bcast(gl_all, GATE_LANE0 + h)
        egc = jnp.exp(gcc)
        kbeta = kb * bcol
        rqh = rotary(rq[:, hs])
        rkh = rotary(rk[:, hs]) * (HEAD_DIM ** -0.5)
        per["qb"].append(chunks(qb))
        per["kb"].append(chunks(kb))
        per["kbeta"].append(chunks(kbeta))
        per["rhs"].append(chunks(jnp.concatenate([kbeta * egc, vh * bcol], axis=1)))
        per["qd"].append(chunks(qb * egc))
        per["kt"].append(chunks(kb * jnp.exp(glc - gcc)))
        per["gcc"].append(chunks(gcc))
        per["egl"].append(chunks(jnp.exp(glc))[:, 0:1, :])
        per["gct"].extend(gct_all[GATE_LANE0 + h:GATE_LANE0 + h + 1, c * chunk:(c + 1) * chunk][None]
                          for c in range(n_chunks))
        per["rq"].append(chunks(rqh))
        per["rk"].append(chunks(rkh))
        per["rqd"].append(chunks(rqh * rscale_ref[0, :, hs]))
        per["rkt"].append(chunks(rkh * rscale_ref[1, :, hs]))
        per["rv"].append(chunks(rv[:, hs]))
    bat = {k: jnp.concatenate(v, axis=0) for k, v in per.items()}
    bf = lambda k: bat[k].astype(BF16)
    n_bat = N_HEADS * n_chunks
    by_head = lambda t: t.reshape((N_HEADS, n_chunks) + t.shape[1:])

    delta = bat["gcc"][:, :, :chunk] - bat["gct"]
    decay = jnp.exp(jnp.where(causal, delta, -jnp.inf))
    kb16 = bf("kb")
    amat = jnp.where(strict, _bmm_nt(bf("kbeta"), kb16) * decay, 0.0)
    tinv = _unit_lower_inverse_packed(amat, eye, n_steps)
    wu = _bmm(_pack_lhs(tinv), _pack_rhs(bat["rhs"])).astype(BF16)
    qk = (_bmm_nt(bf("qb"), kb16) * decay).astype(BF16)
    gh = by_head(_bmm_tn(bf("kt"), wu))
    qo = _bmm(qk, wu)
    qp = by_head((bat["qd"] - qo[:, :, :HEAD_DIM]).astype(BF16))
    op = by_head(qo[:, :, HEAD_DIM:])
    egl = by_head(bat["egl"])
    rv16 = bf("rv")
    dmat = jnp.broadcast_to(dmat_ref[...][:, None], (N_HEADS, n_chunks, chunk, chunk))
    rqk = _bmm_nt(bf("rq"), bf("rk")) * dmat.reshape(n_bat, chunk, chunk)
    ro = by_head(_bmm(rqk.astype(BF16), rv16))
    rh = by_head(_bmm_tn(bf("rkt"), rv16))
    rqd = by_head(bf("rqd"))

    s_cur = s_ref[0]
    r_cur = r_ref[0]
    o_dn = []
    o_ret = []
    for c in range(n_chunks):
        s16 = s_cur.astype(BF16)
        o_dn.append(op[:, c] + _bmm(qp[:, c], s16))
        s_cur = (s_cur * egl[:, c] - _bmm(gh[:, c, :, :HEAD_DIM].astype(BF16), s16)
                 + gh[:, c, :, HEAD_DIM:])
        o_ret.append(ro[:, c] + _bmm(rqd[:, c], r_cur.astype(BF16)))
        r_cur = r_cur * cdec_ref[...] + rh[:, c]
    s_ref[0] = s_cur
    r_ref[0] = r_cur

    outs = []
    for h in range(N_HEADS):
        hs = slice(h * HEAD_DIM, (h + 1) * HEAD_DIM)
        o = jnp.concatenate([o_c[h] for o_c in o_dn], axis=0)
        outs.append(_rmsnorm(o, dnw_ref[...]) * _silu(z[:, hs]))
    for h in range(N_HEADS):
        hs = slice(h * HEAD_DIM, (h + 1) * HEAD_DIM)
        o = jnp.concatenate([o_c[h] for o_c in o_ret], axis=0)
        mu = jnp.mean(o, axis=-1, keepdims=True)
        var = jnp.mean(jnp.square(o - mu), axis=-1, keepdims=True)
        outs.append((o - mu) * lax.rsqrt(var + EPS) * retw_ref[:, hs] * _silu(rg[:, hs]))
    mix = jnp.concatenate(outs, axis=1).astype(BF16)
    h_ref[0] = x + jnp.dot(mix, wout_ref[...], preferred_element_type=F32)


def _ffn_kernel(cfg, h_ref, p_ref, fst_ref, fnw_ref, wup_ref, fcw_ref, fcb_ref, wdown_ref,
                pnw_ref, wgate_ref, wple_ref, finw_ref,
                y_ref, fnew_ref, e_ref):
    sb, tt, final_norm = cfg
    rows = sb * tt
    l_idx = pl.program_id(1)
    base = SUBLANES - (FFN_CONV - 1)

    @pl.when(l_idx == 0)
    def _():
        e_ref[:, base:SUBLANES, :] = fst_ref[...]

    h = h_ref[...].reshape(rows, D_MODEL)
    m = _rmsnorm(h, fnw_ref[...]).astype(BF16)
    for c in range(2 * D_FF // FF_COLS):
        cs = slice(c * FF_COLS, (c + 1) * FF_COLS)
        e_ref[:, SUBLANES:, cs] = jnp.dot(m, wup_ref[:, cs],
                                          preferred_element_type=F32).reshape(sb, tt, FF_COLS)

    def conv(cs):
        acc = e_ref[:, base:base + tt, cs] * fcw_ref[0:1, cs]
        for j in range(1, FFN_CONV):
            acc = acc + e_ref[:, base + j:base + j + tt, cs] * fcw_ref[j:j + 1, cs]
        return (acc + fcb_ref[:, cs]).reshape(rows, FF_COLS)

    acc = h
    for c in range(D_FF // FF_COLS):
        ug = conv(slice(c * FF_COLS, (c + 1) * FF_COLS))
        uv = conv(slice(D_FF + c * FF_COLS, D_FF + (c + 1) * FF_COLS))
        act = (_silu(ug) * uv).astype(BF16)
        acc = acc + jnp.dot(act, wdown_ref[c * FF_COLS:(c + 1) * FF_COLS, :],
                            preferred_element_type=F32)
    tail = e_ref[:, tt + base:tt + SUBLANES, :]
    fnew_ref[...] = tail
    e_ref[:, base:SUBLANES, :] = tail

    h = acc
    gate = jax.nn.sigmoid(jnp.dot(_rmsnorm(h, pnw_ref[...]).astype(BF16), wgate_ref[...],
                                  preferred_element_type=F32))
    pp = jnp.dot(p_ref[...].reshape(rows, PLE_DIM).astype(BF16), wple_ref[...],
                 preferred_element_type=F32)
    h = h + gate * pp
    if final_norm:
        h = _rmsnorm(h, finw_ref[...])
    y_ref[...] = h.reshape(sb, tt, D_MODEL)


def _const_spec(shape):
    zeros = (0,) * len(shape)
    return pl.BlockSpec(shape, lambda b, l: zeros, pipeline_mode=pl.Buffered(1))


def _retention_tables(chunk, rows):
    hh = jnp.arange(N_HEADS, dtype=F32)
    log_gamma = jnp.log(1.0 - 2.0 ** (-5.0 - hh))
    bpos = (jnp.arange(chunk, dtype=F32) + 1.0)[None, :] * log_gamma[:, None]
    causal = jnp.tril(jnp.ones((chunk, chunk), dtype=bool))
    dmat = jnp.exp(jnp.where(causal, bpos[:, :, None] - bpos[:, None, :], -jnp.inf))
    n = rows // chunk
    dfull = jnp.einsum('ab,hij->haibj', jnp.eye(n, dtype=F32), dmat).reshape(N_HEADS, rows, rows)
    qscale = jnp.exp(bpos)
    kscale = jnp.exp(bpos[:, -1:] - bpos)
    expand = lambda t: jnp.repeat(jnp.tile(t.T, (n, 1)), HEAD_DIM, axis=1)
    rscale = jnp.stack([expand(qscale), expand(kscale)])
    return dmat, dfull, rscale


def _chunk_sum_matrices(chunk, rows):
    r = jnp.arange(rows)
    same = (r[:, None] // chunk) == (r[None, :] // chunk)
    causal = same & (r[None, :] <= r[:, None])
    return jnp.stack([causal, same]).astype(BF16)


def _rope_tables(pos0, tt, sb, n_l):
    inv = ROPE_BASE ** (-jnp.arange(0, HEAD_DIM, 2, dtype=F32) / HEAD_DIM)
    pos = pos0 + jnp.arange(n_l * tt)
    ang = pos.astype(F32)[:, None] * inv[None, :]
    cos = jnp.repeat(jnp.cos(ang), 2, axis=1)
    sin = jnp.stack([-jnp.sin(ang), jnp.sin(ang)], axis=-1).reshape(n_l * tt, HEAD_DIM)
    tile = lambda t: jnp.tile(t.reshape(n_l, 1, tt, HEAD_DIM), (1, sb, 1, 1)).reshape(
        n_l, sb * tt, HEAD_DIM)
    return tile(cos), tile(sin)


def _mixer(x, cstate, s0, r0, pos0, wts, sb, tt):
    n_b, seq, _ = x.shape
    chunk = CHUNK if seq % CHUNK == 0 else seq
    rows = sb * tt
    n_l = seq // tt
    assert tt % chunk == 0 and n_b % sb == 0 and seq % tt == 0
    anw, wmain, wba, convw, gpar, dnw, retw, wout = wts
    cos, sin = _rope_tables(pos0, tt, sb, n_l)
    _, dmat, rscale = _retention_tables(chunk, rows)
    cdecay = tuple(float((1.0 - 2.0 ** (-5.0 - h)) ** chunk) for h in range(N_HEADS))
    cfg = (sb, tt, chunk, cdecay)
    state_spec = pl.BlockSpec((sb, N_HEADS, HEAD_DIM, HEAD_DIM), lambda b, l: (b, 0, 0, 0))
    x_spec = pl.BlockSpec((sb, tt, D_MODEL), lambda b, l: (b, l, 0))
    conv_spec = pl.BlockSpec((sb, DN_CONV - 1, 3 * QK), lambda b, l: (b, 0, 0))
    rope_spec = pl.BlockSpec((1, rows, HEAD_DIM), lambda b, l: (l, 0, 0))
    head_rows = pltpu.VMEM((rows, QK), F32)
    head_sq = pltpu.VMEM((N_HEADS, rows, rows), F32)
    head_t = pltpu.VMEM((N_HEADS, HEAD_DIM, rows), F32)
    return pl.pallas_call(
        functools.partial(_mixer_kernel, cfg),
        grid=(n_b // sb, n_l),
        in_specs=[x_spec, conv_spec, state_spec, state_spec, rope_spec, rope_spec,
                  _const_spec(dmat.shape), _const_spec(rscale.shape), _const_spec(anw.shape),
                  _const_spec(wmain.shape), _const_spec(wba.shape), _const_spec(convw.shape),
                  _const_spec(gpar.shape), _const_spec(dnw.shape), _const_spec(retw.shape),
                  _const_spec(wout.shape)],
        out_specs=[x_spec, conv_spec, state_spec, state_spec],
        out_shape=[jax.ShapeDtypeStruct(x.shape, F32),
                   jax.ShapeDtypeStruct(cstate.shape, F32),
                   jax.ShapeDtypeStruct(s0.shape, F32),
                   jax.ShapeDtypeStruct(r0.shape, F32)],
        scratch_shapes=[pltpu.VMEM((sb, SUBLANES + tt, 3 * QK), F32),
                        head_rows, head_rows, head_rows, head_rows, head_rows,
                        head_sq, head_t, pltpu.VMEM((N_HEADS, rows, HEAD_DIM), F32),
                        head_rows, head_t, head_rows, head_rows],
        compiler_params=pltpu.CompilerParams(
            dimension_semantics=("arbitrary", "arbitrary"), vmem_limit_bytes=VMEM_LIMIT),
        name="mixer",
    )(x, cstate, s0, r0, cos, sin, dmat, rscale, anw, wmain, wba, convw, gpar, dnw, retw, wout)


def _mixer_seq(x, cstate, s0, r0, pos0, wts, tt):
    n_b, seq, _ = x.shape
    chunk = CHUNK if seq % CHUNK == 0 else seq
    n_l = seq // tt
    assert tt % chunk == 0 and seq % tt == 0
    anw, wmain, wba, convw, gpar, dnw, retw, wout = wts
    cos, sin = _rope_tables(pos0, tt, 1, n_l)
    dmat, _, rscale = _retention_tables(chunk, tt)
    cdec = jnp.broadcast_to(rscale[0, chunk - 1].reshape(N_HEADS, 1, HEAD_DIM),
                            (N_HEADS, 1, HEAD_DIM))
    tri = _chunk_sum_matrices(chunk, tt)
    state_spec = pl.BlockSpec((1, N_HEADS, HEAD_DIM, HEAD_DIM), lambda b, l: (b, 0, 0, 0))
    x_spec = pl.BlockSpec((1, tt, D_MODEL), lambda b, l: (b, l, 0))
    conv_spec = pl.BlockSpec((1, DN_CONV - 1, 3 * QK), lambda b, l: (b, 0, 0))
    rope_spec = pl.BlockSpec((1, tt, HEAD_DIM), lambda b, l: (l, 0, 0))
    consts = (tri, dmat, rscale, cdec, anw, wmain, wba, convw, gpar, dnw, retw, wout)
    return pl.pallas_call(
        functools.partial(_mixer_seq_kernel, (tt, chunk)),
        grid=(n_b, n_l),
        in_specs=[x_spec, conv_spec, state_spec, state_spec, rope_spec, rope_spec]
        + [_const_spec(c.shape) for c in consts],
        out_specs=[x_spec, conv_spec, state_spec, state_spec],
        out_shape=[jax.ShapeDtypeStruct(x.shape, F32),
                   jax.ShapeDtypeStruct(cstate.shape, F32),
                   jax.ShapeDtypeStruct(s0.shape, F32),
                   jax.ShapeDtypeStruct(r0.shape, F32)],
        scratch_shapes=[pltpu.VMEM((1, SUBLANES + tt, 3 * QK), F32)],
        compiler_params=pltpu.CompilerParams(
            dimension_semantics=("arbitrary", "arbitrary"), vmem_limit_bytes=VMEM_LIMIT),
        name="mixer_seq",
    )(x, cstate, s0, r0, cos, sin, *consts)


def _ffn(h, p, fstate, wts, final_norm, sb, tt):
    n_b, seq, _ = h.shape
    n_l = seq // tt
    assert n_b % sb == 0 and seq % tt == 0
    fnw, wup, fcw, fcb, wdown, pnw, wgate, wple, finw = wts
    x_spec = pl.BlockSpec((sb, tt, D_MODEL), lambda b, l: (b, l, 0))
    p_spec = pl.BlockSpec((sb, tt, PLE_DIM), lambda b, l: (b, l, 0))
    st_spec = pl.BlockSpec((sb, FFN_CONV - 1, 2 * D_FF), lambda b, l: (b, 0, 0))
    return pl.pallas_call(
        functools.partial(_ffn_kernel, (sb, tt, final_norm)),
        grid=(n_b // sb, n_l),
        in_specs=[x_spec, p_spec, st_spec] + [_const_spec(w.shape) for w in wts],
        out_specs=[x_spec, st_spec],
        out_shape=[jax.ShapeDtypeStruct(h.shape, F32), jax.ShapeDtypeStruct(fstate.shape, F32)],
        scratch_shapes=[pltpu.VMEM((sb, SUBLANES + tt, 2 * D_FF), F32)],
        compiler_params=pltpu.CompilerParams(
            dimension_semantics=("arbitrary", "arbitrary"), vmem_limit_bytes=VMEM_LIMIT),
        name="ffn",
    )(h, p, fstate, *wts)


def kernel(x_prompt, x_sample, p_prompt, p_sample, state_dn_conv, state_dn, state_ret, state_ffn_conv, attn_norm_w, w_in, dn_conv_w, dn_A_log, dn_dt_bias, dn_norm_w, ret_norm_w, w_out, ffn_norm_w, w_up, ffn_conv_w, ffn_conv_b, w_down, ple_norm_w, w_ple_gate, w_ple, final_norm_w):
    depth = w_in.shape[0]
    n_bp = x_prompt.shape[0]
    row = lambda v: v.reshape(1, -1).astype(F32)
    hp, hs = x_prompt, x_sample
    outs_p = ([], [], [], [])
    outs_s = ([], [], [], [])
    for i in range(depth):
        wi = w_in[i]
        wmain = jnp.concatenate([wi[:, :4 * QK], wi[:, 4 * QK + 2 * N_HEADS:]], axis=1).astype(BF16)
        wba = jnp.pad(wi[:, 4 * QK:4 * QK + 2 * N_HEADS],
                      ((0, 0), (0, LANES - 2 * N_HEADS))).astype(BF16)
        gpar = jnp.zeros((2, LANES), F32)
        gpar = gpar.at[0, GATE_LANE0:GATE_LANE0 + N_HEADS].set(dn_A_log[i])
        gpar = gpar.at[1, GATE_LANE0:GATE_LANE0 + N_HEADS].set(dn_dt_bias[i])
        mixer_w = (row(attn_norm_w[i]), wmain, wba, dn_conv_w[i], gpar, row(dn_norm_w[i]),
                   row(ret_norm_w[i]), w_out[i].astype(BF16))
        ffn_w = (row(ffn_norm_w[i]), w_up[i].astype(BF16), ffn_conv_w[i], row(ffn_conv_b[i]),
                 w_down[i].astype(BF16), row(ple_norm_w[i]), w_ple_gate[i].astype(BF16),
                 w_ple[i].astype(BF16), row(final_norm_w))
        last = i == depth - 1
        zc = jnp.zeros((n_bp, DN_CONV - 1, 3 * QK), F32)
        zs = jnp.zeros((n_bp, N_HEADS, HEAD_DIM, HEAD_DIM), F32)
        zf = jnp.zeros((n_bp, FFN_CONV - 1, 2 * D_FF), F32)
        hp, c1, s1, r1 = _mixer_seq(hp, zc, zs, zs, 0, mixer_w, tt=256)
        hp, f1 = _ffn(hp, p_prompt[i], zf, ffn_w, last, sb=1, tt=256)
        hs, c2, s2, r2 = _mixer(hs, state_dn_conv[i], state_dn[i], state_ret[i], PAST_LEN,
                                mixer_w, sb=8, tt=x_sample.shape[1])
        hs, f2 = _ffn(hs, p_sample[i], state_ffn_conv[i], ffn_w, last, sb=16, tt=x_sample.shape[1])
        for lst, val in zip(outs_p, (c1, s1, r1, f1)):
            lst.append(val)
        for lst, val in zip(outs_s, (c2, s2, r2, f2)):
            lst.append(val)
    return (hp, hs,
            jnp.stack(outs_p[0]), jnp.stack(outs_p[1]), jnp.stack(outs_p[2]), jnp.stack(outs_p[3]),
            jnp.stack(outs_s[0]), jnp.stack(outs_s[1]), jnp.stack(outs_s[2]), jnp.stack(outs_s[3]))
```

```python
import functools
import math

import jax
import jax.numpy as jnp
from jax import lax
from jax.experimental import pallas as pl
from jax.experimental.pallas import tpu as pltpu

F32 = jnp.float32
BF16 = jnp.bfloat16

D_MODEL = 1024
HEAD_DIM = 128
N_HEADS = 4
QK = N_HEADS * HEAD_DIM
DN_CONV = 4
FFN_CONV = 3
D_FF = 2816
PLE_DIM = 256
CHUNK = 64
PAST_LEN = 16384
ROPE_BASE = 10000.0
EPS = 1e-6
LANES = 128
SUBLANES = 8
GATE_LANE0 = N_HEADS
FF_COLS = 1408
VMEM_LIMIT = 56 * 1024 * 1024


def _mm(a, b):
    return jnp.dot(a.astype(BF16), b.astype(BF16), preferred_element_type=F32)


def _mm_nt(a, b):
    return lax.dot_general(a.astype(BF16), b.astype(BF16), (((1,), (1,)), ((), ())),
                           preferred_element_type=F32)


def _split2(x):
    hi = x.astype(BF16)
    lo = (x - hi.astype(F32)).astype(BF16)
    return hi, lo


def _mm3(a, b):
    ah, al = _split2(a)
    bh, bl = _split2(b)
    d = functools.partial(jnp.dot, preferred_element_type=F32)
    return d(ah, bh) + (d(ah, bl) + d(al, bh))


def _mm_exact_lhs(m_bf16, x):
    hi = x.astype(BF16)
    r1 = x - hi.astype(F32)
    mid = r1.astype(BF16)
    lo = (r1 - mid.astype(F32)).astype(BF16)
    d = functools.partial(jnp.dot, preferred_element_type=F32)
    return d(m_bf16, hi) + (d(m_bf16, mid) + d(m_bf16, lo))


def _rmsnorm(x, w):
    return x * lax.rsqrt(jnp.mean(x * x, axis=-1, keepdims=True) + EPS) * w


def _silu(x):
    return x * jax.nn.sigmoid(x)


def _softplus(x):
    return jnp.maximum(x, 0.0) + jnp.log1p(jnp.exp(-jnp.abs(x)))


def _lane_bcast(x, lane, width=LANES):
    return jnp.broadcast_to(x[:, lane:lane + 1], (x.shape[0], width))


def _unit_lower_inverse(a, eye, chunk):
    p = eye - a
    m = a
    for _ in range(int(math.log2(chunk)) - 1):
        m = _mm3(m, m)
        p = p + _mm3(p, m)
    return p


def _mixer_kernel(cfg, x_ref, cst_ref, s0_ref, r0_ref, cos_ref, sin_ref, dmat_ref, rscale_ref,
                  anw_ref, wmain_ref, wba_ref, convw_ref, gpar_ref, dnw_ref, retw_ref, wout_ref,
                  h_ref, cnew_ref, s_ref, r_ref,
                  e_ref, u_ref, w_ref, qd_ref, vn_ref, oi_ref, qk_ref, ktt_ref, egl_ref,
                  rqd_ref, rktt_ref, roi_ref, rv_ref):
    sb, tt, chunk, cdecay = cfg
    rows = sb * tt
    n_chunks = rows // chunk
    chunks_per_seq = tt // chunk
    shift = int(math.log2(chunk))
    l_idx = pl.program_id(1)

    @pl.when(l_idx == 0)
    def _():
        e_ref[:, SUBLANES - (DN_CONV - 1):SUBLANES, :] = cst_ref[...]
        s_ref[...] = s0_ref[...]
        r_ref[...] = r0_ref[...]

    x = x_ref[...].reshape(rows, D_MODEL)
    a = _rmsnorm(x, anw_ref[...]).astype(BF16)

    e_ref[:, SUBLANES:, :] = jnp.dot(a, wmain_ref[:, 0:3 * QK],
                                     preferred_element_type=F32).reshape(sb, tt, 3 * QK)
    proj = lambda i: jnp.dot(a, wmain_ref[:, (3 + i) * QK:(4 + i) * QK], preferred_element_type=F32)
    z = proj(0)
    rq = proj(1)
    rk = proj(2)
    rv = proj(3)
    rg = proj(4)
    pba = jnp.dot(a, wba_ref[...], preferred_element_type=F32)

    base = SUBLANES - (DN_CONV - 1)
    conv = e_ref[:, base:base + tt, :] * convw_ref[0:1, :]
    for j in range(1, DN_CONV):
        conv = conv + e_ref[:, base + j:base + j + tt, :] * convw_ref[j:j + 1, :]
    tail = e_ref[:, tt + base:tt + SUBLANES, :]
    cnew_ref[...] = tail
    e_ref[:, base:SUBLANES, :] = tail
    qkv = _silu(conv).reshape(rows, 3 * QK)

    beta_all = jax.nn.sigmoid(pba)
    g_all = -jnp.exp(gpar_ref[0:1, :]) * _softplus(pba + gpar_ref[1:2, :])

    ri = lax.broadcasted_iota(jnp.int32, (rows, rows), 0)
    ci = lax.broadcasted_iota(jnp.int32, (rows, rows), 1)
    same = (ri >> shift) == (ci >> shift)
    causal = same & (ci <= ri)
    strict = same & (ci < ri)
    eye = (ri == ci).astype(F32)
    gc_all = _mm_exact_lhs(causal.astype(BF16), g_all)
    gl_all = _mm_exact_lhs(same.astype(BF16), g_all)
    gct_all = gc_all.T

    cos = cos_ref[0]
    sin = sin_ref[0]
    lane = lax.broadcasted_iota(jnp.int32, (rows, LANES), 1)
    even = (lane & 1) == 0

    def rotary(t):
        swapped = jnp.where(even, pltpu.roll(t, LANES - 1, 1), pltpu.roll(t, 1, 1))
        return t * cos + swapped * sin

    for h in range(N_HEADS):
        hs = slice(h * HEAD_DIM, (h + 1) * HEAD_DIM)
        qh = qkv[:, h * HEAD_DIM:(h + 1) * HEAD_DIM]
        kh = qkv[:, QK + h * HEAD_DIM:QK + (h + 1) * HEAD_DIM]
        vh = qkv[:, 2 * QK + h * HEAD_DIM:2 * QK + (h + 1) * HEAD_DIM]
        qb = qh * lax.rsqrt(jnp.sum(qh * qh, axis=-1, keepdims=True) + EPS) * (HEAD_DIM ** -0.5)
        kb = kh * lax.rsqrt(jnp.sum(kh * kh, axis=-1, keepdims=True) + EPS)
        bcol = _lane_bcast(beta_all, h)
        gcc = _lane_bcast(gc_all, GATE_LANE0 + h)
        glc = _lane_bcast(gl_all, GATE_LANE0 + h)
        egc = jnp.exp(gcc)
        delta = (jnp.broadcast_to(gc_all[:, GATE_LANE0 + h:GATE_LANE0 + h + 1], (rows, rows))
                 - jnp.broadcast_to(gct_all[GATE_LANE0 + h:GATE_LANE0 + h + 1, :], (rows, rows)))
        decay = jnp.exp(jnp.where(causal, delta, -jnp.inf))
        kbeta = kb * bcol
        amat = jnp.where(strict, _mm_nt(kbeta, kb) * decay, 0.0)
        tinv = _unit_lower_inverse(amat, eye, chunk)
        sol = _mm3(tinv, jnp.concatenate([vh * bcol, kbeta * egc], axis=1))
        u_ref[:, hs] = sol[:, :HEAD_DIM]
        w_ref[:, hs] = sol[:, HEAD_DIM:]
        qk_ref[h] = _mm_nt(qb, kb) * decay
        qd_ref[:, hs] = qb * egc
        ktt_ref[h] = (kb * jnp.exp(glc - gcc)).T
        egl_ref[h] = jnp.exp(glc)
        rqh = rotary(rq[:, hs])
        rkh = rotary(rk[:, hs]) * (HEAD_DIM ** -0.5)
        rqk = _mm_nt(rqh, rkh) * dmat_ref[h]
        roi_ref[:, hs] = _mm(rqk, rv[:, hs])
        rqd_ref[:, hs] = rqh * rscale_ref[0, :, hs]
        rktt_ref[h] = (rkh * rscale_ref[1, :, hs]).T
    rv_ref[...] = rv
    vn_ref[...] = jnp.zeros_like(vn_ref)

    lane_chunk = lax.broadcasted_iota(jnp.int32, (HEAD_DIM, rows), 1) >> shift

    def chunk_step(c, carry):
        r0 = pl.multiple_of(c * chunk, chunk)
        slot = c // chunks_per_seq
        in_chunk = lane_chunk == c
        for h in range(N_HEADS):
            hs = slice(h * HEAD_DIM, (h + 1) * HEAD_DIM)
            s_old = s_ref[slot, h]
            lhs = jnp.concatenate([w_ref[pl.ds(r0, chunk), hs], qd_ref[pl.ds(r0, chunk), hs]], axis=0)
            res = _mm(lhs, s_old)
            v_new = u_ref[pl.ds(r0, chunk), hs] - res[:chunk]
            oi_ref[pl.ds(r0, chunk), hs] = res[chunk:]
            vn_ref[pl.ds(r0, chunk), hs] = v_new
            egl = jnp.broadcast_to(egl_ref[h, pl.ds(r0, 1), :], (HEAD_DIM, HEAD_DIM))
            s_ref[slot, h] = s_old * egl + _mm(jnp.where(in_chunk, ktt_ref[h], 0.0), vn_ref[:, hs])
            r_old = r_ref[slot, h]
            roi_ref[pl.ds(r0, chunk), hs] += _mm(rqd_ref[pl.ds(r0, chunk), hs], r_old)
            r_ref[slot, h] = r_old * cdecay[h] + _mm(jnp.where(in_chunk, rktt_ref[h], 0.0),
                                                     rv_ref[:, hs])
        return carry

    lax.fori_loop(0, n_chunks, chunk_step, 0)

    outs = []
    for h in range(N_HEADS):
        hs = slice(h * HEAD_DIM, (h + 1) * HEAD_DIM)
        o = oi_ref[:, hs] + _mm(qk_ref[h], vn_ref[:, hs])
        outs.append(_rmsnorm(o, dnw_ref[...]) * _silu(z[:, hs]))
    for h in range(N_HEADS):
        hs = slice(h * HEAD_DIM, (h + 1) * HEAD_DIM)
        o = roi_ref[:, hs]
        mu = jnp.mean(o, axis=-1, keepdims=True)
        var = jnp.mean(jnp.square(o - mu), axis=-1, keepdims=True)
        outs.append((o - mu) * lax.rsqrt(var + EPS) * retw_ref[:, hs] * _silu(rg[:, hs]))
    mix = jnp.concatenate(outs, axis=1).astype(BF16)
    h_new = x + jnp.dot(mix, wout_ref[...], preferred_element_type=F32)
    h_ref[...] = h_new.reshape(sb, tt, D_MODEL)


def _bmm(a, b):
    return lax.dot_general(a, b, (((2,), (1,)), ((0,), (0,))), preferred_element_type=F32)


def _bmm_nt(a, b):
    return lax.dot_general(a, b, (((2,), (2,)), ((0,), (0,))), preferred_element_type=F32)


def _bmm_tn(a, b):
    return lax.dot_general(a, b, (((1,), (1,)), ((0,), (0,))), preferred_element_type=F32)


def _pack_lhs(m):
    hi = m.astype(BF16).astype(F32)
    return jnp.concatenate([hi, m - hi, hi], axis=-1).astype(BF16)


def _pack_rhs(m):
    hi, lo = _split2(m)
    return jnp.concatenate([hi, hi, lo], axis=-2)


def _unit_lower_inverse_packed(a, eye, n_steps):
    c = a.shape[-1]
    upper = lax.broadcasted_iota(jnp.int32, a.shape[:-1] + (2 * c,), a.ndim - 1) >= c
    a2 = _bmm(_pack_lhs(a), _pack_rhs(a))
    mp = jnp.concatenate([a2, eye - a], axis=-1)
    for _ in range(n_steps):
        res = _bmm(_pack_lhs(mp[..., :c]), _pack_rhs(mp))
        mp = res + jnp.where(upper, mp, 0.0)
    return mp[..., c:]


def _mixer_seq_kernel(cfg, x_ref, cst_ref, s0_ref, r0_ref, cos_ref, sin_ref, tri_ref, dmat_ref,
                      rscale_ref, cdec_ref, anw_ref, wmain_ref, wba_ref, convw_ref, gpar_ref,
                      dnw_ref, retw_ref, wout_ref, h_ref, cnew_ref, s_ref, r_ref, e_ref):
    tt, chunk = cfg
    n_chunks = tt // chunk
    l_idx = pl.program_id(1)

    @pl.when(l_idx == 0)
    def _():
        e_ref[:, SUBLANES - (DN_CONV - 1):SUBLANES, :] = cst_ref[...]
        s_ref[...] = s0_ref[...]
        r_ref[...] = r0_ref[...]

    x = x_ref[0]
    a = _rmsnorm(x, anw_ref[...]).astype(BF16)
    e_ref[0, SUBLANES:, :] = jnp.dot(a, wmain_ref[:, 0:3 * QK], preferred_element_type=F32)
    proj = lambda i: jnp.dot(a, wmain_ref[:, (3 + i) * QK:(4 + i) * QK], preferred_element_type=F32)
    pba = jnp.dot(a, wba_ref[...], preferred_element_type=F32)

    base = SUBLANES - (DN_CONV - 1)
    conv = e_ref[0, base:base + tt, :] * convw_ref[0:1, :]
    for j in range(1, DN_CONV):
        conv = conv + e_ref[0, base + j:base + j + tt, :] * convw_ref[j:j + 1, :]
    tail = e_ref[:, tt + base:tt + SUBLANES, :]
    cnew_ref[...] = tail
    e_ref[:, base:SUBLANES, :] = tail
    rq = proj(1)
    qkv = _silu(conv)
    rk = proj(2)

    beta_all = jax.nn.sigmoid(pba)
    g_all = -jnp.exp(gpar_ref[0:1, :]) * _softplus(pba + gpar_ref[1:2, :])
    gc_all = _mm_exact_lhs(tri_ref[0], g_all)
    gl_all = _mm_exact_lhs(tri_ref[1], g_all)
    gct_all = gc_all.T

    ri = lax.broadcasted_iota(jnp.int32, (chunk, chunk), 0)
    ci = lax.broadcasted_iota(jnp.int32, (chunk, chunk), 1)
    causal = ci <= ri
    strict = ci < ri
    eye = (ri == ci).astype(F32)
    n_steps = int(math.log2(chunk)) - 1

    cos = cos_ref[0]
    sin = sin_ref[0]
    even = (lax.broadcasted_iota(jnp.int32, (tt, LANES), 1) & 1) == 0

    def rotary(t):
        swapped = jnp.where(even, pltpu.roll(t, LANES - 1, 1), pltpu.roll(t, 1, 1))
        return t * cos + swapped * sin

    per = {k: [] for k in ("qb", "kb", "kbeta", "rhs", "qd", "kt", "gcc", "gct", "egl",
                           "rq", "rk", "rqd", "rkt", "rv")}
    chunks = lambda t: t.reshape(n_chunks, chunk, t.shape[-1])
    for h in range(N_HEADS):
        hs = slice(h * HEAD_DIM, (h + 1) * HEAD_DIM)
        qh = qkv[:, h * HEAD_DIM:(h + 1) * HEAD_DIM]
        kh = qkv[:, QK + h * HEAD_DIM:QK + (h + 1) * HEAD_DIM]
        vh = qkv[:, 2 * QK + h * HEAD_DIM:2 * QK + (h + 1) * HEAD_DIM]
        qb = qh * lax.rsqrt(jnp.sum(qh * qh, axis=-1, keepdims=True) + EPS) * (HEAD_DIM ** -0.5)
        kb = kh * lax.rsqrt(jnp.sum(kh * kh, axis=-1, keepdims=True) + EPS)
        bcol = _lane_bcast(beta_all, h)
        gcc = _lane_bcast(gc_all, GATE_LANE0 + h)
        glc = _lane_bcast(gl_all, GATE_LANE0 + h)
        egc = jnp.exp(gcc)
        kbeta = kb * bcol
        per["qb"].append(chunks(qb))
        per["kb"].append(chunks(kb))
        per["kbeta"].append(chunks(kbeta))
        per["rhs"].append(chunks(jnp.concatenate([kbeta * egc, vh * bcol], axis=1)))
        per["qd"].append(chunks(qb * egc))
        per["kt"].append(chunks(kb * jnp.exp(glc - gcc)))
        per["gcc"].append(chunks(gcc))
        per["egl"].append(chunks(jnp.exp(glc))[:, 0:1, :])
        per["gct"].extend(gct_all[GATE_LANE0 + h:GATE_LANE0 + h + 1, c * chunk:(c + 1) * chunk][None]
                          for c in range(n_chunks))
    bat = {k: jnp.concatenate(v, axis=0) for k, v in per.items() if v}
    bf = lambda k: bat[k].astype(BF16)
    n_bat = N_HEADS * n_chunks
    by_head = lambda t: t.reshape((N_HEADS, n_chunks) + t.shape[1:])
    rv = proj(3)

    delta = bat["gcc"][:, :, :chunk] - bat["gct"]
    decay = jnp.exp(jnp.where(causal, delta, -jnp.inf))
    kb16 = bf("kb")
    amat = jnp.where(strict, _bmm_nt(bf("kbeta"), kb16) * decay, 0.0)
    tinv = _unit_lower_inverse_packed(amat, eye, n_steps)
    z = proj(0)
    for h in range(N_HEADS):
        hs = slice(h * HEAD_DIM, (h + 1) * HEAD_DIM)
        rqh = rotary(rq[:, hs])
        rkh = rotary(rk[:, hs]) * (HEAD_DIM ** -0.5)
        per["rq"].append(chunks(rqh))
        per["rk"].append(chunks(rkh))
        per["rqd"].append(chunks(rqh * rscale_ref[0, :, hs]))
        per["rkt"].append(chunks(rkh * rscale_ref[1, :, hs]))
        per["rv"].append(chunks(rv[:, hs]))
    bat.update({k: jnp.concatenate(per[k], axis=0) for k in ("rq", "rk", "rqd", "rkt", "rv")})
    rg = proj(4)
    wu = _bmm(_pack_lhs(tinv), _pack_rhs(bat["rhs"])).astype(BF16)
    qk = (_bmm_nt(bf("qb"), kb16) * decay).astype(BF16)
    gh = by_head(_bmm_tn(bf("kt"), wu))
    qo = _bmm(qk, wu)
    qp = by_head((bat["qd"] - qo[:, :, :HEAD_DIM]).astype(BF16))
    op = by_head(qo[:, :, HEAD_DIM:])
    egl = by_head(bat["egl"])
    rv16 = bf("rv")
    dmat = jnp.broadcast_to(dmat_ref[...][:, None], (N_HEADS, n_chunks, chunk, chunk))
    rqk = _bmm_nt(bf("rq"), bf("rk")) * dmat.reshape(n_bat, chunk, chunk)
    ro = by_head(_bmm(rqk.astype(BF16), rv16))
    rh = by_head(_bmm_tn(bf("rkt"), rv16))
    rqd = by_head(bf("rqd"))

    s_cur = s_ref[0]
    r_cur = r_ref[0]
    o_dn = []
    o_ret = []
    for c in range(n_chunks):
        s16 = s_cur.astype(BF16)
        o_dn.append(op[:, c] + _bmm(qp[:, c], s16))
        s_cur = (s_cur * egl[:, c] - _bmm(gh[:, c, :, :HEAD_DIM].astype(BF16), s16)
                 + gh[:, c, :, HEAD_DIM:])
        o_ret.append(ro[:, c] + _bmm(rqd[:, c], r_cur.astype(BF16)))
        r_cur = r_cur * cdec_ref[...] + rh[:, c]
    s_ref[0] = s_cur
    r_ref[0] = r_cur

    outs = []
    for h in range(N_HEADS):
        hs = slice(h * HEAD_DIM, (h + 1) * HEAD_DIM)
        o = jnp.concatenate([o_c[h] for o_c in o_dn], axis=0)
        outs.append(_rmsnorm(o, dnw_ref[...]) * _silu(z[:, hs]))
    for h in range(N_HEADS):
        hs = slice(h * HEAD_DIM, (h + 1) * HEAD_DIM)
        o = jnp.concatenate([o_c[h] for o_c in o_ret], axis=0)
        mu = jnp.mean(o, axis=-1, keepdims=True)
        var = jnp.mean(jnp.square(o - mu), axis=-1, keepdims=True)
        outs.append((o - mu) * lax.rsqrt(var + EPS) * retw_ref[:, hs] * _silu(rg[:, hs]))
    mix = jnp.concatenate(outs, axis=1).astype(BF16)
    h_ref[0] = x + jnp.dot(mix, wout_ref[...], preferred_element_type=F32)


def _ffn_kernel(cfg, h_ref, p_ref, fst_ref, fnw_ref, wup_ref, fcw_ref, fcb_ref, wdown_ref,
                pnw_ref, wgate_ref, wple_ref, finw_ref,
                y_ref, fnew_ref, e_ref):
    sb, tt, final_norm = cfg
    rows = sb * tt
    l_idx = pl.program_id(1)
    base = SUBLANES - (FFN_CONV - 1)

    @pl.when(l_idx == 0)
    def _():
        e_ref[:, base:SUBLANES, :] = fst_ref[...]

    h = h_ref[...].reshape(rows, D_MODEL)
    m = _rmsnorm(h, fnw_ref[...]).astype(BF16)
    for c in range(2 * D_FF // FF_COLS):
        cs = slice(c * FF_COLS, (c + 1) * FF_COLS)
        e_ref[:, SUBLANES:, cs] = jnp.dot(m, wup_ref[:, cs],
                                          preferred_element_type=F32).reshape(sb, tt, FF_COLS)

    def conv(cs):
        acc = e_ref[:, base:base + tt, cs] * fcw_ref[0:1, cs]
        for j in range(1, FFN_CONV):
            acc = acc + e_ref[:, base + j:base + j + tt, cs] * fcw_ref[j:j + 1, cs]
        return (acc + fcb_ref[:, cs]).reshape(rows, FF_COLS)

    acc = h
    for c in range(D_FF // FF_COLS):
        ug = conv(slice(c * FF_COLS, (c + 1) * FF_COLS))
        uv = conv(slice(D_FF + c * FF_COLS, D_FF + (c + 1) * FF_COLS))
        act = (_silu(ug) * uv).astype(BF16)
        acc = acc + jnp.dot(act, wdown_ref[c * FF_COLS:(c + 1) * FF_COLS, :],
                            preferred_element_type=F32)
    tail = e_ref[:, tt + base:tt + SUBLANES, :]
    fnew_ref[...] = tail
    e_ref[:, base:SUBLANES, :] = tail

    h = acc
    gate = jax.nn.sigmoid(jnp.dot(_rmsnorm(h, pnw_ref[...]).astype(BF16), wgate_ref[...],
                                  preferred_element_type=F32))
    pp = jnp.dot(p_ref[...].reshape(rows, PLE_DIM).astype(BF16), wple_ref[...],
                 preferred_element_type=F32)
    h = h + gate * pp
    if final_norm:
        h = _rmsnorm(h, finw_ref[...])
    y_ref[...] = h.reshape(sb, tt, D_MODEL)


def _const_spec(shape):
    zeros = (0,) * len(shape)
    return pl.BlockSpec(shape, lambda b, l: zeros, pipeline_mode=pl.Buffered(1))


def _retention_tables(chunk, rows):
    hh = jnp.arange(N_HEADS, dtype=F32)
    log_gamma = jnp.log(1.0 - 2.0 ** (-5.0 - hh))
    bpos = (jnp.arange(chunk, dtype=F32) + 1.0)[None, :] * log_gamma[:, None]
    causal = jnp.tril(jnp.ones((chunk, chunk), dtype=bool))
    dmat = jnp.exp(jnp.where(causal, bpos[:, :, None] - bpos[:, None, :], -jnp.inf))
    n = rows // chunk
    dfull = jnp.einsum('ab,hij->haibj', jnp.eye(n, dtype=F32), dmat).reshape(N_HEADS, rows, rows)
    qscale = jnp.exp(bpos)
    kscale = jnp.exp(bpos[:, -1:] - bpos)
    expand = lambda t: jnp.repeat(jnp.tile(t.T, (n, 1)), HEAD_DIM, axis=1)
    rscale = jnp.stack([expand(qscale), expand(kscale)])
    return dmat, dfull, rscale


def _chunk_sum_matrices(chunk, rows):
    r = jnp.arange(rows)
    same = (r[:, None] // chunk) == (r[None, :] // chunk)
    causal = same & (r[None, :] <= r[:, None])
    return jnp.stack([causal, same]).astype(BF16)


def _rope_tables(pos0, tt, sb, n_l):
    inv = ROPE_BASE ** (-jnp.arange(0, HEAD_DIM, 2, dtype=F32) / HEAD_DIM)
    pos = pos0 + jnp.arange(n_l * tt)
    ang = pos.astype(F32)[:, None] * inv[None, :]
    cos = jnp.repeat(jnp.cos(ang), 2, axis=1)
    sin = jnp.stack([-jnp.sin(ang), jnp.sin(ang)], axis=-1).reshape(n_l * tt, HEAD_DIM)
    tile = lambda t: jnp.tile(t.reshape(n_l, 1, tt, HEAD_DIM), (1, sb, 1, 1)).reshape(
        n_l, sb * tt, HEAD_DIM)
    return tile(cos), tile(sin)


def _mixer(x, cstate, s0, r0, pos0, wts, sb, tt):
    n_b, seq, _ = x.shape
    chunk = CHUNK if seq % CHUNK == 0 else seq
    rows = sb * tt
    n_l = seq // tt
    assert tt % chunk == 0 and n_b % sb == 0 and seq % tt == 0
    anw, wmain, wba, convw, gpar, dnw, retw, wout = wts
    cos, sin = _rope_tables(pos0, tt, sb, n_l)
    _, dmat, rscale = _retention_tables(chunk, rows)
    cdecay = tuple(float((1.0 - 2.0 ** (-5.0 - h)) ** chunk) for h in range(N_HEADS))
    cfg = (sb, tt, chunk, cdecay)
    state_spec = pl.BlockSpec((sb, N_HEADS, HEAD_DIM, HEAD_DIM), lambda b, l: (b, 0, 0, 0))
    x_spec = pl.BlockSpec((sb, tt, D_MODEL), lambda b, l: (b, l, 0))
    conv_spec = pl.BlockSpec((sb, DN_CONV - 1, 3 * QK), lambda b, l: (b, 0, 0))
    rope_spec = pl.BlockSpec((1, rows, HEAD_DIM), lambda b, l: (l, 0, 0))
    head_rows = pltpu.VMEM((rows, QK), F32)
    head_sq = pltpu.VMEM((N_HEADS, rows, rows), F32)
    head_t = pltpu.VMEM((N_HEADS, HEAD_DIM, rows), F32)
    return pl.pallas_call(
        functools.partial(_mixer_kernel, cfg),
        grid=(n_b // sb, n_l),
        in_specs=[x_spec, conv_spec, state_spec, state_spec, rope_spec, rope_spec,
                  _const_spec(dmat.shape), _const_spec(rscale.shape), _const_spec(anw.shape),
                  _const_spec(wmain.shape), _const_spec(wba.shape), _const_spec(convw.shape),
                  _const_spec(gpar.shape), _const_spec(dnw.shape), _const_spec(retw.shape),
                  _const_spec(wout.shape)],
        out_specs=[x_spec, conv_spec, state_spec, state_spec],
        out_shape=[jax.ShapeDtypeStruct(x.shape, F32),
                   jax.ShapeDtypeStruct(cstate.shape, F32),
                   jax.ShapeDtypeStruct(s0.shape, F32),
                   jax.ShapeDtypeStruct(r0.shape, F32)],
        scratch_shapes=[pltpu.VMEM((sb, SUBLANES + tt, 3 * QK), F32),
                        head_rows, head_rows, head_rows, head_rows, head_rows,
                        head_sq, head_t, pltpu.VMEM((N_HEADS, rows, HEAD_DIM), F32),
                        head_rows, head_t, head_rows, head_rows],
        compiler_params=pltpu.CompilerParams(
            dimension_semantics=("arbitrary", "arbitrary"), vmem_limit_bytes=VMEM_LIMIT),
        name="mixer",
    )(x, cstate, s0, r0, cos, sin, dmat, rscale, anw, wmain, wba, convw, gpar, dnw, retw, wout)


def _mixer_seq(x, cstate, s0, r0, pos0, wts, tt):
    n_b, seq, _ = x.shape
    chunk = CHUNK if seq % CHUNK == 0 else seq
    n_l = seq // tt
    assert tt % chunk == 0 and seq % tt == 0
    anw, wmain, wba, convw, gpar, dnw, retw, wout = wts
    cos, sin = _rope_tables(pos0, tt, 1, n_l)
    dmat, _, rscale = _retention_tables(chunk, tt)
    cdec = jnp.broadcast_to(rscale[0, chunk - 1].reshape(N_HEADS, 1, HEAD_DIM),
                            (N_HEADS, 1, HEAD_DIM))
    tri = _chunk_sum_matrices(chunk, tt)
    state_spec = pl.BlockSpec((1, N_HEADS, HEAD_DIM, HEAD_DIM), lambda b, l: (b, 0, 0, 0))
    x_spec = pl.BlockSpec((1, tt, D_MODEL), lambda b, l: (b, l, 0))
    conv_spec = pl.BlockSpec((1, DN_CONV - 1, 3 * QK), lambda b, l: (b, 0, 0))
    rope_spec = pl.BlockSpec((1, tt, HEAD_DIM), lambda b, l: (l, 0, 0))
    consts = (tri, dmat, rscale, cdec, anw, wmain, wba, convw, gpar, dnw, retw, wout)
    return pl.pallas_call(
        functools.partial(_mixer_seq_kernel, (tt, chunk)),
        grid=(n_b, n_l),
        in_specs=[x_spec, conv_spec, state_spec, state_spec, rope_spec, rope_spec]
        + [_const_spec(c.shape) for c in consts],
        out_specs=[x_spec, conv_spec, state_spec, state_spec],
        out_shape=[jax.ShapeDtypeStruct(x.shape, F32),
                   jax.ShapeDtypeStruct(cstate.shape, F32),
                   jax.ShapeDtypeStruct(s0.shape, F32),
                   jax.ShapeDtypeStruct(r0.shape, F32)],
        scratch_shapes=[pltpu.VMEM((1, SUBLANES + tt, 3 * QK), F32)],
        compiler_params=pltpu.CompilerParams(
            dimension_semantics=("arbitrary", "arbitrary"), vmem_limit_bytes=VMEM_LIMIT),
        name="mixer_seq",
    )(x, cstate, s0, r0, cos, sin, *consts)


def _ffn(h, p, fstate, wts, final_norm, sb, tt):
    n_b, seq, _ = h.shape
    n_l = seq // tt
    assert n_b % sb == 0 and seq % tt == 0
    fnw, wup, fcw, fcb, wdown, pnw, wgate, wple, finw = wts
    x_spec = pl.BlockSpec((sb, tt, D_MODEL), lambda b, l: (b, l, 0))
    p_spec = pl.BlockSpec((sb, tt, PLE_DIM), lambda b, l: (b, l, 0))
    st_spec = pl.BlockSpec((sb, FFN_CONV - 1, 2 * D_FF), lambda b, l: (b, 0, 0))
    return pl.pallas_call(
        functools.partial(_ffn_kernel, (sb, tt, final_norm)),
        grid=(n_b // sb, n_l),
        in_specs=[x_spec, p_spec, st_spec] + [_const_spec(w.shape) for w in wts],
        out_specs=[x_spec, st_spec],
        out_shape=[jax.ShapeDtypeStruct(h.shape, F32), jax.ShapeDtypeStruct(fstate.shape, F32)],
        scratch_shapes=[pltpu.VMEM((sb, SUBLANES + tt, 2 * D_FF), F32)],
        compiler_params=pltpu.CompilerParams(
            dimension_semantics=("arbitrary", "arbitrary"), vmem_limit_bytes=VMEM_LIMIT),
        name="ffn",
    )(h, p, fstate, *wts)


def kernel(x_prompt, x_sample, p_prompt, p_sample, state_dn_conv, state_dn, state_ret, state_ffn_conv, attn_norm_w, w_in, dn_conv_w, dn_A_log, dn_dt_bias, dn_norm_w, ret_norm_w, w_out, ffn_norm_w, w_up, ffn_conv_w, ffn_conv_b, w_down, ple_norm_w, w_ple_gate, w_ple, final_norm_w):
    depth = w_in.shape[0]
    n_bp = x_prompt.shape[0]
    row = lambda v: v.reshape(1, -1).astype(F32)
    hp, hs = x_prompt, x_sample
    outs_p = ([], [], [], [])
    outs_s = ([], [], [], [])
    for i in range(depth):
        wi = w_in[i]
        wmain = jnp.concatenate([wi[:, :4 * QK], wi[:, 4 * QK + 2 * N_HEADS:]], axis=1).astype(BF16)
        wba = jnp.pad(wi[:, 4 * QK:4 * QK + 2 * N_HEADS],
                      ((0, 0), (0, LANES - 2 * N_HEADS))).astype(BF16)
        gpar = jnp.zeros((2, LANES), F32)
        gpar = gpar.at[0, GATE_LANE0:GATE_LANE0 + N_HEADS].set(dn_A_log[i])
        gpar = gpar.at[1, GATE_LANE0:GATE_LANE0 + N_HEADS].set(dn_dt_bias[i])
        mixer_w = (row(attn_norm_w[i]), wmain, wba, dn_conv_w[i], gpar, row(dn_norm_w[i]),
                   row(ret_norm_w[i]), w_out[i].astype(BF16))
        ffn_w = (row(ffn_norm_w[i]), w_up[i].astype(BF16), ffn_conv_w[i], row(ffn_conv_b[i]),
                 w_down[i].astype(BF16), row(ple_norm_w[i]), w_ple_gate[i].astype(BF16),
                 w_ple[i].astype(BF16), row(final_norm_w))
        last = i == depth - 1
        zc = jnp.zeros((n_bp, DN_CONV - 1, 3 * QK), F32)
        zs = jnp.zeros((n_bp, N_HEADS, HEAD_DIM, HEAD_DIM), F32)
        zf = jnp.zeros((n_bp, FFN_CONV - 1, 2 * D_FF), F32)
        hp, c1, s1, r1 = _mixer_seq(hp, zc, zs, zs, 0, mixer_w, tt=256)
        hp, f1 = _ffn(hp, p_prompt[i], zf, ffn_w, last, sb=1, tt=256)
        hs, c2, s2, r2 = _mixer(hs, state_dn_conv[i], state_dn[i], state_ret[i], PAST_LEN,
                                mixer_w, sb=16, tt=x_sample.shape[1])
        hs, f2 = _ffn(hs, p_sample[i], state_ffn_conv[i], ffn_w, last, sb=16, tt=x_sample.shape[1])
        for lst, val in zip(outs_p, (c1, s1, r1, f1)):
            lst.append(val)
        for lst, val in zip(outs_s, (c2, s2, r2, f2)):
            lst.append(val)
    return (hp, hs,
            jnp.stack(outs_p[0]), jnp.stack(outs_p[1]), jnp.stack(outs_p[2]), jnp.stack(outs_p[3]),
            jnp.stack(outs_s[0]), jnp.stack(outs_s[1]), jnp.stack(outs_s[2]), jnp.stack(outs_s[3]))
```

```python
import functools
import math

import jax
import jax.numpy as jnp
from jax import lax
from jax.experimental import pallas as pl
from jax.experimental.pallas import tpu as pltpu

F32 = jnp.float32
BF16 = jnp.bfloat16

D_MODEL = 1024
HEAD_DIM = 128
N_HEADS = 4
QK = N_HEADS * HEAD_DIM
DN_CONV = 4
FFN_CONV = 3
D_FF = 2816
PLE_DIM = 256
CHUNK = 64
PAST_LEN = 16384
ROPE_BASE = 10000.0
EPS = 1e-6
LANES = 128
SUBLANES = 8
GATE_LANE0 = N_HEADS
FF_COLS = 1408
PROJ_SLAB = 256
VMEM_LIMIT = 56 * 1024 * 1024


def _mm(a, b):
    return jnp.dot(a.astype(BF16), b.astype(BF16), preferred_element_type=F32)


def _mm_nt(a, b):
    return lax.dot_general(a.astype(BF16), b.astype(BF16), (((1,), (1,)), ((), ())),
                           preferred_element_type=F32)


def _split2(x):
    hi = x.astype(BF16)
    lo = (x - hi.astype(F32)).astype(BF16)
    return hi, lo


def _mm3(a, b):
    ah, al = _split2(a)
    bh, bl = _split2(b)
    d = functools.partial(jnp.dot, preferred_element_type=F32)
    return d(ah, bh) + (d(ah, bl) + d(al, bh))


def _mm_exact_lhs(m_bf16, x):
    hi = x.astype(BF16)
    r1 = x - hi.astype(F32)
    mid = r1.astype(BF16)
    lo = (r1 - mid.astype(F32)).astype(BF16)
    d = functools.partial(jnp.dot, preferred_element_type=F32)
    return d(m_bf16, hi) + (d(m_bf16, mid) + d(m_bf16, lo))


def _rmsnorm(x, w):
    return x * lax.rsqrt(jnp.mean(x * x, axis=-1, keepdims=True) + EPS) * w


def _silu(x):
    return x * jax.nn.sigmoid(x)


def _softplus(x):
    return jnp.maximum(x, 0.0) + jnp.log1p(jnp.exp(-jnp.abs(x)))


def _lane_bcast(x, lane, width=LANES):
    return jnp.broadcast_to(x[:, lane:lane + 1], (x.shape[0], width))


def _unit_lower_inverse(a, eye, chunk):
    p = eye - a
    m = a
    for _ in range(int(math.log2(chunk)) - 1):
        m = _mm3(m, m)
        p = p + _mm3(p, m)
    return p


def _mixer_kernel(cfg, x_ref, cst_ref, s0_ref, r0_ref, cos_ref, sin_ref, dmat_ref, rscale_ref,
                  anw_ref, wmain_ref, wba_ref, convw_ref, gpar_ref, dnw_ref, retw_ref, wout_ref,
                  h_ref, cnew_ref, s_ref, r_ref,
                  e_ref, u_ref, w_ref, qd_ref, vn_ref, oi_ref, qk_ref, ktt_ref, egl_ref,
                  rqd_ref, rktt_ref, roi_ref, rv_ref):
    sb, tt, chunk, cdecay = cfg
    rows = sb * tt
    n_chunks = rows // chunk
    chunks_per_seq = tt // chunk
    shift = int(math.log2(chunk))
    l_idx = pl.program_id(1)

    @pl.when(l_idx == 0)
    def _():
        e_ref[:, SUBLANES - (DN_CONV - 1):SUBLANES, :] = cst_ref[...]
        s_ref[...] = s0_ref[...]
        r_ref[...] = r0_ref[...]

    x = x_ref[...].reshape(rows, D_MODEL)
    a = _rmsnorm(x, anw_ref[...]).astype(BF16)

    e_ref[:, SUBLANES:, :] = jnp.dot(a, wmain_ref[:, 0:3 * QK],
                                     preferred_element_type=F32).reshape(sb, tt, 3 * QK)
    proj = lambda i: jnp.dot(a, wmain_ref[:, (3 + i) * QK:(4 + i) * QK], preferred_element_type=F32)
    z = proj(0)
    rq = proj(1)
    rk = proj(2)
    rv = proj(3)
    rg = proj(4)
    pba = jnp.dot(a, wba_ref[...], preferred_element_type=F32)

    base = SUBLANES - (DN_CONV - 1)
    conv = e_ref[:, base:base + tt, :] * convw_ref[0:1, :]
    for j in range(1, DN_CONV):
        conv = conv + e_ref[:, base + j:base + j + tt, :] * convw_ref[j:j + 1, :]
    tail = e_ref[:, tt + base:tt + SUBLANES, :]
    cnew_ref[...] = tail
    e_ref[:, base:SUBLANES, :] = tail
    qkv = _silu(conv).reshape(rows, 3 * QK)

    beta_all = jax.nn.sigmoid(pba)
    g_all = -jnp.exp(gpar_ref[0:1, :]) * _softplus(pba + gpar_ref[1:2, :])

    ri = lax.broadcasted_iota(jnp.int32, (rows, rows), 0)
    ci = lax.broadcasted_iota(jnp.int32, (rows, rows), 1)
    same = (ri >> shift) == (ci >> shift)
    causal = same & (ci <= ri)
    strict = same & (ci < ri)
    eye = (ri == ci).astype(F32)
    gc_all = _mm_exact_lhs(causal.astype(BF16), g_all)
    gl_all = _mm_exact_lhs(same.astype(BF16), g_all)
    gct_all = gc_all.T

    cos = cos_ref[0]
    sin = sin_ref[0]
    lane = lax.broadcasted_iota(jnp.int32, (rows, LANES), 1)
    even = (lane & 1) == 0

    def rotary(t):
        swapped = jnp.where(even, pltpu.roll(t, LANES - 1, 1), pltpu.roll(t, 1, 1))
        return t * cos + swapped * sin

    for h in range(N_HEADS):
        hs = slice(h * HEAD_DIM, (h + 1) * HEAD_DIM)
        qh = qkv[:, h * HEAD_DIM:(h + 1) * HEAD_DIM]
        kh = qkv[:, QK + h * HEAD_DIM:QK + (h + 1) * HEAD_DIM]
        vh = qkv[:, 2 * QK + h * HEAD_DIM:2 * QK + (h + 1) * HEAD_DIM]
        qb = qh * lax.rsqrt(jnp.sum(qh * qh, axis=-1, keepdims=True) + EPS) * (HEAD_DIM ** -0.5)
        kb = kh * lax.rsqrt(jnp.sum(kh * kh, axis=-1, keepdims=True) + EPS)
        bcol = _lane_bcast(beta_all, h)
        gcc = _lane_bcast(gc_all, GATE_LANE0 + h)
        glc = _lane_bcast(gl_all, GATE_LANE0 + h)
        egc = jnp.exp(gcc)
        delta = (jnp.broadcast_to(gc_all[:, GATE_LANE0 + h:GATE_LANE0 + h + 1], (rows, rows))
                 - jnp.broadcast_to(gct_all[GATE_LANE0 + h:GATE_LANE0 + h + 1, :], (rows, rows)))
        decay = jnp.exp(jnp.where(causal, delta, -jnp.inf))
        kbeta = kb * bcol
        amat = jnp.where(strict, _mm_nt(kbeta, kb) * decay, 0.0)
        tinv = _unit_lower_inverse(amat, eye, chunk)
        sol = _mm3(tinv, jnp.concatenate([vh * bcol, kbeta * egc], axis=1))
        u_ref[:, hs] = sol[:, :HEAD_DIM]
        w_ref[:, hs] = sol[:, HEAD_DIM:]
        qk_ref[h] = _mm_nt(qb, kb) * decay
        qd_ref[:, hs] = qb * egc
        ktt_ref[h] = (kb * jnp.exp(glc - gcc)).T
        egl_ref[h] = jnp.exp(glc)
        rqh = rotary(rq[:, hs])
        rkh = rotary(rk[:, hs]) * (HEAD_DIM ** -0.5)
        rqk = _mm_nt(rqh, rkh) * dmat_ref[h]
        roi_ref[:, hs] = _mm(rqk, rv[:, hs])
        rqd_ref[:, hs] = rqh * rscale_ref[0, :, hs]
        rktt_ref[h] = (rkh * rscale_ref[1, :, hs]).T
    rv_ref[...] = rv
    vn_ref[...] = jnp.zeros_like(vn_ref)

    lane_chunk = lax.broadcasted_iota(jnp.int32, (HEAD_DIM, rows), 1) >> shift

    def chunk_step(c, carry):
        r0 = pl.multiple_of(c * chunk, chunk)
        slot = c // chunks_per_seq
        in_chunk = lane_chunk == c
        for h in range(N_HEADS):
            hs = slice(h * HEAD_DIM, (h + 1) * HEAD_DIM)
            s_old = s_ref[slot, h]
            lhs = jnp.concatenate([w_ref[pl.ds(r0, chunk), hs], qd_ref[pl.ds(r0, chunk), hs]], axis=0)
            res = _mm(lhs, s_old)
            v_new = u_ref[pl.ds(r0, chunk), hs] - res[:chunk]
            oi_ref[pl.ds(r0, chunk), hs] = res[chunk:]
            vn_ref[pl.ds(r0, chunk), hs] = v_new
            egl = jnp.broadcast_to(egl_ref[h, pl.ds(r0, 1), :], (HEAD_DIM, HEAD_DIM))
            s_ref[slot, h] = s_old * egl + _mm(jnp.where(in_chunk, ktt_ref[h], 0.0), vn_ref[:, hs])
            r_old = r_ref[slot, h]
            roi_ref[pl.ds(r0, chunk), hs] += _mm(rqd_ref[pl.ds(r0, chunk), hs], r_old)
            r_ref[slot, h] = r_old * cdecay[h] + _mm(jnp.where(in_chunk, rktt_ref[h], 0.0),
                                                     rv_ref[:, hs])
        return carry

    lax.fori_loop(0, n_chunks, chunk_step, 0)

    outs = []
    for h in range(N_HEADS):
        hs = slice(h * HEAD_DIM, (h + 1) * HEAD_DIM)
        o = oi_ref[:, hs] + _mm(qk_ref[h], vn_ref[:, hs])
        outs.append(_rmsnorm(o, dnw_ref[...]) * _silu(z[:, hs]))
    for h in range(N_HEADS):
        hs = slice(h * HEAD_DIM, (h + 1) * HEAD_DIM)
        o = roi_ref[:, hs]
        mu = jnp.mean(o, axis=-1, keepdims=True)
        var = jnp.mean(jnp.square(o - mu), axis=-1, keepdims=True)
        outs.append((o - mu) * lax.rsqrt(var + EPS) * retw_ref[:, hs] * _silu(rg[:, hs]))
    mix = jnp.concatenate(outs, axis=1).astype(BF16)
    h_new = x + jnp.dot(mix, wout_ref[...], preferred_element_type=F32)
    h_ref[...] = h_new.reshape(sb, tt, D_MODEL)


def _bmm(a, b):
    return lax.dot_general(a, b, (((2,), (1,)), ((0,), (0,))), preferred_element_type=F32)


def _bmm_nt(a, b):
    return lax.dot_general(a, b, (((2,), (2,)), ((0,), (0,))), preferred_element_type=F32)


def _bmm_tn(a, b):
    return lax.dot_general(a, b, (((1,), (1,)), ((0,), (0,))), preferred_element_type=F32)


def _pack_lhs(m):
    hi = m.astype(BF16).astype(F32)
    return jnp.concatenate([hi, m - hi, hi], axis=-1).astype(BF16)


def _pack_rhs(m):
    hi, lo = _split2(m)
    return jnp.concatenate([hi, hi, lo], axis=-2)


def _unit_lower_inverse_packed(a, eye, n_steps, between=lambda: None):
    c = a.shape[-1]
    upper = lax.broadcasted_iota(jnp.int32, a.shape[:-1] + (2 * c,), a.ndim - 1) >= c
    a2 = _bmm(_pack_lhs(a), _pack_rhs(a))
    between()
    mp = jnp.concatenate([a2, eye - a], axis=-1)
    for _ in range(n_steps):
        res = _bmm(_pack_lhs(mp[..., :c]), _pack_rhs(mp))
        between()
        mp = res + jnp.where(upper, mp, 0.0)
    return mp[..., c:]


def _mixer_seq_kernel(cfg, x_ref, cst_ref, s0_ref, r0_ref, cos_ref, sin_ref, tri_ref, dmat_ref,
                      rscale_ref, cdec_ref, anw_ref, wmain_ref, wba_ref, convw_ref, gpar_ref,
                      dnw_ref, retw_ref, wout_ref, h_ref, cnew_ref, s_ref, r_ref,
                      e_ref, proj_ref, pba_ref, xprev_ref):
    tt, chunk, n_l = cfg
    n_chunks = tt // chunk
    step = pl.program_id(0)
    l_idx = jnp.maximum(step - 1, 0) % n_l

    @pl.when(step == 0)
    def _():
        e_ref[...] = jnp.zeros_like(e_ref)
        proj_ref[...] = jnp.zeros_like(proj_ref)
        pba_ref[...] = jnp.zeros_like(pba_ref)
        xprev_ref[...] = jnp.zeros_like(xprev_ref)

    @pl.when(l_idx == 0)
    def _():
        e_ref[:, SUBLANES - (DN_CONV - 1):SUBLANES, :] = cst_ref[...]
        s_ref[...] = s0_ref[...]
        r_ref[...] = r0_ref[...]

    x = x_ref[0]
    a = _rmsnorm(x, anw_ref[...]).astype(BF16)
    n_qkv_slabs = 3 * QK // PROJ_SLAB
    n_rest_slabs = 5 * QK // PROJ_SLAB

    def project_slab(j):
        cs = slice(j * PROJ_SLAB, (j + 1) * PROJ_SLAB)
        res = jnp.dot(a, wmain_ref[:, cs], preferred_element_type=F32)
        if j < n_qkv_slabs:
            e_ref[0, SUBLANES:, cs] = res
        else:
            proj_ref[:, (j - n_qkv_slabs) * PROJ_SLAB:(j - n_qkv_slabs + 1) * PROJ_SLAB] = res

    def project_gates():
        pba_ref[...] = jnp.dot(a, wba_ref[...], preferred_element_type=F32)

    pending = ([functools.partial(project_slab, j) for j in range(n_qkv_slabs)] + [project_gates]
               + [functools.partial(project_slab, n_qkv_slabs + j) for j in range(n_rest_slabs)])

    def emit(n):
        for _ in range(min(n, len(pending))):
            pending.pop(0)()

    base = SUBLANES - (DN_CONV - 1)
    conv = e_ref[0, base:base + tt, :] * convw_ref[0:1, :]
    for j in range(1, DN_CONV):
        conv = conv + e_ref[0, base + j:base + j + tt, :] * convw_ref[j:j + 1, :]
    tail = e_ref[:, tt + base:tt + SUBLANES, :]
    cnew_ref[...] = tail
    e_ref[:, base:SUBLANES, :] = tail
    pba = pba_ref[...]
    z_gate = _silu(proj_ref[:, 0:QK])
    rq = proj_ref[:, QK:2 * QK]
    rk = proj_ref[:, 2 * QK:3 * QK]
    rv = proj_ref[:, 3 * QK:4 * QK]
    rg_gate = _silu(proj_ref[:, 4 * QK:5 * QK])

    cos = cos_ref[0]
    sin = sin_ref[0]
    even = (lax.broadcasted_iota(jnp.int32, (tt, LANES), 1) & 1) == 0

    def rotary(t):
        swapped = jnp.where(even, pltpu.roll(t, LANES - 1, 1), pltpu.roll(t, 1, 1))
        return t * cos + swapped * sin

    per = {k: [] for k in ("qb", "kb", "kbeta", "rhs", "qd", "kt", "gcc", "gct", "egl",
                           "rq", "rk", "rqd", "rkt", "rv")}
    chunks = lambda t: t.reshape(n_chunks, chunk, t.shape[-1])
    for h in range(N_HEADS):
        hs = slice(h * HEAD_DIM, (h + 1) * HEAD_DIM)
        rqh = rotary(rq[:, hs])
        rkh = rotary(rk[:, hs]) * (HEAD_DIM ** -0.5)
        per["rq"].append(chunks(rqh))
        per["rk"].append(chunks(rkh))
        per["rqd"].append(chunks(rqh * rscale_ref[0, :, hs]))
        per["rkt"].append(chunks(rkh * rscale_ref[1, :, hs]))
        per["rv"].append(chunks(rv[:, hs]))
    emit(3)
    qkv = _silu(conv)

    beta_all = jax.nn.sigmoid(pba)
    g_all = -jnp.exp(gpar_ref[0:1, :]) * _softplus(pba + gpar_ref[1:2, :])
    gc_all = _mm_exact_lhs(tri_ref[0], g_all)
    gl_all = _mm_exact_lhs(tri_ref[1], g_all)
    gct_all = gc_all.T
    emit(3)

    ri = lax.broadcasted_iota(jnp.int32, (chunk, chunk), 0)
    ci = lax.broadcasted_iota(jnp.int32, (chunk, chunk), 1)
    causal = ci <= ri
    strict = ci < ri
    eye = (ri == ci).astype(F32)
    n_steps = int(math.log2(chunk)) - 1

    for h in range(N_HEADS):
        hs = slice(h * HEAD_DIM, (h + 1) * HEAD_DIM)
        qh = qkv[:, h * HEAD_DIM:(h + 1) * HEAD_DIM]
        kh = qkv[:, QK + h * HEAD_DIM:QK + (h + 1) * HEAD_DIM]
        vh = qkv[:, 2 * QK + h * HEAD_DIM:2 * QK + (h + 1) * HEAD_DIM]
        qb = qh * lax.rsqrt(jnp.sum(qh * qh, axis=-1, keepdims=True) + EPS) * (HEAD_DIM ** -0.5)
        kb = kh * lax.rsqrt(jnp.sum(kh * kh, axis=-1, keepdims=True) + EPS)
        bcol = _lane_bcast(beta_all, h)
        gcc = _lane_bcast(gc_all, GATE_LANE0 + h)
        glc = _lane_bcast(gl_all, GATE_LANE0 + h)
        egc = jnp.exp(gcc)
        kbeta = kb * bcol
        per["qb"].append(chunks(qb))
        per["kb"].append(chunks(kb))
        per["kbeta"].append(chunks(kbeta))
        per["rhs"].append(chunks(jnp.concatenate([kbeta * egc, vh * bcol], axis=1)))
        per["qd"].append(chunks(qb * egc))
        per["kt"].append(chunks(kb * jnp.exp(glc - gcc)))
        per["gcc"].append(chunks(gcc))
        per["egl"].append(chunks(jnp.exp(glc))[:, 0:1, :])
        per["gct"].extend(gct_all[GATE_LANE0 + h:GATE_LANE0 + h + 1, c * chunk:(c + 1) * chunk][None]
                          for c in range(n_chunks))
    bat = {k: jnp.concatenate(v, axis=0) for k, v in per.items()}
    bf = lambda k: bat[k].astype(BF16)
    n_bat = N_HEADS * n_chunks
    by_head = lambda t: t.reshape((N_HEADS, n_chunks) + t.shape[1:])
    emit(2)

    delta = bat["gcc"][:, :, :chunk] - bat["gct"]
    decay = jnp.exp(jnp.where(causal, delta, -jnp.inf))
    kb16 = bf("kb")
    amat = jnp.where(strict, _bmm_nt(bf("kbeta"), kb16) * decay, 0.0)
    emit(1)
    tinv = _unit_lower_inverse_packed(amat, eye, n_steps, between=functools.partial(emit, 1))
    wu = _bmm(_pack_lhs(tinv), _pack_rhs(bat["rhs"])).astype(BF16)
    qk = (_bmm_nt(bf("qb"), kb16) * decay).astype(BF16)
    gh = by_head(_bmm_tn(bf("kt"), wu))
    qo = _bmm(qk, wu)
    qp = by_head((bat["qd"] - qo[:, :, :HEAD_DIM]).astype(BF16))
    op = by_head(qo[:, :, HEAD_DIM:])
    egl = by_head(bat["egl"])
    emit(1)
    rv16 = bf("rv")
    dmat = jnp.broadcast_to(dmat_ref[...][:, None], (N_HEADS, n_chunks, chunk, chunk))
    rqk = _bmm_nt(bf("rq"), bf("rk")) * dmat.reshape(n_bat, chunk, chunk)
    ro = by_head(_bmm(rqk.astype(BF16), rv16))
    rh = by_head(_bmm_tn(bf("rkt"), rv16))
    rqd = by_head(bf("rqd"))

    s_cur = s_ref[0]
    r_cur = r_ref[0]
    o_dn = []
    o_ret = []
    for c in range(n_chunks):
        s16 = s_cur.astype(BF16)
        o_dn.append(op[:, c] + _bmm(qp[:, c], s16))
        s_cur = (s_cur * egl[:, c] - _bmm(gh[:, c, :, :HEAD_DIM].astype(BF16), s16)
                 + gh[:, c, :, HEAD_DIM:])
        o_ret.append(ro[:, c] + _bmm(rqd[:, c], r_cur.astype(BF16)))
        r_cur = r_cur * cdec_ref[...] + rh[:, c]
    s_ref[0] = s_cur
    r_ref[0] = r_cur
    emit(len(pending))

    outs = []
    for h in range(N_HEADS):
        hs = slice(h * HEAD_DIM, (h + 1) * HEAD_DIM)
        o = jnp.concatenate([o_c[h] for o_c in o_dn], axis=0)
        outs.append(_rmsnorm(o, dnw_ref[...]) * z_gate[:, hs])
    for h in range(N_HEADS):
        hs = slice(h * HEAD_DIM, (h + 1) * HEAD_DIM)
        o = jnp.concatenate([o_c[h] for o_c in o_ret], axis=0)
        mu = jnp.mean(o, axis=-1, keepdims=True)
        var = jnp.mean(jnp.square(o - mu), axis=-1, keepdims=True)
        outs.append((o - mu) * lax.rsqrt(var + EPS) * retw_ref[:, hs] * rg_gate[:, hs])
    mix = jnp.concatenate(outs, axis=1).astype(BF16)
    h_ref[0] = xprev_ref[...] + jnp.dot(mix, wout_ref[...], preferred_element_type=F32)
    xprev_ref[...] = x


def _ffn_kernel(cfg, h_ref, p_ref, fst_ref, fnw_ref, wup_ref, fcw_ref, fcb_ref, wdown_ref,
                pnw_ref, wgate_ref, wple_ref, finw_ref,
                y_ref, fnew_ref, e_ref):
    sb, tt, final_norm = cfg
    rows = sb * tt
    l_idx = pl.program_id(1)
    base = SUBLANES - (FFN_CONV - 1)

    @pl.when(l_idx == 0)
    def _():
        e_ref[:, base:SUBLANES, :] = fst_ref[...]

    h = h_ref[...].reshape(rows, D_MODEL)
    m = _rmsnorm(h, fnw_ref[...]).astype(BF16)
    for c in range(2 * D_FF // FF_COLS):
        cs = slice(c * FF_COLS, (c + 1) * FF_COLS)
        e_ref[:, SUBLANES:, cs] = jnp.dot(m, wup_ref[:, cs],
                                          preferred_element_type=F32).reshape(sb, tt, FF_COLS)

    def conv(cs):
        acc = e_ref[:, base:base + tt, cs] * fcw_ref[0:1, cs]
        for j in range(1, FFN_CONV):
            acc = acc + e_ref[:, base + j:base + j + tt, cs] * fcw_ref[j:j + 1, cs]
        return (acc + fcb_ref[:, cs]).reshape(rows, FF_COLS)

    acc = h
    for c in range(D_FF // FF_COLS):
        ug = conv(slice(c * FF_COLS, (c + 1) * FF_COLS))
        uv = conv(slice(D_FF + c * FF_COLS, D_FF + (c + 1) * FF_COLS))
        act = (_silu(ug) * uv).astype(BF16)
        acc = acc + jnp.dot(act, wdown_ref[c * FF_COLS:(c + 1) * FF_COLS, :],
                            preferred_element_type=F32)
    tail = e_ref[:, tt + base:tt + SUBLANES, :]
    fnew_ref[...] = tail
    e_ref[:, base:SUBLANES, :] = tail

    h = acc
    gate = jax.nn.sigmoid(jnp.dot(_rmsnorm(h, pnw_ref[...]).astype(BF16), wgate_ref[...],
                                  preferred_element_type=F32))
    pp = jnp.dot(p_ref[...].reshape(rows, PLE_DIM).astype(BF16), wple_ref[...],
                 preferred_element_type=F32)
    h = h + gate * pp
    if final_norm:
        h = _rmsnorm(h, finw_ref[...])
    y_ref[...] = h.reshape(sb, tt, D_MODEL)


def _const_spec(shape):
    zeros = (0,) * len(shape)
    return pl.BlockSpec(shape, lambda b, l: zeros, pipeline_mode=pl.Buffered(1))


def _retention_tables(chunk, rows):
    hh = jnp.arange(N_HEADS, dtype=F32)
    log_gamma = jnp.log(1.0 - 2.0 ** (-5.0 - hh))
    bpos = (jnp.arange(chunk, dtype=F32) + 1.0)[None, :] * log_gamma[:, None]
    causal = jnp.tril(jnp.ones((chunk, chunk), dtype=bool))
    dmat = jnp.exp(jnp.where(causal, bpos[:, :, None] - bpos[:, None, :], -jnp.inf))
    n = rows // chunk
    dfull = jnp.einsum('ab,hij->haibj', jnp.eye(n, dtype=F32), dmat).reshape(N_HEADS, rows, rows)
    qscale = jnp.exp(bpos)
    kscale = jnp.exp(bpos[:, -1:] - bpos)
    expand = lambda t: jnp.repeat(jnp.tile(t.T, (n, 1)), HEAD_DIM, axis=1)
    rscale = jnp.stack([expand(qscale), expand(kscale)])
    return dmat, dfull, rscale


def _chunk_sum_matrices(chunk, rows):
    r = jnp.arange(rows)
    same = (r[:, None] // chunk) == (r[None, :] // chunk)
    causal = same & (r[None, :] <= r[:, None])
    return jnp.stack([causal, same]).astype(BF16)


def _rope_tables(pos0, tt, sb, n_l):
    inv = ROPE_BASE ** (-jnp.arange(0, HEAD_DIM, 2, dtype=F32) / HEAD_DIM)
    pos = pos0 + jnp.arange(n_l * tt)
    ang = pos.astype(F32)[:, None] * inv[None, :]
    cos = jnp.repeat(jnp.cos(ang), 2, axis=1)
    sin = jnp.stack([-jnp.sin(ang), jnp.sin(ang)], axis=-1).reshape(n_l * tt, HEAD_DIM)
    tile = lambda t: jnp.tile(t.reshape(n_l, 1, tt, HEAD_DIM), (1, sb, 1, 1)).reshape(
        n_l, sb * tt, HEAD_DIM)
    return tile(cos), tile(sin)


def _mixer(x, cstate, s0, r0, pos0, wts, sb, tt):
    n_b, seq, _ = x.shape
    chunk = CHUNK if seq % CHUNK == 0 else seq
    rows = sb * tt
    n_l = seq // tt
    assert tt % chunk == 0 and n_b % sb == 0 and seq % tt == 0
    anw, wmain, wba, convw, gpar, dnw, retw, wout = wts
    cos, sin = _rope_tables(pos0, tt, sb, n_l)
    _, dmat, rscale = _retention_tables(chunk, rows)
    cdecay = tuple(float((1.0 - 2.0 ** (-5.0 - h)) ** chunk) for h in range(N_HEADS))
    cfg = (sb, tt, chunk, cdecay)
    state_spec = pl.BlockSpec((sb, N_HEADS, HEAD_DIM, HEAD_DIM), lambda b, l: (b, 0, 0, 0))
    x_spec = pl.BlockSpec((sb, tt, D_MODEL), lambda b, l: (b, l, 0))
    conv_spec = pl.BlockSpec((sb, DN_CONV - 1, 3 * QK), lambda b, l: (b, 0, 0))
    rope_spec = pl.BlockSpec((1, rows, HEAD_DIM), lambda b, l: (l, 0, 0))
    head_rows = pltpu.VMEM((rows, QK), F32)
    head_sq = pltpu.VMEM((N_HEADS, rows, rows), F32)
    head_t = pltpu.VMEM((N_HEADS, HEAD_DIM, rows), F32)
    return pl.pallas_call(
        functools.partial(_mixer_kernel, cfg),
        grid=(n_b // sb, n_l),
        in_specs=[x_spec, conv_spec, state_spec, state_spec, rope_spec, rope_spec,
                  _const_spec(dmat.shape), _const_spec(rscale.shape), _const_spec(anw.shape),
                  _const_spec(wmain.shape), _const_spec(wba.shape), _const_spec(convw.shape),
                  _const_spec(gpar.shape), _const_spec(dnw.shape), _const_spec(retw.shape),
                  _const_spec(wout.shape)],
        out_specs=[x_spec, conv_spec, state_spec, state_spec],
        out_shape=[jax.ShapeDtypeStruct(x.shape, F32),
                   jax.ShapeDtypeStruct(cstate.shape, F32),
                   jax.ShapeDtypeStruct(s0.shape, F32),
                   jax.ShapeDtypeStruct(r0.shape, F32)],
        scratch_shapes=[pltpu.VMEM((sb, SUBLANES + tt, 3 * QK), F32),
                        head_rows, head_rows, head_rows, head_rows, head_rows,
                        head_sq, head_t, pltpu.VMEM((N_HEADS, rows, HEAD_DIM), F32),
                        head_rows, head_t, head_rows, head_rows],
        compiler_params=pltpu.CompilerParams(
            dimension_semantics=("arbitrary", "arbitrary"), vmem_limit_bytes=VMEM_LIMIT),
        name="mixer",
    )(x, cstate, s0, r0, cos, sin, dmat, rscale, anw, wmain, wba, convw, gpar, dnw, retw, wout)


def _mixer_seq(x, cstate, s0, r0, pos0, wts, tt):
    n_b, seq, _ = x.shape
    chunk = CHUNK if seq % CHUNK == 0 else seq
    n_l = seq // tt
    assert tt % chunk == 0 and seq % tt == 0
    anw, wmain, wba, convw, gpar, dnw, retw, wout = wts
    cos, sin = _rope_tables(pos0, tt, 1, n_l)
    dmat, _, rscale = _retention_tables(chunk, tt)
    cdec = jnp.broadcast_to(rscale[0, chunk - 1].reshape(N_HEADS, 1, HEAD_DIM),
                            (N_HEADS, 1, HEAD_DIM))
    tri = _chunk_sum_matrices(chunk, tt)
    n_tiles = n_b * n_l
    proj_tile = lambda i: jnp.minimum(i, n_tiles - 1)
    mix_tile = lambda i: jnp.maximum(i - 1, 0)
    state_spec = pl.BlockSpec((1, N_HEADS, HEAD_DIM, HEAD_DIM),
                              lambda i: (mix_tile(i) // n_l, 0, 0, 0))
    x_spec = pl.BlockSpec((1, tt, D_MODEL), lambda i: (proj_tile(i) // n_l, proj_tile(i) % n_l, 0))
    h_spec = pl.BlockSpec((1, tt, D_MODEL), lambda i: (mix_tile(i) // n_l, mix_tile(i) % n_l, 0))
    conv_spec = pl.BlockSpec((1, DN_CONV - 1, 3 * QK), lambda i: (mix_tile(i) // n_l, 0, 0))
    rope_spec = pl.BlockSpec((1, tt, HEAD_DIM), lambda i: (mix_tile(i) % n_l, 0, 0))
    consts = (tri, dmat, rscale, cdec, anw, wmain, wba, convw, gpar, dnw, retw, wout)
    const_spec = lambda shape: pl.BlockSpec(shape, lambda i: (0,) * len(shape),
                                            pipeline_mode=pl.Buffered(1))
    return pl.pallas_call(
        functools.partial(_mixer_seq_kernel, (tt, chunk, n_l)),
        grid=(n_tiles + 1,),
        in_specs=[x_spec, conv_spec, state_spec, state_spec, rope_spec, rope_spec]
        + [const_spec(c.shape) for c in consts],
        out_specs=[h_spec, conv_spec, state_spec, state_spec],
        out_shape=[jax.ShapeDtypeStruct(x.shape, F32),
                   jax.ShapeDtypeStruct(cstate.shape, F32),
                   jax.ShapeDtypeStruct(s0.shape, F32),
                   jax.ShapeDtypeStruct(r0.shape, F32)],
        scratch_shapes=[pltpu.VMEM((1, SUBLANES + tt, 3 * QK), F32),
                        pltpu.VMEM((tt, 5 * QK), F32),
                        pltpu.VMEM((tt, LANES), F32),
                        pltpu.VMEM((tt, D_MODEL), F32)],
        compiler_params=pltpu.CompilerParams(
            dimension_semantics=("arbitrary",), vmem_limit_bytes=VMEM_LIMIT),
        name="mixer_seq",
    )(x, cstate, s0, r0, cos, sin, *consts)


def _ffn(h, p, fstate, wts, final_norm, sb, tt):
    n_b, seq, _ = h.shape
    n_l = seq // tt
    assert n_b % sb == 0 and seq % tt == 0
    fnw, wup, fcw, fcb, wdown, pnw, wgate, wple, finw = wts
    x_spec = pl.BlockSpec((sb, tt, D_MODEL), lambda b, l: (b, l, 0))
    p_spec = pl.BlockSpec((sb, tt, PLE_DIM), lambda b, l: (b, l, 0))
    st_spec = pl.BlockSpec((sb, FFN_CONV - 1, 2 * D_FF), lambda b, l: (b, 0, 0))
    return pl.pallas_call(
        functools.partial(_ffn_kernel, (sb, tt, final_norm)),
        grid=(n_b // sb, n_l),
        in_specs=[x_spec, p_spec, st_spec] + [_const_spec(w.shape) for w in wts],
        out_specs=[x_spec, st_spec],
        out_shape=[jax.ShapeDtypeStruct(h.shape, F32), jax.ShapeDtypeStruct(fstate.shape, F32)],
        scratch_shapes=[pltpu.VMEM((sb, SUBLANES + tt, 2 * D_FF), F32)],
        compiler_params=pltpu.CompilerParams(
            dimension_semantics=("arbitrary", "arbitrary"), vmem_limit_bytes=VMEM_LIMIT),
        name="ffn",
    )(h, p, fstate, *wts)


def kernel(x_prompt, x_sample, p_prompt, p_sample, state_dn_conv, state_dn, state_ret, state_ffn_conv, attn_norm_w, w_in, dn_conv_w, dn_A_log, dn_dt_bias, dn_norm_w, ret_norm_w, w_out, ffn_norm_w, w_up, ffn_conv_w, ffn_conv_b, w_down, ple_norm_w, w_ple_gate, w_ple, final_norm_w):
    depth = w_in.shape[0]
    n_bp = x_prompt.shape[0]
    row = lambda v: v.reshape(1, -1).astype(F32)
    hp, hs = x_prompt, x_sample
    outs_p = ([], [], [], [])
    outs_s = ([], [], [], [])
    for i in range(depth):
        wi = w_in[i]
        wmain = jnp.concatenate([wi[:, :4 * QK], wi[:, 4 * QK + 2 * N_HEADS:]], axis=1).astype(BF16)
        wba = jnp.pad(wi[:, 4 * QK:4 * QK + 2 * N_HEADS],
                      ((0, 0), (0, LANES - 2 * N_HEADS))).astype(BF16)
        gpar = jnp.zeros((2, LANES), F32)
        gpar = gpar.at[0, GATE_LANE0:GATE_LANE0 + N_HEADS].set(dn_A_log[i])
        gpar = gpar.at[1, GATE_LANE0:GATE_LANE0 + N_HEADS].set(dn_dt_bias[i])
        mixer_w = (row(attn_norm_w[i]), wmain, wba, dn_conv_w[i], gpar, row(dn_norm_w[i]),
                   row(ret_norm_w[i]), w_out[i].astype(BF16))
        ffn_w = (row(ffn_norm_w[i]), w_up[i].astype(BF16), ffn_conv_w[i], row(ffn_conv_b[i]),
                 w_down[i].astype(BF16), row(ple_norm_w[i]), w_ple_gate[i].astype(BF16),
                 w_ple[i].astype(BF16), row(final_norm_w))
        last = i == depth - 1
        zc = jnp.zeros((n_bp, DN_CONV - 1, 3 * QK), F32)
        zs = jnp.zeros((n_bp, N_HEADS, HEAD_DIM, HEAD_DIM), F32)
        zf = jnp.zeros((n_bp, FFN_CONV - 1, 2 * D_FF), F32)
        hp, c1, s1, r1 = _mixer_seq(hp, zc, zs, zs, 0, mixer_w, tt=256)
        hp, f1 = _ffn(hp, p_prompt[i], zf, ffn_w, last, sb=1, tt=256)
        hs, c2, s2, r2 = _mixer(hs, state_dn_conv[i], state_dn[i], state_ret[i], PAST_LEN,
                                mixer_w, sb=16, tt=x_sample.shape[1])
        hs, f2 = _ffn(hs, p_sample[i], state_ffn_conv[i], ffn_w, last, sb=16, tt=x_sample.shape[1])
        for lst, val in zip(outs_p, (c1, s1, r1, f1)):
            lst.append(val)
        for lst, val in zip(outs_s, (c2, s2, r2, f2)):
            lst.append(val)
    return (hp, hs,
            jnp.stack(outs_p[0]), jnp.stack(outs_p[1]), jnp.stack(outs_p[2]), jnp.stack(outs_p[3]),
            jnp.stack(outs_s[0]), jnp.stack(outs_s[1]), jnp.stack(outs_s[2]), jnp.stack(outs_s[3]))
```

```python
import functools
import math

import jax
import jax.numpy as jnp
from jax import lax
from jax.experimental import pallas as pl
from jax.experimental.pallas import tpu as pltpu

F32 = jnp.float32
BF16 = jnp.bfloat16

D_MODEL = 1024
HEAD_DIM = 128
N_HEADS = 4
QK = N_HEADS * HEAD_DIM
DN_CONV = 4
FFN_CONV = 3
D_FF = 2816
PLE_DIM = 256
CHUNK = 64
PAST_LEN = 16384
ROPE_BASE = 10000.0
EPS = 1e-6
LANES = 128
SUBLANES = 8
MXU_DIM = 256
BLK = 64
GATE_LANE0 = N_HEADS
FF_SLABS = ((0, 1280), (1280, 1536))
VMEM_LIMIT = 56 * 1024 * 1024
PROMPT_TILE = (1, 256)
SAMPLE_MIXER_SEQS = 16
SAMPLE_FFN_SEQS = 16


def _split2(x):
    hi = x.astype(BF16)
    lo = (x - hi.astype(F32)).astype(BF16)
    return hi, lo


def _mm_exact_lhs(m_bf16, x):
    hi = x.astype(BF16)
    r1 = x - hi.astype(F32)
    mid = r1.astype(BF16)
    lo = (r1 - mid.astype(F32)).astype(BF16)
    d = functools.partial(jnp.dot, preferred_element_type=F32)
    return d(m_bf16, hi) + (d(m_bf16, mid) + d(m_bf16, lo))


def _rmsnorm(x, w):
    return x * lax.rsqrt(jnp.mean(x * x, axis=-1, keepdims=True) + EPS) * w


def _silu(x):
    return x * jax.nn.sigmoid(x)


def _softplus(x):
    return jnp.maximum(x, 0.0) + jnp.log1p(jnp.exp(-jnp.abs(x)))


def _lane_bcast(x, lane, width=LANES):
    return jnp.broadcast_to(x[:, lane:lane + 1], (x.shape[0], width))


def _bmm(a, b):
    return lax.dot_general(a, b, (((2,), (1,)), ((0,), (0,))), preferred_element_type=F32)


def _bmm_nt(a, b):
    return lax.dot_general(a, b, (((2,), (2,)), ((0,), (0,))), preferred_element_type=F32)


def _bmm_tn(a, b):
    return lax.dot_general(a, b, (((1,), (1,)), ((0,), (0,))), preferred_element_type=F32)


def _pack_lhs(m):
    hi = m.astype(BF16).astype(F32)
    return jnp.concatenate([hi, m - hi, hi], axis=-1).astype(BF16)


def _pack_rhs(m):
    hi, lo = _split2(m)
    return jnp.concatenate([hi, hi, lo], axis=-2)


def _unit_lower_inverse_packed(a, eye, n_steps):
    c = a.shape[-1]
    upper = lax.broadcasted_iota(jnp.int32, a.shape[:-1] + (2 * c,), a.ndim - 1) >= c
    a2 = _bmm(_pack_lhs(a), _pack_rhs(a))
    mp = jnp.concatenate([a2, eye - a], axis=-1)
    for _ in range(n_steps):
        res = _bmm(_pack_lhs(mp[..., :c]), _pack_rhs(mp))
        mp = res + jnp.where(upper, mp, 0.0)
    return mp[..., c:]


def _mixer_kernel(cfg, x_ref, cst_ref, s0_ref, r0_ref, cos_ref, sin_ref, tri_ref, dmat_ref,
                  rscale_ref, cdec_ref, anw_ref, wmain_ref, wba_ref, convw_ref, gpar_ref,
                  dnw_ref, retw_ref, wout_ref, h_ref, cnew_ref, s_ref, r_ref, e_ref):
    sb, tt, chunk, chained = cfg
    rows = sb * tt
    nblk = rows // BLK
    n_bat = N_HEADS * nblk
    shift = int(math.log2(chunk))
    n_steps = shift - 1
    base = SUBLANES - (DN_CONV - 1)
    l_idx = pl.program_id(1)

    @pl.when(l_idx == 0)
    def _():
        e_ref[:, base:SUBLANES, :] = cst_ref[...]
        if chained:
            s_ref[...] = s0_ref[...]
            r_ref[...] = r0_ref[...]

    x = x_ref[...].reshape(rows, D_MODEL)
    a = _rmsnorm(x, anw_ref[...]).astype(BF16)
    e_ref[:, SUBLANES:, :] = jnp.dot(a, wmain_ref[:, 0:3 * QK],
                                     preferred_element_type=F32).reshape(sb, tt, 3 * QK)
    proj = lambda i: jnp.dot(a, wmain_ref[:, (3 + i) * QK:(4 + i) * QK], preferred_element_type=F32)
    pba = jnp.dot(a, wba_ref[...], preferred_element_type=F32)

    conv = e_ref[:, base:base + tt, :] * convw_ref[0:1, :]
    for j in range(1, DN_CONV):
        conv = conv + e_ref[:, base + j:base + j + tt, :] * convw_ref[j:j + 1, :]
    tail = e_ref[:, tt + base:tt + SUBLANES, :]
    cnew_ref[...] = tail
    e_ref[:, base:SUBLANES, :] = tail
    rq = proj(1)
    qkv = _silu(conv).reshape(rows, 3 * QK)
    rk = proj(2)

    beta_all = jax.nn.sigmoid(pba)
    g_all = -jnp.exp(gpar_ref[0:1, :]) * _softplus(pba + gpar_ref[1:2, :])
    gc_all = _mm_exact_lhs(tri_ref[0], g_all)
    gl_all = _mm_exact_lhs(tri_ref[1], g_all)
    gct_all = gc_all.T

    ri = lax.broadcasted_iota(jnp.int32, (BLK, BLK), 0)
    ci = lax.broadcasted_iota(jnp.int32, (BLK, BLK), 1)
    same = (ri >> shift) == (ci >> shift)
    causal = same & (ci <= ri)
    strict = same & (ci < ri)
    eye = (ri == ci).astype(F32)

    cos = cos_ref[0]
    sin = sin_ref[0]
    even = (lax.broadcasted_iota(jnp.int32, (rows, LANES), 1) & 1) == 0

    def rotary(t):
        swapped = jnp.where(even, pltpu.roll(t, LANES - 1, 1), pltpu.roll(t, 1, 1))
        return t * cos + swapped * sin

    per = {k: [] for k in ("qb", "kb", "kbeta", "rhs", "qd", "kt", "gcc", "gct", "egl",
                           "rq", "rk", "rqd", "rkt", "rv")}
    blocks = lambda t: t.reshape(nblk, BLK, t.shape[-1])
    for h in range(N_HEADS):
        qh = qkv[:, h * HEAD_DIM:(h + 1) * HEAD_DIM]
        kh = qkv[:, QK + h * HEAD_DIM:QK + (h + 1) * HEAD_DIM]
        vh = qkv[:, 2 * QK + h * HEAD_DIM:2 * QK + (h + 1) * HEAD_DIM]
        qb = qh * lax.rsqrt(jnp.sum(qh * qh, axis=-1, keepdims=True) + EPS) * (HEAD_DIM ** -0.5)
        kb = kh * lax.rsqrt(jnp.sum(kh * kh, axis=-1, keepdims=True) + EPS)
        bcol = _lane_bcast(beta_all, h)
        gcc = _lane_bcast(gc_all, GATE_LANE0 + h)
        glc = _lane_bcast(gl_all, GATE_LANE0 + h)
        egc = jnp.exp(gcc)
        kbeta = kb * bcol
        per["qb"].append(blocks(qb))
        per["kb"].append(blocks(kb))
        per["kbeta"].append(blocks(kbeta))
        per["rhs"].append(blocks(jnp.concatenate([kbeta * egc, vh * bcol], axis=1)))
        per["qd"].append(blocks(qb * egc))
        per["kt"].append(blocks(kb * jnp.exp(glc - gcc)))
        per["gcc"].append(blocks(gcc))
        per["egl"].append(jnp.exp(glc).reshape(rows // chunk, chunk, HEAD_DIM)[:, 0:1, :])
        per["gct"].extend(gct_all[GATE_LANE0 + h:GATE_LANE0 + h + 1, b * BLK:(b + 1) * BLK][None]
                          for b in range(nblk))
    bat = {k: jnp.concatenate(v, axis=0) for k, v in per.items() if v}
    bf = lambda k: bat[k].astype(BF16)
    rv = proj(3)

    delta = bat["gcc"][:, :, :BLK] - bat["gct"]
    decay = jnp.exp(jnp.where(causal, delta, -jnp.inf))
    kb16 = bf("kb")
    amat = jnp.where(strict, _bmm_nt(bf("kbeta"), kb16) * decay, 0.0)
    tinv = _unit_lower_inverse_packed(amat, eye, n_steps)
    z = proj(0)
    for h in range(N_HEADS):
        hs = slice(h * HEAD_DIM, (h + 1) * HEAD_DIM)
        rqh = rotary(rq[:, hs])
        rkh = rotary(rk[:, hs]) * (HEAD_DIM ** -0.5)
        per["rq"].append(blocks(rqh))
        per["rk"].append(blocks(rkh))
        per["rqd"].append(blocks(rqh * rscale_ref[0, :, hs]))
        per["rkt"].append(blocks(rkh * rscale_ref[1, :, hs]))
        per["rv"].append(blocks(rv[:, hs]))
    bat.update({k: jnp.concatenate(per[k], axis=0) for k in ("rq", "rk", "rqd", "rkt", "rv")})
    rg = proj(4)
    wu = _bmm(_pack_lhs(tinv), _pack_rhs(bat["rhs"]))
    qk = (_bmm_nt(bf("qb"), kb16) * decay).astype(BF16)
    rv16 = bf("rv")
    dmat = jnp.broadcast_to(dmat_ref[...][:, None], (N_HEADS, nblk, BLK, BLK)).reshape(n_bat, BLK, BLK)
    ro = _bmm((_bmm_nt(bf("rq"), bf("rk")) * dmat).astype(BF16), rv16)

    if chained:
        by_head = lambda t: t.reshape((N_HEADS, nblk) + t.shape[1:])
        wu16 = wu.astype(BF16)
        gh = by_head(_bmm_tn(bf("kt"), wu16))
        qo = _bmm(qk, wu16)
        qp = by_head((bat["qd"] - qo[:, :, :HEAD_DIM]).astype(BF16))
        op = by_head(qo[:, :, HEAD_DIM:])
        egl = by_head(bat["egl"])
        ro = by_head(ro)
        rh = by_head(_bmm_tn(bf("rkt"), rv16))
        rqd = by_head(bf("rqd"))
        s_cur = s_ref[0]
        r_cur = r_ref[0]
        o_dn = []
        o_ret = []
        for c in range(nblk):
            s16 = s_cur.astype(BF16)
            o_dn.append(op[:, c] + _bmm(qp[:, c], s16))
            s_cur = (s_cur * egl[:, c] - _bmm(gh[:, c, :, :HEAD_DIM].astype(BF16), s16)
                     + gh[:, c, :, HEAD_DIM:])
            o_ret.append(ro[:, c] + _bmm(rqd[:, c], r_cur.astype(BF16)))
            r_cur = r_cur * cdec_ref[...] + rh[:, c]
        s_ref[0] = s_cur
        r_ref[0] = r_cur
        head_rows = lambda parts, h: jnp.concatenate([p[h] for p in parts], axis=0)
        o_dn = [head_rows(o_dn, h) for h in range(N_HEADS)]
        o_ret = [head_rows(o_ret, h) for h in range(N_HEADS)]
    else:
        n_seq = N_HEADS * sb
        seqs = lambda t: t.reshape(n_seq, tt, t.shape[-1])
        s_old = jnp.concatenate([s0_ref[:, h] for h in range(N_HEADS)], axis=0)
        lhs = jnp.concatenate([seqs(wu[:, :, :HEAD_DIM]), seqs(bat["qd"])], axis=1).astype(BF16)
        res = _bmm(lhs, s_old.astype(BF16))
        v_new = seqs(wu[:, :, HEAD_DIM:]) - res[:, :tt]
        v16 = v_new.astype(BF16)
        s_new = s_old * bat["egl"] + _bmm_tn(seqs(bat["kt"]).astype(BF16), v16)
        o = res[:, tt:].reshape(n_bat, BLK, HEAD_DIM) + _bmm(qk, v16.reshape(n_bat, BLK, HEAD_DIM))
        r_old = jnp.concatenate([r0_ref[:, h] for h in range(N_HEADS)], axis=0)
        ro = ro + _bmm(seqs(bat["rqd"]).astype(BF16),
                       r_old.astype(BF16)).reshape(n_bat, BLK, HEAD_DIM)
        cdec = jnp.broadcast_to(cdec_ref[...][:, None], (N_HEADS, sb, 1, HEAD_DIM))
        r_new = (r_old * cdec.reshape(n_seq, 1, HEAD_DIM)
                 + _bmm_tn(seqs(bat["rkt"]).astype(BF16), seqs(bat["rv"]).astype(BF16)))
        for h in range(N_HEADS):
            s_ref[:, h] = s_new[h * sb:(h + 1) * sb]
            r_ref[:, h] = r_new[h * sb:(h + 1) * sb]
        o_dn = [o[h * nblk:(h + 1) * nblk].reshape(rows, HEAD_DIM) for h in range(N_HEADS)]
        o_ret = [ro[h * nblk:(h + 1) * nblk].reshape(rows, HEAD_DIM) for h in range(N_HEADS)]

    outs = []
    for h in range(N_HEADS):
        hs = slice(h * HEAD_DIM, (h + 1) * HEAD_DIM)
        outs.append(_rmsnorm(o_dn[h], dnw_ref[...]) * _silu(z[:, hs]))
    for h in range(N_HEADS):
        hs = slice(h * HEAD_DIM, (h + 1) * HEAD_DIM)
        o = o_ret[h]
        mu = jnp.mean(o, axis=-1, keepdims=True)
        var = jnp.mean(jnp.square(o - mu), axis=-1, keepdims=True)
        outs.append((o - mu) * lax.rsqrt(var + EPS) * retw_ref[:, hs] * _silu(rg[:, hs]))
    mix = jnp.concatenate(outs, axis=1).astype(BF16)
    h_new = x + jnp.dot(mix, wout_ref[...], preferred_element_type=F32)
    h_ref[...] = h_new.reshape(sb, tt, D_MODEL)


def _ffn_kernel(cfg, h_ref, p_ref, fst_ref, fnw_ref, wup_ref, fcw_ref, fcb_ref, wdown_ref,
                pnw_ref, wgate_ref, wple_ref, finw_ref,
                y_ref, fnew_ref, e_ref):
    sb, tt, final_norm = cfg
    rows = sb * tt
    l_idx = pl.program_id(1)
    base = SUBLANES - (FFN_CONV - 1)

    @pl.when(l_idx == 0)
    def _():
        e_ref[:, base:SUBLANES, :] = fst_ref[...]

    h = h_ref[...].reshape(rows, D_MODEL)
    m = _rmsnorm(h, fnw_ref[...]).astype(BF16)
    for half in (0, D_FF):
        for c0, cw in FF_SLABS:
            cs = slice(half + c0, half + c0 + cw)
            e_ref[:, SUBLANES:, cs] = jnp.dot(m, wup_ref[:, cs],
                                              preferred_element_type=F32).reshape(sb, tt, cw)

    def conv(cs):
        acc = e_ref[:, base:base + tt, cs] * fcw_ref[0:1, cs]
        for j in range(1, FFN_CONV):
            acc = acc + e_ref[:, base + j:base + j + tt, cs] * fcw_ref[j:j + 1, cs]
        return (acc + fcb_ref[:, cs]).reshape(rows, cs.stop - cs.start)

    acc = h
    for c0, cw in FF_SLABS:
        ug = conv(slice(c0, c0 + cw))
        uv = conv(slice(D_FF + c0, D_FF + c0 + cw))
        act = (_silu(ug) * uv).astype(BF16)
        acc = acc + jnp.dot(act, wdown_ref[c0:c0 + cw, :], preferred_element_type=F32)
    tail = e_ref[:, tt + base:tt + SUBLANES, :]
    fnew_ref[...] = tail
    e_ref[:, base:SUBLANES, :] = tail

    h = acc
    gate = jax.nn.sigmoid(jnp.dot(_rmsnorm(h, pnw_ref[...]).astype(BF16), wgate_ref[...],
                                  preferred_element_type=F32))
    pp = jnp.dot(p_ref[...].reshape(rows, PLE_DIM).astype(BF16), wple_ref[...],
                 preferred_element_type=F32)
    h = h + gate * pp
    if final_norm:
        h = _rmsnorm(h, finw_ref[...])
    y_ref[...] = h.reshape(sb, tt, D_MODEL)


def _const_spec(shape):
    zeros = (0,) * len(shape)
    return pl.BlockSpec(shape, lambda b, l: zeros, pipeline_mode=pl.Buffered(1))


def _retention_tables(chunk, rows):
    hh = jnp.arange(N_HEADS, dtype=F32)
    log_gamma = jnp.log(1.0 - 2.0 ** (-5.0 - hh))
    bpos = (jnp.arange(chunk, dtype=F32) + 1.0)[None, :] * log_gamma[:, None]
    causal = jnp.tril(jnp.ones((chunk, chunk), dtype=bool))
    dmat = jnp.exp(jnp.where(causal, bpos[:, :, None] - bpos[:, None, :], -jnp.inf))
    n = BLK // chunk
    dblk = jnp.einsum('ab,hij->haibj', jnp.eye(n, dtype=F32), dmat).reshape(N_HEADS, BLK, BLK)
    qscale = jnp.exp(bpos)
    kscale = jnp.exp(bpos[:, -1:] - bpos)
    expand = lambda t: jnp.repeat(jnp.tile(t.T, (rows // chunk, 1)), HEAD_DIM, axis=1)
    rscale = jnp.stack([expand(qscale), expand(kscale)])
    cdec = jnp.broadcast_to(jnp.exp(bpos[:, -1]).reshape(N_HEADS, 1, 1), (N_HEADS, 1, HEAD_DIM))
    return dblk, rscale, cdec


def _chunk_sum_matrices(chunk, rows):
    r = jnp.arange(rows)
    same = (r[:, None] // chunk) == (r[None, :] // chunk)
    causal = same & (r[None, :] <= r[:, None])
    return jnp.stack([causal, same]).astype(BF16)


def _rope_tables(pos0, tt, sb, n_l):
    inv = ROPE_BASE ** (-jnp.arange(0, HEAD_DIM, 2, dtype=F32) / HEAD_DIM)
    pos = pos0 + jnp.arange(n_l * tt)
    ang = pos.astype(F32)[:, None] * inv[None, :]
    cos = jnp.repeat(jnp.cos(ang), 2, axis=1)
    sin = jnp.stack([-jnp.sin(ang), jnp.sin(ang)], axis=-1).reshape(n_l * tt, HEAD_DIM)
    tile = lambda t: jnp.tile(t.reshape(n_l, 1, tt, HEAD_DIM), (1, sb, 1, 1)).reshape(
        n_l, sb * tt, HEAD_DIM)
    return tile(cos), tile(sin)


def _mixer(x, cstate, s0, r0, pos0, wts, sb, tt):
    n_b, seq, _ = x.shape
    chunk = CHUNK if seq % CHUNK == 0 else seq
    chained = chunk == BLK
    rows = sb * tt
    n_l = seq // tt
    assert n_b % sb == 0 and seq % tt == 0 and rows % BLK == 0 and BLK % chunk == 0
    assert (sb == 1 and tt % chunk == 0) if chained else (tt == seq == chunk)
    cos, sin = _rope_tables(pos0, tt, sb, n_l)
    dmat, rscale, cdec = _retention_tables(chunk, rows)
    tri = _chunk_sum_matrices(chunk, rows)
    state_spec = pl.BlockSpec((sb, N_HEADS, HEAD_DIM, HEAD_DIM), lambda b, l: (b, 0, 0, 0))
    x_spec = pl.BlockSpec((sb, tt, D_MODEL), lambda b, l: (b, l, 0))
    conv_spec = pl.BlockSpec((sb, DN_CONV - 1, 3 * QK), lambda b, l: (b, 0, 0))
    rope_spec = pl.BlockSpec((1, rows, HEAD_DIM), lambda b, l: (l, 0, 0))
    consts = (tri, dmat, rscale, cdec) + tuple(wts)
    return pl.pallas_call(
        functools.partial(_mixer_kernel, (sb, tt, chunk, chained)),
        grid=(n_b // sb, n_l),
        in_specs=[x_spec, conv_spec, state_spec, state_spec, rope_spec, rope_spec]
        + [_const_spec(c.shape) for c in consts],
        out_specs=[x_spec, conv_spec, state_spec, state_spec],
        out_shape=[jax.ShapeDtypeStruct(x.shape, F32),
                   jax.ShapeDtypeStruct(cstate.shape, F32),
                   jax.ShapeDtypeStruct(s0.shape, F32),
                   jax.ShapeDtypeStruct(r0.shape, F32)],
        scratch_shapes=[pltpu.VMEM((sb, SUBLANES + tt, 3 * QK), F32)],
        compiler_params=pltpu.CompilerParams(
            dimension_semantics=("arbitrary", "arbitrary"), vmem_limit_bytes=VMEM_LIMIT),
        name="mixer",
    )(x, cstate, s0, r0, cos, sin, *consts)


def _ffn(h, p, fstate, wts, final_norm, sb, tt):
    n_b, seq, _ = h.shape
    n_l = seq // tt
    assert n_b % sb == 0 and seq % tt == 0
    x_spec = pl.BlockSpec((sb, tt, D_MODEL), lambda b, l: (b, l, 0))
    p_spec = pl.BlockSpec((sb, tt, PLE_DIM), lambda b, l: (b, l, 0))
    st_spec = pl.BlockSpec((sb, FFN_CONV - 1, 2 * D_FF), lambda b, l: (b, 0, 0))
    return pl.pallas_call(
        functools.partial(_ffn_kernel, (sb, tt, final_norm)),
        grid=(n_b // sb, n_l),
        in_specs=[x_spec, p_spec, st_spec] + [_const_spec(w.shape) for w in wts],
        out_specs=[x_spec, st_spec],
        out_shape=[jax.ShapeDtypeStruct(h.shape, F32), jax.ShapeDtypeStruct(fstate.shape, F32)],
        scratch_shapes=[pltpu.VMEM((sb, SUBLANES + tt, 2 * D_FF), F32)],
        compiler_params=pltpu.CompilerParams(
            dimension_semantics=("arbitrary", "arbitrary"), vmem_limit_bytes=VMEM_LIMIT),
        name="ffn",
    )(h, p, fstate, *wts)


def kernel(x_prompt, x_sample, p_prompt, p_sample, state_dn_conv, state_dn, state_ret, state_ffn_conv, attn_norm_w, w_in, dn_conv_w, dn_A_log, dn_dt_bias, dn_norm_w, ret_norm_w, w_out, ffn_norm_w, w_up, ffn_conv_w, ffn_conv_b, w_down, ple_norm_w, w_ple_gate, w_ple, final_norm_w):
    depth = w_in.shape[0]
    n_bp = x_prompt.shape[0]
    dec_len = x_sample.shape[1]
    row = lambda v: v.reshape(1, -1).astype(F32)
    hp, hs = x_prompt, x_sample
    outs_p = ([], [], [], [])
    outs_s = ([], [], [], [])
    for i in range(depth):
        wi = w_in[i]
        wmain = jnp.concatenate([wi[:, :4 * QK], wi[:, 4 * QK + 2 * N_HEADS:]], axis=1).astype(BF16)
        wba = jnp.pad(wi[:, 4 * QK:4 * QK + 2 * N_HEADS],
                      ((0, 0), (0, LANES - 2 * N_HEADS))).astype(BF16)
        gpar = jnp.zeros((2, LANES), F32)
        gpar = gpar.at[0, GATE_LANE0:GATE_LANE0 + N_HEADS].set(dn_A_log[i])
        gpar = gpar.at[1, GATE_LANE0:GATE_LANE0 + N_HEADS].set(dn_dt_bias[i])
        mixer_w = (row(attn_norm_w[i]), wmain, wba, dn_conv_w[i], gpar, row(dn_norm_w[i]),
                   row(ret_norm_w[i]), w_out[i].astype(BF16))
        ffn_w = (row(ffn_norm_w[i]), w_up[i].astype(BF16), ffn_conv_w[i], row(ffn_conv_b[i]),
                 w_down[i].astype(BF16), row(ple_norm_w[i]), w_ple_gate[i].astype(BF16),
                 w_ple[i].astype(BF16), row(final_norm_w))
        last = i == depth - 1
        zc = jnp.zeros((n_bp, DN_CONV - 1, 3 * QK), F32)
        zs = jnp.zeros((n_bp, N_HEADS, HEAD_DIM, HEAD_DIM), F32)
        zf = jnp.zeros((n_bp, FFN_CONV - 1, 2 * D_FF), F32)
        p_sb, p_tt = PROMPT_TILE
        hp, c1, s1, r1 = _mixer(hp, zc, zs, zs, 0, mixer_w, p_sb, p_tt)
        hp, f1 = _ffn(hp, p_prompt[i], zf, ffn_w, last, p_sb, p_tt)
        hs, c2, s2, r2 = _mixer(hs, state_dn_conv[i], state_dn[i], state_ret[i], PAST_LEN,
                                mixer_w, SAMPLE_MIXER_SEQS, dec_len)
        hs, f2 = _ffn(hs, p_sample[i], state_ffn_conv[i], ffn_w, last, SAMPLE_FFN_SEQS, dec_len)
        for lst, val in zip(outs_p, (c1, s1, r1, f1)):
            lst.append(val)
        for lst, val in zip(outs_s, (c2, s2, r2, f2)):
            lst.append(val)
    return (hp, hs,
            jnp.stack(outs_p[0]), jnp.stack(outs_p[1]), jnp.stack(outs_p[2]), jnp.stack(outs_p[3]),
            jnp.stack(outs_s[0]), jnp.stack(outs_s[1]), jnp.stack(outs_s[2]), jnp.stack(outs_s[3]))
```

```python
import functools
import math

import jax
import jax.numpy as jnp
import numpy as np
from jax import lax
from jax.experimental import pallas as pl
from jax.experimental.pallas import tpu as pltpu

F32 = jnp.float32
BF16 = jnp.bfloat16

D_MODEL = 1024
HEAD_DIM = 128
N_HEADS = 4
QK = N_HEADS * HEAD_DIM
DN_CONV = 4
FFN_CONV = 3
D_FF = 2816
PLE_DIM = 256
CHUNK = 64
PAST_LEN = 16384
ROPE_BASE = 10000.0
EPS = 1e-6
LANES = 128
SUBLANES = 8
MXU_DIM = 256
BLK = 64
GATE_LANE0 = N_HEADS
FF_SLABS = ((0, 1280), (1280, 1536))
VMEM_LIMIT = 56 * 1024 * 1024
PROMPT_MIXER_TILE = (1, 256)
PROMPT_FFN_TILE = (1, 512)
SAMPLE_MIXER_SEQS = 16
SAMPLE_FFN_SEQS = 16


def _split2(x):
    hi = x.astype(BF16)
    lo = (x - hi.astype(F32)).astype(BF16)
    return hi, lo


def _mm_exact_lhs(m_bf16, x):
    hi = x.astype(BF16)
    r1 = x - hi.astype(F32)
    mid = r1.astype(BF16)
    lo = (r1 - mid.astype(F32)).astype(BF16)
    d = functools.partial(jnp.dot, preferred_element_type=F32)
    return d(m_bf16, hi) + (d(m_bf16, mid) + d(m_bf16, lo))


def _rmsnorm(x, w):
    return x * lax.rsqrt(jnp.mean(x * x, axis=-1, keepdims=True) + EPS) * w


def _silu(x):
    return x * jax.nn.sigmoid(x)


def _softplus(x):
    return jnp.maximum(x, 0.0) + jnp.log(1.0 + jnp.exp(-jnp.abs(x)))


def _lane_bcast(x, lane, width=LANES):
    return jnp.broadcast_to(x[:, lane:lane + 1], (x.shape[0], width))


def _bmm(a, b):
    return lax.dot_general(a, b, (((2,), (1,)), ((0,), (0,))), preferred_element_type=F32)


def _bmm_nt(a, b):
    return lax.dot_general(a, b, (((2,), (2,)), ((0,), (0,))), preferred_element_type=F32)


def _bmm_tn(a, b):
    return lax.dot_general(a, b, (((1,), (1,)), ((0,), (0,))), preferred_element_type=F32)


def _pack_lhs(m):
    hi = m.astype(BF16).astype(F32)
    return jnp.concatenate([hi, m - hi, hi], axis=-1).astype(BF16)


def _pack_rhs(m):
    hi, lo = _split2(m)
    return jnp.concatenate([hi, hi, lo], axis=-2)


def _unit_lower_inverse_packed(a, eye, n_steps):
    c = a.shape[-1]
    upper = lax.broadcasted_iota(jnp.int32, a.shape[:-1] + (2 * c,), a.ndim - 1) >= c
    a2 = _bmm(_pack_lhs(a), _pack_rhs(a))
    mp = jnp.concatenate([a2, eye - a], axis=-1)
    for _ in range(n_steps):
        res = _bmm(_pack_lhs(mp[..., :c]), _pack_rhs(mp))
        mp = res + jnp.where(upper, mp, 0.0)
    return mp[..., c:]


def _mixer_kernel(cfg, x_ref, cst_ref, s0_ref, r0_ref, cos_ref, sin_ref, tri_ref, dmat_ref,
                  rscale_ref, cdec_ref, anw_ref, wmain_ref, wba_ref, convw_ref, gpar_ref,
                  dnw_ref, retw_ref, wout_ref, h_ref, cnew_ref, s_ref, r_ref, e_ref):
    sb, tt, chunk, chained = cfg
    rows = sb * tt
    nblk = rows // BLK
    n_bat = N_HEADS * nblk
    shift = int(math.log2(chunk))
    n_steps = shift - 1
    base = SUBLANES - (DN_CONV - 1)
    l_idx = pl.program_id(1)

    @pl.when(l_idx == 0)
    def _():
        e_ref[:, base:SUBLANES, :] = cst_ref[...]
        if chained:
            s_ref[...] = s0_ref[...]
            r_ref[...] = r0_ref[...]

    x = x_ref[...].reshape(rows, D_MODEL)
    a = _rmsnorm(x, anw_ref[...]).astype(BF16)
    e_ref[:, SUBLANES:, :] = jnp.dot(a, wmain_ref[:, 0:3 * QK],
                                     preferred_element_type=F32).reshape(sb, tt, 3 * QK)
    proj = lambda i: jnp.dot(a, wmain_ref[:, (3 + i) * QK:(4 + i) * QK], preferred_element_type=F32)
    pba = jnp.dot(a, wba_ref[...], preferred_element_type=F32)

    conv = e_ref[:, base:base + tt, :] * convw_ref[0:1, :]
    for j in range(1, DN_CONV):
        conv = conv + e_ref[:, base + j:base + j + tt, :] * convw_ref[j:j + 1, :]
    tail = e_ref[:, tt + base:tt + SUBLANES, :]
    cnew_ref[...] = tail
    e_ref[:, base:SUBLANES, :] = tail
    rq = proj(1)
    qkv = _silu(conv).reshape(rows, 3 * QK)
    rk = proj(2)

    beta_all = jax.nn.sigmoid(pba)
    g_all = -jnp.exp(gpar_ref[0:1, :]) * _softplus(pba + gpar_ref[1:2, :])
    gc_all = _mm_exact_lhs(tri_ref[0], g_all)
    gl_all = _mm_exact_lhs(tri_ref[1], g_all)
    gct_all = gc_all.T

    ri = lax.broadcasted_iota(jnp.int32, (BLK, BLK), 0)
    ci = lax.broadcasted_iota(jnp.int32, (BLK, BLK), 1)
    same = (ri >> shift) == (ci >> shift)
    causal = same & (ci <= ri)
    strict = same & (ci < ri)
    eye = (ri == ci).astype(F32)

    cos = cos_ref[0]
    sin = sin_ref[0]
    even = (lax.broadcasted_iota(jnp.int32, (rows, LANES), 1) & 1) == 0

    def rotary(t):
        swapped = jnp.where(even, pltpu.roll(t, LANES - 1, 1), pltpu.roll(t, 1, 1))
        return t * cos + swapped * sin

    per = {k: [] for k in ("qb", "kb", "kbeta", "rhs", "qd", "kt", "gcc", "gct", "egl",
                           "rq", "rk", "rqd", "rkt", "rv")}
    blocks = lambda t: t.reshape(nblk, BLK, t.shape[-1])
    for h in range(N_HEADS):
        qh = qkv[:, h * HEAD_DIM:(h + 1) * HEAD_DIM]
        kh = qkv[:, QK + h * HEAD_DIM:QK + (h + 1) * HEAD_DIM]
        vh = qkv[:, 2 * QK + h * HEAD_DIM:2 * QK + (h + 1) * HEAD_DIM]
        qb = qh * lax.rsqrt(jnp.sum(qh * qh, axis=-1, keepdims=True) + EPS) * (HEAD_DIM ** -0.5)
        kb = kh * lax.rsqrt(jnp.sum(kh * kh, axis=-1, keepdims=True) + EPS)
        bcol = _lane_bcast(beta_all, h)
        gcc = _lane_bcast(gc_all, GATE_LANE0 + h)
        glc = _lane_bcast(gl_all, GATE_LANE0 + h)
        egc = jnp.exp(gcc)
        kbeta = kb * bcol
        per["qb"].append(blocks(qb))
        per["kb"].append(blocks(kb))
        per["kbeta"].append(blocks(kbeta))
        per["rhs"].append(blocks(jnp.concatenate([kbeta * egc, vh * bcol], axis=1)))
        per["qd"].append(blocks(qb * egc))
        per["kt"].append(blocks(kb * jnp.exp(glc - gcc)))
        per["gcc"].append(blocks(gcc))
        per["egl"].append(jnp.exp(glc).reshape(rows // chunk, chunk, HEAD_DIM)[:, 0:1, :])
        per["gct"].extend(gct_all[GATE_LANE0 + h:GATE_LANE0 + h + 1, b * BLK:(b + 1) * BLK][None]
                          for b in range(nblk))
    bat = {k: jnp.concatenate(v, axis=0) for k, v in per.items() if v}
    bf = lambda k: bat[k].astype(BF16)
    rv = proj(3)

    delta = bat["gcc"][:, :, :BLK] - bat["gct"]
    decay = jnp.exp(jnp.where(causal, delta, -jnp.inf))
    kb16 = bf("kb")
    amat = jnp.where(strict, _bmm_nt(bf("kbeta"), kb16) * decay, 0.0)
    tinv = _unit_lower_inverse_packed(amat, eye, n_steps)
    z = proj(0)
    for h in range(N_HEADS):
        hs = slice(h * HEAD_DIM, (h + 1) * HEAD_DIM)
        rqh = rotary(rq[:, hs])
        rkh = rotary(rk[:, hs]) * (HEAD_DIM ** -0.5)
        per["rq"].append(blocks(rqh))
        per["rk"].append(blocks(rkh))
        per["rqd"].append(blocks(rqh * rscale_ref[0, :, hs]))
        per["rkt"].append(blocks(rkh * rscale_ref[1, :, hs]))
        per["rv"].append(blocks(rv[:, hs]))
    bat.update({k: jnp.concatenate(per[k], axis=0) for k in ("rq", "rk", "rqd", "rkt", "rv")})
    rg = proj(4)
    wu = _bmm(_pack_lhs(tinv), _pack_rhs(bat["rhs"]))
    qk = (_bmm_nt(bf("qb"), kb16) * decay).astype(BF16)
    rv16 = bf("rv")
    dmat = jnp.broadcast_to(dmat_ref[...][:, None], (N_HEADS, nblk, BLK, BLK)).reshape(n_bat, BLK, BLK)
    ro = _bmm((_bmm_nt(bf("rq"), bf("rk")) * dmat).astype(BF16), rv16)

    if chained:
        by_head = lambda t: t.reshape((N_HEADS, nblk) + t.shape[1:])
        wu16 = wu.astype(BF16)
        gh = by_head(_bmm_tn(bf("kt"), wu16))
        qo = _bmm(qk, wu16)
        qp = by_head((bat["qd"] - qo[:, :, :HEAD_DIM]).astype(BF16))
        op = by_head(qo[:, :, HEAD_DIM:])
        egl = by_head(bat["egl"])
        ro = by_head(ro)
        rh = by_head(_bmm_tn(bf("rkt"), rv16))
        rqd = by_head(bf("rqd"))
        s_cur = s_ref[0]
        r_cur = r_ref[0]
        o_dn = []
        o_ret = []
        for c in range(nblk):
            s16 = s_cur.astype(BF16)
            o_dn.append(op[:, c] + _bmm(qp[:, c], s16))
            s_cur = (s_cur * egl[:, c] - _bmm(gh[:, c, :, :HEAD_DIM].astype(BF16), s16)
                     + gh[:, c, :, HEAD_DIM:])
            o_ret.append(ro[:, c] + _bmm(rqd[:, c], r_cur.astype(BF16)))
            r_cur = r_cur * cdec_ref[...] + rh[:, c]
        s_ref[0] = s_cur
        r_ref[0] = r_cur
        head_rows = lambda parts, h: jnp.concatenate([p[h] for p in parts], axis=0)
        o_dn = [head_rows(o_dn, h) for h in range(N_HEADS)]
        o_ret = [head_rows(o_ret, h) for h in range(N_HEADS)]
    else:
        n_seq = N_HEADS * sb
        seqs = lambda t: t.reshape(n_seq, tt, t.shape[-1])
        s_old = jnp.concatenate([s0_ref[:, h] for h in range(N_HEADS)], axis=0)
        lhs = jnp.concatenate([seqs(wu[:, :, :HEAD_DIM]), seqs(bat["qd"])], axis=1).astype(BF16)
        res = _bmm(lhs, s_old.astype(BF16))
        v_new = seqs(wu[:, :, HEAD_DIM:]) - res[:, :tt]
        v16 = v_new.astype(BF16)
        s_new = s_old * bat["egl"] + _bmm_tn(seqs(bat["kt"]).astype(BF16), v16)
        o = res[:, tt:].reshape(n_bat, BLK, HEAD_DIM) + _bmm(qk, v16.reshape(n_bat, BLK, HEAD_DIM))
        r_old = jnp.concatenate([r0_ref[:, h] for h in range(N_HEADS)], axis=0)
        ro = ro + _bmm(seqs(bat["rqd"]).astype(BF16),
                       r_old.astype(BF16)).reshape(n_bat, BLK, HEAD_DIM)
        cdec = jnp.broadcast_to(cdec_ref[...][:, None], (N_HEADS, sb, 1, HEAD_DIM))
        r_new = (r_old * cdec.reshape(n_seq, 1, HEAD_DIM)
                 + _bmm_tn(seqs(bat["rkt"]).astype(BF16), seqs(bat["rv"]).astype(BF16)))
        for h in range(N_HEADS):
            s_ref[:, h] = s_new[h * sb:(h + 1) * sb]
            r_ref[:, h] = r_new[h * sb:(h + 1) * sb]
        o_dn = [o[h * nblk:(h + 1) * nblk].reshape(rows, HEAD_DIM) for h in range(N_HEADS)]
        o_ret = [ro[h * nblk:(h + 1) * nblk].reshape(rows, HEAD_DIM) for h in range(N_HEADS)]

    outs = []
    for h in range(N_HEADS):
        hs = slice(h * HEAD_DIM, (h + 1) * HEAD_DIM)
        outs.append(_rmsnorm(o_dn[h], dnw_ref[...]) * _silu(z[:, hs]))
    for h in range(N_HEADS):
        hs = slice(h * HEAD_DIM, (h + 1) * HEAD_DIM)
        o = o_ret[h]
        mu = jnp.mean(o, axis=-1, keepdims=True)
        var = jnp.mean(jnp.square(o - mu), axis=-1, keepdims=True)
        outs.append((o - mu) * lax.rsqrt(var + EPS) * retw_ref[:, hs] * _silu(rg[:, hs]))
    mix = jnp.concatenate(outs, axis=1).astype(BF16)
    h_new = x + jnp.dot(mix, wout_ref[...], preferred_element_type=F32)
    h_ref[...] = h_new.reshape(sb, tt, D_MODEL)


def _ffn_kernel(cfg, h_ref, p_ref, fst_ref, fnw_ref, wup_ref, fcw_ref, fcb_ref, wdown_ref,
                pnw_ref, wgate_ref, wple_ref, finw_ref,
                y_ref, fnew_ref, e_ref):
    sb, tt, final_norm = cfg
    rows = sb * tt
    l_idx = pl.program_id(1)
    base = SUBLANES - (FFN_CONV - 1)

    @pl.when(l_idx == 0)
    def _():
        e_ref[:, base:SUBLANES, :] = fst_ref[...]

    h = h_ref[...].reshape(rows, D_MODEL)
    m = _rmsnorm(h, fnw_ref[...]).astype(BF16)
    for half in (0, D_FF):
        for c0, cw in FF_SLABS:
            cs = slice(half + c0, half + c0 + cw)
            e_ref[:, SUBLANES:, cs] = jnp.dot(m, wup_ref[:, cs],
                                              preferred_element_type=F32).reshape(sb, tt, cw)

    def conv(cs):
        acc = e_ref[:, base:base + tt, cs] * fcw_ref[0:1, cs]
        for j in range(1, FFN_CONV):
            acc = acc + e_ref[:, base + j:base + j + tt, cs] * fcw_ref[j:j + 1, cs]
        return (acc + fcb_ref[:, cs]).reshape(rows, cs.stop - cs.start)

    acc = h
    for c0, cw in FF_SLABS:
        ug = conv(slice(c0, c0 + cw))
        uv = conv(slice(D_FF + c0, D_FF + c0 + cw))
        act = (_silu(ug) * uv).astype(BF16)
        acc = acc + jnp.dot(act, wdown_ref[c0:c0 + cw, :], preferred_element_type=F32)
    tail = e_ref[:, tt + base:tt + SUBLANES, :]
    fnew_ref[...] = tail
    e_ref[:, base:SUBLANES, :] = tail

    h = acc
    gate = jax.nn.sigmoid(jnp.dot(_rmsnorm(h, pnw_ref[...]).astype(BF16), wgate_ref[...],
                                  preferred_element_type=F32))
    pp = jnp.dot(p_ref[...].reshape(rows, PLE_DIM).astype(BF16), wple_ref[...],
                 preferred_element_type=F32)
    h = h + gate * pp
    if final_norm:
        h = _rmsnorm(h, finw_ref[...])
    y_ref[...] = h.reshape(sb, tt, D_MODEL)


def _const_spec(shape):
    zeros = (0,) * len(shape)
    return pl.BlockSpec(shape, lambda b, l: zeros, pipeline_mode=pl.Buffered(1))


def _retention_tables(chunk, rows):
    hh = np.arange(N_HEADS, dtype=np.float64)
    log_gamma = np.log(1.0 - 2.0 ** (-5.0 - hh))
    bpos = (np.arange(chunk, dtype=np.float64) + 1.0)[None, :] * log_gamma[:, None]
    causal = np.tril(np.ones((chunk, chunk), dtype=bool))
    dmat = np.where(causal, np.exp(np.where(causal, bpos[:, :, None] - bpos[:, None, :], 0.0)), 0.0)
    n = BLK // chunk
    dblk = np.einsum('ab,hij->haibj', np.eye(n), dmat).reshape(N_HEADS, BLK, BLK)
    qscale = np.exp(bpos)
    kscale = np.exp(bpos[:, -1:] - bpos)
    expand = lambda t: np.repeat(np.tile(t.T, (rows // chunk, 1)), HEAD_DIM, axis=1)
    rscale = np.stack([expand(qscale), expand(kscale)])
    cdec = np.broadcast_to(np.exp(bpos[:, -1]).reshape(N_HEADS, 1, 1), (N_HEADS, 1, HEAD_DIM))
    return tuple(jnp.asarray(t, F32) for t in (dblk, rscale, cdec))


def _chunk_sum_matrices(chunk, rows):
    r = np.arange(rows)
    same = (r[:, None] // chunk) == (r[None, :] // chunk)
    causal = same & (r[None, :] <= r[:, None])
    return jnp.asarray(np.stack([causal, same]), BF16)


def _rope_tables(pos0, tt, sb, n_l):
    inv = ROPE_BASE ** (-np.arange(0, HEAD_DIM, 2, dtype=np.float64) / HEAD_DIM)
    ang = (pos0 + np.arange(n_l * tt, dtype=np.float64))[:, None] * inv[None, :]
    cos = np.repeat(np.cos(ang), 2, axis=1)
    sin = np.stack([-np.sin(ang), np.sin(ang)], axis=-1).reshape(n_l * tt, HEAD_DIM)
    tile = lambda t: jnp.asarray(
        np.tile(t.reshape(n_l, 1, tt, HEAD_DIM), (1, sb, 1, 1)).reshape(n_l, sb * tt, HEAD_DIM), F32)
    return tile(cos), tile(sin)


def _mixer(x, cstate, s0, r0, pos0, wts, sb, tt):
    n_b, seq, _ = x.shape
    chunk = CHUNK if seq % CHUNK == 0 else seq
    chained = chunk == BLK
    rows = sb * tt
    n_l = seq // tt
    assert n_b % sb == 0 and seq % tt == 0 and rows % BLK == 0 and BLK % chunk == 0
    assert (sb == 1 and tt % chunk == 0) if chained else (tt == seq == chunk)
    cos, sin = _rope_tables(pos0, tt, sb, n_l)
    dmat, rscale, cdec = _retention_tables(chunk, rows)
    tri = _chunk_sum_matrices(chunk, rows)
    state_spec = pl.BlockSpec((sb, N_HEADS, HEAD_DIM, HEAD_DIM), lambda b, l: (b, 0, 0, 0))
    x_spec = pl.BlockSpec((sb, tt, D_MODEL), lambda b, l: (b, l, 0))
    conv_spec = pl.BlockSpec((sb, DN_CONV - 1, 3 * QK), lambda b, l: (b, 0, 0))
    rope_spec = pl.BlockSpec((1, rows, HEAD_DIM), lambda b, l: (l, 0, 0))
    consts = (tri, dmat, rscale, cdec) + tuple(wts)
    return pl.pallas_call(
        functools.partial(_mixer_kernel, (sb, tt, chunk, chained)),
        grid=(n_b // sb, n_l),
        in_specs=[x_spec, conv_spec, state_spec, state_spec, rope_spec, rope_spec]
        + [_const_spec(c.shape) for c in consts],
        out_specs=[x_spec, conv_spec, state_spec, state_spec],
        out_shape=[jax.ShapeDtypeStruct(x.shape, F32),
                   jax.ShapeDtypeStruct(cstate.shape, F32),
                   jax.ShapeDtypeStruct(s0.shape, F32),
                   jax.ShapeDtypeStruct(r0.shape, F32)],
        scratch_shapes=[pltpu.VMEM((sb, SUBLANES + tt, 3 * QK), F32)],
        compiler_params=pltpu.CompilerParams(
            dimension_semantics=("arbitrary", "arbitrary"), vmem_limit_bytes=VMEM_LIMIT),
        name="mixer",
    )(x, cstate, s0, r0, cos, sin, *consts)


def _ffn(h, p, fstate, wts, final_norm, sb, tt):
    n_b, seq, _ = h.shape
    n_l = seq // tt
    assert n_b % sb == 0 and seq % tt == 0
    x_spec = pl.BlockSpec((sb, tt, D_MODEL), lambda b, l: (b, l, 0))
    p_spec = pl.BlockSpec((sb, tt, PLE_DIM), lambda b, l: (b, l, 0))
    st_spec = pl.BlockSpec((sb, FFN_CONV - 1, 2 * D_FF), lambda b, l: (b, 0, 0))
    return pl.pallas_call(
        functools.partial(_ffn_kernel, (sb, tt, final_norm)),
        grid=(n_b // sb, n_l),
        in_specs=[x_spec, p_spec, st_spec] + [_const_spec(w.shape) for w in wts],
        out_specs=[x_spec, st_spec],
        out_shape=[jax.ShapeDtypeStruct(h.shape, F32), jax.ShapeDtypeStruct(fstate.shape, F32)],
        scratch_shapes=[pltpu.VMEM((sb, SUBLANES + tt, 2 * D_FF), F32)],
        compiler_params=pltpu.CompilerParams(
            dimension_semantics=("arbitrary", "arbitrary"), vmem_limit_bytes=VMEM_LIMIT),
        name="ffn",
    )(h, p, fstate, *wts)


def kernel(x_prompt, x_sample, p_prompt, p_sample, state_dn_conv, state_dn, state_ret, state_ffn_conv, attn_norm_w, w_in, dn_conv_w, dn_A_log, dn_dt_bias, dn_norm_w, ret_norm_w, w_out, ffn_norm_w, w_up, ffn_conv_w, ffn_conv_b, w_down, ple_norm_w, w_ple_gate, w_ple, final_norm_w):
    depth = w_in.shape[0]
    n_bp = x_prompt.shape[0]
    dec_len = x_sample.shape[1]
    row = lambda v: v.reshape(1, -1).astype(F32)
    hp, hs = x_prompt, x_sample
    outs_p = ([], [], [], [])
    outs_s = ([], [], [], [])
    for i in range(depth):
        wi = w_in[i].astype(BF16)
        wmain = jnp.concatenate([wi[:, :4 * QK], wi[:, 4 * QK + 2 * N_HEADS:]], axis=1)
        wba = jnp.pad(wi[:, 4 * QK:4 * QK + 2 * N_HEADS], ((0, 0), (0, LANES - 2 * N_HEADS)))
        gpar = jnp.zeros((2, LANES), F32)
        gpar = gpar.at[0, GATE_LANE0:GATE_LANE0 + N_HEADS].set(dn_A_log[i])
        gpar = gpar.at[1, GATE_LANE0:GATE_LANE0 + N_HEADS].set(dn_dt_bias[i])
        mixer_w = (row(attn_norm_w[i]), wmain, wba, dn_conv_w[i], gpar, row(dn_norm_w[i]),
                   row(ret_norm_w[i]), w_out[i].astype(BF16))
        ffn_w = (row(ffn_norm_w[i]), w_up[i].astype(BF16), ffn_conv_w[i], row(ffn_conv_b[i]),
                 w_down[i].astype(BF16), row(ple_norm_w[i]), w_ple_gate[i].astype(BF16),
                 w_ple[i].astype(BF16), row(final_norm_w))
        last = i == depth - 1
        zc = jnp.zeros((n_bp, DN_CONV - 1, 3 * QK), F32)
        zs = jnp.zeros((n_bp, N_HEADS, HEAD_DIM, HEAD_DIM), F32)
        zf = jnp.zeros((n_bp, FFN_CONV - 1, 2 * D_FF), F32)
        hp, c1, s1, r1 = _mixer(hp, zc, zs, zs, 0, mixer_w, *PROMPT_MIXER_TILE)
        hp, f1 = _ffn(hp, p_prompt[i], zf, ffn_w, last, *PROMPT_FFN_TILE)
        hs, c2, s2, r2 = _mixer(hs, state_dn_conv[i], state_dn[i], state_ret[i], PAST_LEN,
                                mixer_w, SAMPLE_MIXER_SEQS, dec_len)
        hs, f2 = _ffn(hs, p_sample[i], state_ffn_conv[i], ffn_w, last, SAMPLE_FFN_SEQS, dec_len)
        for lst, val in zip(outs_p, (c1, s1, r1, f1)):
            lst.append(val)
        for lst, val in zip(outs_s, (c2, s2, r2, f2)):
            lst.append(val)
    return (hp, hs,
            jnp.stack(outs_p[0]), jnp.stack(outs_p[1]), jnp.stack(outs_p[2]), jnp.stack(outs_p[3]),
            jnp.stack(outs_s[0]), jnp.stack(outs_s[1]), jnp.stack(outs_s[2]), jnp.stack(outs_s[3]))
```

```python
import functools
import math

import jax
import jax.numpy as jnp
import numpy as np
from jax import lax
from jax.experimental import pallas as pl
from jax.experimental.pallas import tpu as pltpu

F32 = jnp.float32
BF16 = jnp.bfloat16

D_MODEL = 1024
HEAD_DIM = 128
N_HEADS = 4
QK = N_HEADS * HEAD_DIM
DN_CONV = 4
FFN_CONV = 3
D_FF = 2816
PLE_DIM = 256
CHUNK = 64
PAST_LEN = 16384
ROPE_BASE = 10000.0
EPS = 1e-6
LANES = 128
SUBLANES = 8
MXU_DIM = 256
BLK = 64
GATE_LANE0 = N_HEADS
FF_SLABS = ((0, 1280), (1280, 1536))
VMEM_LIMIT = 56 * 1024 * 1024
PROMPT_MIXER_TILE = (1, 256)
PROMPT_FFN_TILE = (1, 512)
SAMPLE_MIXER_SEQS = 16
SAMPLE_FFN_SEQS = 32


def _split2(x):
    hi = x.astype(BF16)
    lo = (x - hi.astype(F32)).astype(BF16)
    return hi, lo


def _rmsnorm(x, w):
    return x * lax.rsqrt(jnp.mean(x * x, axis=-1, keepdims=True) + EPS) * w


def _silu(x):
    return x * jax.nn.sigmoid(x)


def _softplus(x):
    return jnp.maximum(x, 0.0) + jnp.log(1.0 + jnp.exp(-jnp.abs(x)))


def _lane_bcast(x, lane, width=LANES):
    return jnp.broadcast_to(x[:, lane:lane + 1], (x.shape[0], width))


def _bmm(a, b):
    return lax.dot_general(a, b, (((2,), (1,)), ((0,), (0,))), preferred_element_type=F32)


def _bmm_nt(a, b):
    return lax.dot_general(a, b, (((2,), (2,)), ((0,), (0,))), preferred_element_type=F32)


def _bmm_tn(a, b):
    return lax.dot_general(a, b, (((1,), (1,)), ((0,), (0,))), preferred_element_type=F32)


def _pack_lhs(m):
    hi = m.astype(BF16).astype(F32)
    return jnp.concatenate([hi, m - hi, hi], axis=-1).astype(BF16)


def _pack_rhs(m):
    hi, lo = _split2(m)
    return jnp.concatenate([hi, hi, lo], axis=-2)


def _unit_lower_inverse_packed(a, eye, n_steps):
    c = a.shape[-1]
    upper = lax.broadcasted_iota(jnp.int32, a.shape[:-1] + (2 * c,), a.ndim - 1) >= c
    a2 = _bmm(_pack_lhs(a), _pack_rhs(a))
    mp = jnp.concatenate([a2, eye - a], axis=-1)
    for _ in range(n_steps):
        res = _bmm(_pack_lhs(mp[..., :c]), _pack_rhs(mp))
        mp = res + jnp.where(upper, mp, 0.0)
    return mp[..., c:]


def _mixer_kernel(cfg, x_ref, cst_ref, s0_ref, r0_ref, cos_ref, sin_ref, tri_ref, dmat_ref,
                  rscale_ref, cdec_ref, anw_ref, wmain_ref, wba_ref, convw_ref, gpar_ref,
                  dnw_ref, retw_ref, wout_ref, h_ref, cnew_ref, s_ref, r_ref, e_ref):
    sb, tt, chunk, chained = cfg
    rows = sb * tt
    nblk = rows // BLK
    n_bat = N_HEADS * nblk
    shift = int(math.log2(chunk))
    n_steps = shift - 1
    base = SUBLANES - (DN_CONV - 1)
    l_idx = pl.program_id(1)

    @pl.when(l_idx == 0)
    def _():
        e_ref[:, base:SUBLANES, :] = cst_ref[...]
        if chained:
            s_ref[...] = s0_ref[...]
            r_ref[...] = r0_ref[...]

    x = x_ref[...].reshape(rows, D_MODEL)
    a = _rmsnorm(x, anw_ref[...]).astype(BF16)
    e_ref[:, SUBLANES:, :] = jnp.dot(a, wmain_ref[:, 0:3 * QK],
                                     preferred_element_type=F32).reshape(sb, tt, 3 * QK)
    proj = lambda i: jnp.dot(a, wmain_ref[:, (3 + i) * QK:(4 + i) * QK], preferred_element_type=F32)
    pba = jnp.dot(a, wba_ref[...], preferred_element_type=F32)

    conv = e_ref[:, base:base + tt, :] * convw_ref[0:1, :]
    for j in range(1, DN_CONV):
        conv = conv + e_ref[:, base + j:base + j + tt, :] * convw_ref[j:j + 1, :]
    tail = e_ref[:, tt + base:tt + SUBLANES, :]
    cnew_ref[...] = tail
    e_ref[:, base:SUBLANES, :] = tail
    rq = proj(1)
    qkv = _silu(conv).reshape(rows, 3 * QK)
    rk = proj(2)

    beta_all = jax.nn.sigmoid(pba)
    g_all = -jnp.exp(gpar_ref[0:1, :]) * _softplus(pba + gpar_ref[1:2, :])
    g_hi = g_all.astype(BF16)
    g_r1 = g_all - g_hi.astype(F32)
    g_mid = g_r1.astype(BF16)
    g_lo = (g_r1 - g_mid.astype(F32)).astype(BF16)
    g_split = jnp.concatenate([t.reshape(nblk, BLK, LANES) for t in (g_hi, g_mid, g_lo)], axis=1)
    sums = _bmm(jnp.broadcast_to(tri_ref[...][None], (nblk, 2 * BLK, 3 * BLK)), g_split)
    gc_all = sums[:, :BLK].reshape(rows, LANES)
    gl_all = sums[:, BLK:].reshape(rows, LANES)
    gct_all = gc_all.T

    ri = lax.broadcasted_iota(jnp.int32, (BLK, BLK), 0)
    ci = lax.broadcasted_iota(jnp.int32, (BLK, BLK), 1)
    same = (ri >> shift) == (ci >> shift)
    causal = same & (ci <= ri)
    strict = same & (ci < ri)
    eye = (ri == ci).astype(F32)

    cos = cos_ref[0]
    sin = sin_ref[0]
    even = (lax.broadcasted_iota(jnp.int32, (rows, LANES), 1) & 1) == 0

    def rotary(t):
        swapped = jnp.where(even, pltpu.roll(t, LANES - 1, 1), pltpu.roll(t, 1, 1))
        return t * cos + swapped * sin

    per = {k: [] for k in ("qb", "kb", "kbeta", "rhs", "qd", "kt", "gcc", "gct", "egl",
                           "rq", "rk", "rqd", "rkt", "rv")}
    blocks = lambda t: t.reshape(nblk, BLK, t.shape[-1])
    for h in range(N_HEADS):
        qh = qkv[:, h * HEAD_DIM:(h + 1) * HEAD_DIM]
        kh = qkv[:, QK + h * HEAD_DIM:QK + (h + 1) * HEAD_DIM]
        vh = qkv[:, 2 * QK + h * HEAD_DIM:2 * QK + (h + 1) * HEAD_DIM]
        qb = qh * lax.rsqrt(jnp.sum(qh * qh, axis=-1, keepdims=True) + EPS) * (HEAD_DIM ** -0.5)
        kb = kh * lax.rsqrt(jnp.sum(kh * kh, axis=-1, keepdims=True) + EPS)
        bcol = _lane_bcast(beta_all, h)
        gcc = _lane_bcast(gc_all, GATE_LANE0 + h)
        glc = _lane_bcast(gl_all, GATE_LANE0 + h)
        egc = jnp.exp(gcc)
        kbeta = kb * bcol
        per["qb"].append(blocks(qb))
        per["kb"].append(blocks(kb))
        per["kbeta"].append(blocks(kbeta))
        per["rhs"].append(blocks(jnp.concatenate([kbeta * egc, vh * bcol], axis=1)))
        per["qd"].append(blocks(qb * egc))
        per["kt"].append(blocks(kb * jnp.exp(glc - gcc)))
        per["gcc"].append(blocks(gcc))
        per["egl"].append(jnp.exp(glc).reshape(rows // chunk, chunk, HEAD_DIM)[:, 0:1, :])
        per["gct"].extend(gct_all[GATE_LANE0 + h:GATE_LANE0 + h + 1, b * BLK:(b + 1) * BLK][None]
                          for b in range(nblk))
    bat = {k: jnp.concatenate(v, axis=0) for k, v in per.items() if v}
    bf = lambda k: bat[k].astype(BF16)
    rv = proj(3)

    delta = bat["gcc"][:, :, :BLK] - bat["gct"]
    decay = jnp.exp(jnp.where(causal, delta, -jnp.inf))
    kb16 = bf("kb")
    kq = _bmm_nt(jnp.concatenate([bat["kbeta"], bat["qb"]], axis=1).astype(BF16), kb16)
    amat = jnp.where(strict, kq[:, :BLK] * decay, 0.0)
    tinv = _unit_lower_inverse_packed(amat, eye, n_steps)
    z = proj(0)
    for h in range(N_HEADS):
        hs = slice(h * HEAD_DIM, (h + 1) * HEAD_DIM)
        rqh = rotary(rq[:, hs])
        rkh = rotary(rk[:, hs]) * (HEAD_DIM ** -0.5)
        per["rq"].append(blocks(rqh))
        per["rk"].append(blocks(rkh))
        per["rqd"].append(blocks(rqh * rscale_ref[0, :, hs]))
        per["rkt"].append(blocks(rkh * rscale_ref[1, :, hs]))
        per["rv"].append(blocks(rv[:, hs]))
    bat.update({k: jnp.concatenate(per[k], axis=0) for k in ("rq", "rk", "rqd", "rkt", "rv")})
    rg = proj(4)
    wu = _bmm(_pack_lhs(tinv), _pack_rhs(bat["rhs"]))
    qk = (kq[:, BLK:] * decay).astype(BF16)
    rv16 = bf("rv")
    dmat = jnp.broadcast_to(dmat_ref[...][:, None], (N_HEADS, nblk, BLK, BLK)).reshape(n_bat, BLK, BLK)
    rqk = (_bmm_nt(bf("rq"), bf("rk")) * dmat).astype(BF16)

    if chained:
        by_head = lambda t: t.reshape((N_HEADS, nblk) + t.shape[1:])
        transposed16 = lambda k: jnp.swapaxes(bat[k], 1, 2).astype(BF16)
        qg_wu = _bmm(jnp.concatenate([qk, transposed16("kt")], axis=1), wu.astype(BF16))
        qo = qg_wu[:, :BLK]
        gh = by_head(qg_wu[:, BLK:])
        qp = by_head((bat["qd"] - qo[:, :, :HEAD_DIM]).astype(BF16))
        op = by_head(qo[:, :, HEAD_DIM:])
        egl = by_head(bat["egl"])
        ro_rh = _bmm(jnp.concatenate([rqk, transposed16("rkt")], axis=1), rv16)
        ro = by_head(ro_rh[:, :BLK])
        rh = by_head(ro_rh[:, BLK:])
        rqd = by_head(bf("rqd"))
        s_cur = s_ref[0]
        r_cur = r_ref[0]
        o_dn = []
        o_ret = []
        for c in range(nblk):
            s16 = s_cur.astype(BF16)
            qg = jnp.concatenate([qp[:, c], gh[:, c, :, :HEAD_DIM].astype(BF16)], axis=1)
            qgs = _bmm(qg, s16)
            o_dn.append(op[:, c] + qgs[:, :BLK])
            s_cur = s_cur * egl[:, c] - qgs[:, BLK:] + gh[:, c, :, HEAD_DIM:]
            o_ret.append(ro[:, c] + _bmm(rqd[:, c], r_cur.astype(BF16)))
            r_cur = r_cur * cdec_ref[...] + rh[:, c]
        s_ref[0] = s_cur
        r_ref[0] = r_cur
        head_rows = lambda parts, h: jnp.concatenate([p[h] for p in parts], axis=0)
        o_dn = [head_rows(o_dn, h) for h in range(N_HEADS)]
        o_ret = [head_rows(o_ret, h) for h in range(N_HEADS)]
    else:
        n_seq = N_HEADS * sb
        seqs = lambda t: t.reshape(n_seq, tt, t.shape[-1])
        s_old = jnp.concatenate([s0_ref[:, h] for h in range(N_HEADS)], axis=0)
        lhs = jnp.concatenate([seqs(wu[:, :, :HEAD_DIM]), seqs(bat["qd"])], axis=1).astype(BF16)
        res = _bmm(lhs, s_old.astype(BF16))
        v_new = seqs(wu[:, :, HEAD_DIM:]) - res[:, :tt]
        v16 = v_new.astype(BF16)
        s_new = s_old * bat["egl"] + _bmm_tn(seqs(bat["kt"]).astype(BF16), v16)
        o = res[:, tt:].reshape(n_bat, BLK, HEAD_DIM) + _bmm(qk, v16.reshape(n_bat, BLK, HEAD_DIM))
        r_old = jnp.concatenate([r0_ref[:, h] for h in range(N_HEADS)], axis=0)
        ro = _bmm(rqk, rv16) + _bmm(seqs(bat["rqd"]).astype(BF16),
                                    r_old.astype(BF16)).reshape(n_bat, BLK, HEAD_DIM)
        cdec = jnp.broadcast_to(cdec_ref[...][:, None], (N_HEADS, sb, 1, HEAD_DIM))
        r_new = (r_old * cdec.reshape(n_seq, 1, HEAD_DIM)
                 + _bmm_tn(seqs(bat["rkt"]).astype(BF16), seqs(bat["rv"]).astype(BF16)))
        for h in range(N_HEADS):
            s_ref[:, h] = s_new[h * sb:(h + 1) * sb]
            r_ref[:, h] = r_new[h * sb:(h + 1) * sb]
        o_dn = [o[h * nblk:(h + 1) * nblk].reshape(rows, HEAD_DIM) for h in range(N_HEADS)]
        o_ret = [ro[h * nblk:(h + 1) * nblk].reshape(rows, HEAD_DIM) for h in range(N_HEADS)]

    outs = []
    for h in range(N_HEADS):
        hs = slice(h * HEAD_DIM, (h + 1) * HEAD_DIM)
        outs.append(_rmsnorm(o_dn[h], dnw_ref[...]) * _silu(z[:, hs]))
    for h in range(N_HEADS):
        hs = slice(h * HEAD_DIM, (h + 1) * HEAD_DIM)
        o = o_ret[h]
        mu = jnp.mean(o, axis=-1, keepdims=True)
        var = jnp.mean(jnp.square(o - mu), axis=-1, keepdims=True)
        outs.append((o - mu) * lax.rsqrt(var + EPS) * retw_ref[:, hs] * _silu(rg[:, hs]))
    mix = jnp.concatenate(outs, axis=1).astype(BF16)
    h_new = x + jnp.dot(mix, wout_ref[...], preferred_element_type=F32)
    h_ref[...] = h_new.reshape(sb, tt, D_MODEL)


def _ffn_kernel(cfg, h_ref, p_ref, fst_ref, fnw_ref, wup_ref, fcw_ref, fcb_ref, wdown_ref,
                pnw_ref, wgate_ref, wple_ref, finw_ref,
                y_ref, fnew_ref, e_ref):
    sb, tt, final_norm = cfg
    rows = sb * tt
    l_idx = pl.program_id(1)
    base = SUBLANES - (FFN_CONV - 1)

    @pl.when(l_idx == 0)
    def _():
        e_ref[:, base:SUBLANES, :] = fst_ref[...]

    h = h_ref[...].reshape(rows, D_MODEL)
    m = _rmsnorm(h, fnw_ref[...]).astype(BF16)
    for half in (0, D_FF):
        for c0, cw in FF_SLABS:
            cs = slice(half + c0, half + c0 + cw)
            e_ref[:, SUBLANES:, cs] = jnp.dot(m, wup_ref[:, cs],
                                              preferred_element_type=F32).reshape(sb, tt, cw)

    def conv(cs):
        acc = e_ref[:, base:base + tt, cs] * fcw_ref[0:1, cs]
        for j in range(1, FFN_CONV):
            acc = acc + e_ref[:, base + j:base + j + tt, cs] * fcw_ref[j:j + 1, cs]
        return (acc + fcb_ref[:, cs]).reshape(rows, cs.stop - cs.start)

    acc = h
    for c0, cw in FF_SLABS:
        ug = conv(slice(c0, c0 + cw))
        uv = conv(slice(D_FF + c0, D_FF + c0 + cw))
        act = (_silu(ug) * uv).astype(BF16)
        acc = acc + jnp.dot(act, wdown_ref[c0:c0 + cw, :], preferred_element_type=F32)
    tail = e_ref[:, tt + base:tt + SUBLANES, :]
    fnew_ref[...] = tail
    e_ref[:, base:SUBLANES, :] = tail

    h = acc
    gate = jax.nn.sigmoid(jnp.dot(_rmsnorm(h, pnw_ref[...]).astype(BF16), wgate_ref[...],
                                  preferred_element_type=F32))
    pp = jnp.dot(p_ref[...].reshape(rows, PLE_DIM).astype(BF16), wple_ref[...],
                 preferred_element_type=F32)
    h = h + gate * pp
    if final_norm:
        h = _rmsnorm(h, finw_ref[...])
    y_ref[...] = h.reshape(sb, tt, D_MODEL)


def _const_spec(shape):
    zeros = (0,) * len(shape)
    return pl.BlockSpec(shape, lambda b, l: zeros, pipeline_mode=pl.Buffered(1))


def _retention_tables(chunk, rows):
    hh = np.arange(N_HEADS, dtype=np.float64)
    log_gamma = np.log(1.0 - 2.0 ** (-5.0 - hh))
    bpos = (np.arange(chunk, dtype=np.float64) + 1.0)[None, :] * log_gamma[:, None]
    causal = np.tril(np.ones((chunk, chunk), dtype=bool))
    dmat = np.where(causal, np.exp(np.where(causal, bpos[:, :, None] - bpos[:, None, :], 0.0)), 0.0)
    n = BLK // chunk
    dblk = np.einsum('ab,hij->haibj', np.eye(n), dmat).reshape(N_HEADS, BLK, BLK)
    qscale = np.exp(bpos)
    kscale = np.exp(bpos[:, -1:] - bpos)
    expand = lambda t: np.repeat(np.tile(t.T, (rows // chunk, 1)), HEAD_DIM, axis=1)
    rscale = np.stack([expand(qscale), expand(kscale)])
    cdec = np.broadcast_to(np.exp(bpos[:, -1]).reshape(N_HEADS, 1, 1), (N_HEADS, 1, HEAD_DIM))
    return tuple(jnp.asarray(t, F32) for t in (dblk, rscale, cdec))


def _chunk_sum_matrix(chunk):
    r = np.arange(BLK)
    same = (r[:, None] // chunk) == (r[None, :] // chunk)
    causal = same & (r[None, :] <= r[:, None])
    return jnp.asarray(np.tile(np.concatenate([causal, same], axis=0), (1, 3)), BF16)


def _rope_tables(pos0, tt, sb, n_l):
    inv = ROPE_BASE ** (-np.arange(0, HEAD_DIM, 2, dtype=np.float64) / HEAD_DIM)
    ang = (pos0 + np.arange(n_l * tt, dtype=np.float64))[:, None] * inv[None, :]
    cos = np.repeat(np.cos(ang), 2, axis=1)
    sin = np.stack([-np.sin(ang), np.sin(ang)], axis=-1).reshape(n_l * tt, HEAD_DIM)
    tile = lambda t: jnp.asarray(
        np.tile(t.reshape(n_l, 1, tt, HEAD_DIM), (1, sb, 1, 1)).reshape(n_l, sb * tt, HEAD_DIM), F32)
    return tile(cos), tile(sin)


def _mixer(x, cstate, s0, r0, pos0, wts, sb, tt):
    n_b, seq, _ = x.shape
    chunk = CHUNK if seq % CHUNK == 0 else seq
    chained = chunk == BLK
    rows = sb * tt
    n_l = seq // tt
    assert n_b % sb == 0 and seq % tt == 0 and rows % BLK == 0 and BLK % chunk == 0
    assert (sb == 1 and tt % chunk == 0) if chained else (tt == seq == chunk)
    cos, sin = _rope_tables(pos0, tt, sb, n_l)
    dmat, rscale, cdec = _retention_tables(chunk, rows)
    tri = _chunk_sum_matrix(chunk)
    state_spec = pl.BlockSpec((sb, N_HEADS, HEAD_DIM, HEAD_DIM), lambda b, l: (b, 0, 0, 0))
    x_spec = pl.BlockSpec((sb, tt, D_MODEL), lambda b, l: (b, l, 0))
    conv_spec = pl.BlockSpec((sb, DN_CONV - 1, 3 * QK), lambda b, l: (b, 0, 0))
    rope_spec = pl.BlockSpec((1, rows, HEAD_DIM), lambda b, l: (l, 0, 0))
    consts = (tri, dmat, rscale, cdec) + tuple(wts)
    return pl.pallas_call(
        functools.partial(_mixer_kernel, (sb, tt, chunk, chained)),
        grid=(n_b // sb, n_l),
        in_specs=[x_spec, conv_spec, state_spec, state_spec, rope_spec, rope_spec]
        + [_const_spec(c.shape) for c in consts],
        out_specs=[x_spec, conv_spec, state_spec, state_spec],
        out_shape=[jax.ShapeDtypeStruct(x.shape, F32),
                   jax.ShapeDtypeStruct(cstate.shape, F32),
                   jax.ShapeDtypeStruct(s0.shape, F32),
                   jax.ShapeDtypeStruct(r0.shape, F32)],
        scratch_shapes=[pltpu.VMEM((sb, SUBLANES + tt, 3 * QK), F32)],
        compiler_params=pltpu.CompilerParams(
            dimension_semantics=("arbitrary", "arbitrary"), vmem_limit_bytes=VMEM_LIMIT),
        name="mixer",
    )(x, cstate, s0, r0, cos, sin, *consts)


def _ffn(h, p, fstate, wts, final_norm, sb, tt):
    n_b, seq, _ = h.shape
    n_l = seq // tt
    assert n_b % sb == 0 and seq % tt == 0
    x_spec = pl.BlockSpec((sb, tt, D_MODEL), lambda b, l: (b, l, 0))
    p_spec = pl.BlockSpec((sb, tt, PLE_DIM), lambda b, l: (b, l, 0))
    st_spec = pl.BlockSpec((sb, FFN_CONV - 1, 2 * D_FF), lambda b, l: (b, 0, 0))
    return pl.pallas_call(
        functools.partial(_ffn_kernel, (sb, tt, final_norm)),
        grid=(n_b // sb, n_l),
        in_specs=[x_spec, p_spec, st_spec] + [_const_spec(w.shape) for w in wts],
        out_specs=[x_spec, st_spec],
        out_shape=[jax.ShapeDtypeStruct(h.shape, F32), jax.ShapeDtypeStruct(fstate.shape, F32)],
        scratch_shapes=[pltpu.VMEM((sb, SUBLANES + tt, 2 * D_FF), F32)],
        compiler_params=pltpu.CompilerParams(
            dimension_semantics=("arbitrary", "arbitrary"), vmem_limit_bytes=VMEM_LIMIT),
        name="ffn",
    )(h, p, fstate, *wts)


def kernel(x_prompt, x_sample, p_prompt, p_sample, state_dn_conv, state_dn, state_ret, state_ffn_conv, attn_norm_w, w_in, dn_conv_w, dn_A_log, dn_dt_bias, dn_norm_w, ret_norm_w, w_out, ffn_norm_w, w_up, ffn_conv_w, ffn_conv_b, w_down, ple_norm_w, w_ple_gate, w_ple, final_norm_w):
    depth = w_in.shape[0]
    n_bp = x_prompt.shape[0]
    dec_len = x_sample.shape[1]
    row = lambda v: v.reshape(1, -1).astype(F32)
    hp, hs = x_prompt, x_sample
    outs_p = ([], [], [], [])
    outs_s = ([], [], [], [])
    for i in range(depth):
        wi = w_in[i].astype(BF16)
        wmain = jnp.concatenate([wi[:, :4 * QK], wi[:, 4 * QK + 2 * N_HEADS:]], axis=1)
        wba = jnp.pad(wi[:, 4 * QK:4 * QK + 2 * N_HEADS], ((0, 0), (0, LANES - 2 * N_HEADS)))
        gpar = jnp.zeros((2, LANES), F32)
        gpar = gpar.at[0, GATE_LANE0:GATE_LANE0 + N_HEADS].set(dn_A_log[i])
        gpar = gpar.at[1, GATE_LANE0:GATE_LANE0 + N_HEADS].set(dn_dt_bias[i])
        mixer_w = (row(attn_norm_w[i]), wmain, wba, dn_conv_w[i], gpar, row(dn_norm_w[i]),
                   row(ret_norm_w[i]), w_out[i].astype(BF16))
        ffn_w = (row(ffn_norm_w[i]), w_up[i].astype(BF16), ffn_conv_w[i], row(ffn_conv_b[i]),
                 w_down[i].astype(BF16), row(ple_norm_w[i]), w_ple_gate[i].astype(BF16),
                 w_ple[i].astype(BF16), row(final_norm_w))
        last = i == depth - 1
        zc = jnp.zeros((n_bp, DN_CONV - 1, 3 * QK), F32)
        zs = jnp.zeros((n_bp, N_HEADS, HEAD_DIM, HEAD_DIM), F32)
        zf = jnp.zeros((n_bp, FFN_CONV - 1, 2 * D_FF), F32)
        hp, c1, s1, r1 = _mixer(hp, zc, zs, zs, 0, mixer_w, *PROMPT_MIXER_TILE)
        hp, f1 = _ffn(hp, p_prompt[i], zf, ffn_w, last, *PROMPT_FFN_TILE)
        hs, c2, s2, r2 = _mixer(hs, state_dn_conv[i], state_dn[i], state_ret[i], PAST_LEN,
                                mixer_w, SAMPLE_MIXER_SEQS, dec_len)
        hs, f2 = _ffn(hs, p_sample[i], state_ffn_conv[i], ffn_w, last, SAMPLE_FFN_SEQS, dec_len)
        for lst, val in zip(outs_p, (c1, s1, r1, f1)):
            lst.append(val)
        for lst, val in zip(outs_s, (c2, s2, r2, f2)):
            lst.append(val)
    return (hp, hs,
            jnp.stack(outs_p[0]), jnp.stack(outs_p[1]), jnp.stack(outs_p[2]), jnp.stack(outs_p[3]),
            jnp.stack(outs_s[0]), jnp.stack(outs_s[1]), jnp.stack(outs_s[2]), jnp.stack(outs_s[3]))
```

```python
import functools
import math

import jax
import jax.numpy as jnp
import numpy as np
from jax import lax
from jax.experimental import pallas as pl
from jax.experimental.pallas import tpu as pltpu

F32 = jnp.float32
BF16 = jnp.bfloat16

D_MODEL = 1024
HEAD_DIM = 128
N_HEADS = 4
QK = N_HEADS * HEAD_DIM
DN_CONV = 4
FFN_CONV = 3
D_FF = 2816
PLE_DIM = 256
CHUNK = 64
PAST_LEN = 16384
ROPE_BASE = 10000.0
EPS = 1e-6
LANES = 128
SUBLANES = 8
MXU_DIM = 256
BLK = 64
GATE_LANE0 = N_HEADS
FF_SLABS = ((0, 1280), (1280, 1536))
VMEM_LIMIT = 56 * 1024 * 1024
PROMPT_MIXER_TILE = (1, 256)
PROMPT_FFN_TILE = (1, 512)
SAMPLE_MIXER_SEQS = 16
SAMPLE_FFN_SEQS = 32


def _split2(x):
    hi = x.astype(BF16)
    lo = (x - hi.astype(F32)).astype(BF16)
    return hi, lo


def _rmsnorm(x, w):
    return x * lax.rsqrt(jnp.mean(x * x, axis=-1, keepdims=True) + EPS) * w


def _silu(x):
    return x * jax.nn.sigmoid(x)


def _softplus(x):
    return jnp.maximum(x, 0.0) + jnp.log(1.0 + jnp.exp(-jnp.abs(x)))


def _lane_bcast(x, lane, width=LANES):
    return jnp.broadcast_to(x[:, lane:lane + 1], (x.shape[0], width))


def _bmm(a, b):
    return lax.dot_general(a, b, (((2,), (1,)), ((0,), (0,))), preferred_element_type=F32)


def _bmm_nt(a, b):
    return lax.dot_general(a, b, (((2,), (2,)), ((0,), (0,))), preferred_element_type=F32)


def _bmm_tn(a, b):
    return lax.dot_general(a, b, (((1,), (1,)), ((0,), (0,))), preferred_element_type=F32)


def _pack_lhs(m):
    hi = m.astype(BF16).astype(F32)
    return jnp.concatenate([hi, m - hi, hi], axis=-1).astype(BF16)


def _pack_rhs(m):
    hi, lo = _split2(m)
    return jnp.concatenate([hi, hi, lo], axis=-2)


def _unit_lower_inverse_packed(a, eye, n_steps):
    c = a.shape[-1]
    upper = lax.broadcasted_iota(jnp.int32, a.shape[:-1] + (2 * c,), a.ndim - 1) >= c
    a2 = _bmm(_pack_lhs(a), _pack_rhs(a))
    mp = jnp.concatenate([a2, eye - a], axis=-1)
    for _ in range(n_steps):
        res = _bmm(_pack_lhs(mp[..., :c]), _pack_rhs(mp))
        mp = res + jnp.where(upper, mp, 0.0)
    return mp[..., c:]


def _mixer_kernel(cfg, x_ref, cst_ref, s0_ref, r0_ref, cos_ref, sin_ref, tri_ref, dmat_ref,
                  rscale_ref, cdec_ref, anw_ref, wmain_ref, wba_ref, convw_ref, gpar_ref,
                  dnw_ref, retw_ref, wout_ref, h_ref, cnew_ref, s_ref, r_ref, e_ref):
    sb, tt, chunk, chained = cfg
    rows = sb * tt
    nblk = rows // BLK
    n_bat = N_HEADS * nblk
    shift = int(math.log2(chunk))
    n_steps = shift - 1
    base = SUBLANES - (DN_CONV - 1)
    l_idx = pl.program_id(1)

    @pl.when(l_idx == 0)
    def _():
        e_ref[:, base:SUBLANES, :] = cst_ref[...]
        if chained:
            s_ref[...] = s0_ref[...]
            r_ref[...] = r0_ref[...]

    x = x_ref[...].reshape(rows, D_MODEL)
    a = _rmsnorm(x, anw_ref[...]).astype(BF16)
    e_ref[:, SUBLANES:, :] = jnp.dot(a, wmain_ref[:, 0:3 * QK],
                                     preferred_element_type=F32).reshape(sb, tt, 3 * QK)
    proj = lambda i: jnp.dot(a, wmain_ref[:, (3 + i) * QK:(4 + i) * QK], preferred_element_type=F32)
    pba = jnp.dot(a, wba_ref[...], preferred_element_type=F32)

    conv = e_ref[:, base:base + tt, :] * convw_ref[0:1, :]
    for j in range(1, DN_CONV):
        conv = conv + e_ref[:, base + j:base + j + tt, :] * convw_ref[j:j + 1, :]
    tail = e_ref[:, tt + base:tt + SUBLANES, :]
    cnew_ref[...] = tail
    e_ref[:, base:SUBLANES, :] = tail
    rq = proj(1)
    qkv = _silu(conv).reshape(rows, 3 * QK)
    rk = proj(2)

    beta_all = jax.nn.sigmoid(pba)
    g_all = -jnp.exp(gpar_ref[0:1, :]) * _softplus(pba + gpar_ref[1:2, :])
    g_hi = g_all.astype(BF16)
    g_r1 = g_all - g_hi.astype(F32)
    g_mid = g_r1.astype(BF16)
    g_lo = (g_r1 - g_mid.astype(F32)).astype(BF16)
    g_split = jnp.concatenate([t.reshape(nblk, BLK, LANES) for t in (g_hi, g_mid, g_lo)], axis=1)
    sums = _bmm(jnp.broadcast_to(tri_ref[...][None], (nblk, 2 * BLK, 3 * BLK)), g_split)
    gc_all = sums[:, :BLK].reshape(rows, LANES)
    gl_all = sums[:, BLK:].reshape(rows, LANES)
    gct_all = gc_all.T

    ri = lax.broadcasted_iota(jnp.int32, (BLK, BLK), 0)
    ci = lax.broadcasted_iota(jnp.int32, (BLK, BLK), 1)
    same = (ri >> shift) == (ci >> shift)
    causal = same & (ci <= ri)
    strict = same & (ci < ri)
    eye = (ri == ci).astype(F32)

    cos = cos_ref[0]
    sin = sin_ref[0]
    even = (lax.broadcasted_iota(jnp.int32, (rows, LANES), 1) & 1) == 0

    def rotary(t):
        swapped = jnp.where(even, pltpu.roll(t, LANES - 1, 1), pltpu.roll(t, 1, 1))
        return t * cos + swapped * sin

    per = {k: [] for k in ("qb", "kb", "kbeta", "rhs", "qd", "kt", "gcc", "gct", "egl",
                           "rq", "rk", "rqd", "rkt", "rv")}
    blocks = lambda t: t.reshape(nblk, BLK, t.shape[-1])
    rblk = rows if chained else BLK
    rblocks = lambda t: t.reshape(rows // rblk, rblk, t.shape[-1])
    for h in range(N_HEADS):
        qh = qkv[:, h * HEAD_DIM:(h + 1) * HEAD_DIM]
        kh = qkv[:, QK + h * HEAD_DIM:QK + (h + 1) * HEAD_DIM]
        vh = qkv[:, 2 * QK + h * HEAD_DIM:2 * QK + (h + 1) * HEAD_DIM]
        qb = qh * lax.rsqrt(jnp.sum(qh * qh, axis=-1, keepdims=True) + EPS) * (HEAD_DIM ** -0.5)
        kb = kh * lax.rsqrt(jnp.sum(kh * kh, axis=-1, keepdims=True) + EPS)
        bcol = _lane_bcast(beta_all, h)
        gcc = _lane_bcast(gc_all, GATE_LANE0 + h)
        glc = _lane_bcast(gl_all, GATE_LANE0 + h)
        egc = jnp.exp(gcc)
        kbeta = kb * bcol
        per["qb"].append(blocks(qb))
        per["kb"].append(blocks(kb))
        per["kbeta"].append(blocks(kbeta))
        per["rhs"].append(blocks(jnp.concatenate([kbeta * egc, vh * bcol], axis=1)))
        per["qd"].append(blocks(qb * egc))
        per["kt"].append(blocks(kb * jnp.exp(glc - gcc)))
        per["gcc"].append(blocks(gcc))
        per["egl"].append(jnp.exp(glc).reshape(rows // chunk, chunk, HEAD_DIM)[:, 0:1, :])
        per["gct"].extend(gct_all[GATE_LANE0 + h:GATE_LANE0 + h + 1, b * BLK:(b + 1) * BLK][None]
                          for b in range(nblk))
    bat = {k: jnp.concatenate(v, axis=0) for k, v in per.items() if v}
    bf = lambda k: bat[k].astype(BF16)
    rv = proj(3)

    delta = bat["gcc"][:, :, :BLK] - bat["gct"]
    decay = jnp.exp(jnp.where(causal, delta, -jnp.inf))
    kb16 = bf("kb")
    kq = _bmm_nt(jnp.concatenate([bat["kbeta"], bat["qb"]], axis=1).astype(BF16), kb16)
    amat = jnp.where(strict, kq[:, :BLK] * decay, 0.0)
    tinv = _unit_lower_inverse_packed(amat, eye, n_steps)
    z = proj(0)
    for h in range(N_HEADS):
        hs = slice(h * HEAD_DIM, (h + 1) * HEAD_DIM)
        rqh = rotary(rq[:, hs])
        rkh = rotary(rk[:, hs]) * (HEAD_DIM ** -0.5)
        per["rq"].append(rblocks(rqh))
        per["rk"].append(rblocks(rkh))
        per["rqd"].append(rblocks(rqh * rscale_ref[0, :, hs]))
        per["rkt"].append(rblocks(rkh * rscale_ref[1, :, hs]))
        per["rv"].append(rblocks(rv[:, hs]))
    bat.update({k: jnp.concatenate(per[k], axis=0) for k in ("rq", "rk", "rqd", "rkt", "rv")})
    rg = proj(4)
    wu = _bmm(_pack_lhs(tinv), _pack_rhs(bat["rhs"]))
    qk = (kq[:, BLK:] * decay).astype(BF16)
    rv16 = bf("rv")
    dmat = jnp.broadcast_to(dmat_ref[...][:, None], (N_HEADS, rows // rblk, rblk, rblk))
    rqk = (_bmm_nt(bf("rq"), bf("rk")) * dmat.reshape(-1, rblk, rblk)).astype(BF16)

    if chained:
        by_head = lambda t: t.reshape((N_HEADS, nblk) + t.shape[1:])
        transposed16 = lambda k: jnp.swapaxes(bat[k], 1, 2).astype(BF16)
        qg_wu = _bmm(jnp.concatenate([qk, transposed16("kt")], axis=1), wu.astype(BF16))
        qo = qg_wu[:, :BLK]
        gh = by_head(qg_wu[:, BLK:])
        qp = by_head((bat["qd"] - qo[:, :, :HEAD_DIM]).astype(BF16))
        op = by_head(qo[:, :, HEAD_DIM:])
        egl = by_head(bat["egl"])
        ro_rh = _bmm(jnp.concatenate([rqk, transposed16("rkt")], axis=1), rv16)
        r_old = r_ref[0]
        o_ret = ro_rh[:, :rows] + _bmm(bf("rqd"), r_old.astype(BF16))
        r_ref[0] = r_old * cdec_ref[...] + ro_rh[:, rows:]
        s_cur = s_ref[0]
        o_dn = []
        for c in range(nblk):
            s16 = s_cur.astype(BF16)
            qg = jnp.concatenate([qp[:, c], gh[:, c, :, :HEAD_DIM].astype(BF16)], axis=1)
            qgs = _bmm(qg, s16)
            o_dn.append(op[:, c] + qgs[:, :BLK])
            s_cur = s_cur * egl[:, c] - qgs[:, BLK:] + gh[:, c, :, HEAD_DIM:]
        s_ref[0] = s_cur
        o_dn = [jnp.concatenate([part[h] for part in o_dn], axis=0) for h in range(N_HEADS)]
        o_ret = [o_ret[h] for h in range(N_HEADS)]
    else:
        n_seq = N_HEADS * sb
        seqs = lambda t: t.reshape(n_seq, tt, t.shape[-1])
        s_old = jnp.concatenate([s0_ref[:, h] for h in range(N_HEADS)], axis=0)
        lhs = jnp.concatenate([seqs(wu[:, :, :HEAD_DIM]), seqs(bat["qd"])], axis=1).astype(BF16)
        res = _bmm(lhs, s_old.astype(BF16))
        v_new = seqs(wu[:, :, HEAD_DIM:]) - res[:, :tt]
        v16 = v_new.astype(BF16)
        s_new = s_old * bat["egl"] + _bmm_tn(seqs(bat["kt"]).astype(BF16), v16)
        o = res[:, tt:].reshape(n_bat, BLK, HEAD_DIM) + _bmm(qk, v16.reshape(n_bat, BLK, HEAD_DIM))
        r_old = jnp.concatenate([r0_ref[:, h] for h in range(N_HEADS)], axis=0)
        ro = _bmm(rqk, rv16) + _bmm(seqs(bat["rqd"]).astype(BF16),
                                    r_old.astype(BF16)).reshape(n_bat, BLK, HEAD_DIM)
        cdec = jnp.broadcast_to(cdec_ref[...][:, None], (N_HEADS, sb, 1, HEAD_DIM))
        r_new = (r_old * cdec.reshape(n_seq, 1, HEAD_DIM)
                 + _bmm_tn(seqs(bat["rkt"]).astype(BF16), seqs(bat["rv"]).astype(BF16)))
        for h in range(N_HEADS):
            s_ref[:, h] = s_new[h * sb:(h + 1) * sb]
            r_ref[:, h] = r_new[h * sb:(h + 1) * sb]
        o_dn = [o[h * nblk:(h + 1) * nblk].reshape(rows, HEAD_DIM) for h in range(N_HEADS)]
        o_ret = [ro[h * nblk:(h + 1) * nblk].reshape(rows, HEAD_DIM) for h in range(N_HEADS)]

    outs = []
    for h in range(N_HEADS):
        hs = slice(h * HEAD_DIM, (h + 1) * HEAD_DIM)
        outs.append(_rmsnorm(o_dn[h], dnw_ref[...]) * _silu(z[:, hs]))
    for h in range(N_HEADS):
        hs = slice(h * HEAD_DIM, (h + 1) * HEAD_DIM)
        o = o_ret[h]
        mu = jnp.mean(o, axis=-1, keepdims=True)
        var = jnp.mean(jnp.square(o - mu), axis=-1, keepdims=True)
        outs.append((o - mu) * lax.rsqrt(var + EPS) * retw_ref[:, hs] * _silu(rg[:, hs]))
    mix = jnp.concatenate(outs, axis=1).astype(BF16)
    h_new = x + jnp.dot(mix, wout_ref[...], preferred_element_type=F32)
    h_ref[...] = h_new.reshape(sb, tt, D_MODEL)


def _ffn_kernel(cfg, h_ref, p_ref, fst_ref, fnw_ref, wup_ref, fcw_ref, fcb_ref, wdown_ref,
                pnw_ref, wgate_ref, wple_ref, finw_ref,
                y_ref, fnew_ref, e_ref):
    sb, tt, final_norm = cfg
    rows = sb * tt
    l_idx = pl.program_id(1)
    base = SUBLANES - (FFN_CONV - 1)

    @pl.when(l_idx == 0)
    def _():
        e_ref[:, base:SUBLANES, :] = fst_ref[...]

    h = h_ref[...].reshape(rows, D_MODEL)
    m = _rmsnorm(h, fnw_ref[...]).astype(BF16)
    for half in (0, D_FF):
        for c0, cw in FF_SLABS:
            cs = slice(half + c0, half + c0 + cw)
            e_ref[:, SUBLANES:, cs] = jnp.dot(m, wup_ref[:, cs],
                                              preferred_element_type=F32).reshape(sb, tt, cw)

    def conv(cs):
        acc = e_ref[:, base:base + tt, cs] * fcw_ref[0:1, cs]
        for j in range(1, FFN_CONV):
            acc = acc + e_ref[:, base + j:base + j + tt, cs] * fcw_ref[j:j + 1, cs]
        return (acc + fcb_ref[:, cs]).reshape(rows, cs.stop - cs.start)

    acc = h
    for c0, cw in FF_SLABS:
        ug = conv(slice(c0, c0 + cw))
        uv = conv(slice(D_FF + c0, D_FF + c0 + cw))
        act = (_silu(ug) * uv).astype(BF16)
        acc = acc + jnp.dot(act, wdown_ref[c0:c0 + cw, :], preferred_element_type=F32)
    tail = e_ref[:, tt + base:tt + SUBLANES, :]
    fnew_ref[...] = tail
    e_ref[:, base:SUBLANES, :] = tail

    h = acc
    gate = jax.nn.sigmoid(jnp.dot(_rmsnorm(h, pnw_ref[...]).astype(BF16), wgate_ref[...],
                                  preferred_element_type=F32))
    pp = jnp.dot(p_ref[...].reshape(rows, PLE_DIM).astype(BF16), wple_ref[...],
                 preferred_element_type=F32)
    h = h + gate * pp
    if final_norm:
        h = _rmsnorm(h, finw_ref[...])
    y_ref[...] = h.reshape(sb, tt, D_MODEL)


def _const_spec(shape):
    zeros = (0,) * len(shape)
    return pl.BlockSpec(shape, lambda b, l: zeros, pipeline_mode=pl.Buffered(1))


def _retention_tables(chunk, blk, rows):
    hh = np.arange(N_HEADS, dtype=np.float64)
    log_gamma = np.log(1.0 - 2.0 ** (-5.0 - hh))
    bpos = (np.arange(chunk, dtype=np.float64) + 1.0)[None, :] * log_gamma[:, None]
    causal = np.tril(np.ones((chunk, chunk), dtype=bool))
    dmat = np.where(causal, np.exp(np.where(causal, bpos[:, :, None] - bpos[:, None, :], 0.0)), 0.0)
    n = blk // chunk
    dblk = np.einsum('ab,hij->haibj', np.eye(n), dmat).reshape(N_HEADS, blk, blk)
    qscale = np.exp(bpos)
    kscale = np.exp(bpos[:, -1:] - bpos)
    expand = lambda t: np.repeat(np.tile(t.T, (rows // chunk, 1)), HEAD_DIM, axis=1)
    rscale = np.stack([expand(qscale), expand(kscale)])
    cdec = np.broadcast_to(np.exp(bpos[:, -1]).reshape(N_HEADS, 1, 1), (N_HEADS, 1, HEAD_DIM))
    return tuple(jnp.asarray(t, F32) for t in (dblk, rscale, cdec))


def _chunk_sum_matrix(chunk):
    r = np.arange(BLK)
    same = (r[:, None] // chunk) == (r[None, :] // chunk)
    causal = same & (r[None, :] <= r[:, None])
    return jnp.asarray(np.tile(np.concatenate([causal, same], axis=0), (1, 3)), BF16)


def _rope_tables(pos0, tt, sb, n_l):
    inv = ROPE_BASE ** (-np.arange(0, HEAD_DIM, 2, dtype=np.float64) / HEAD_DIM)
    ang = (pos0 + np.arange(n_l * tt, dtype=np.float64))[:, None] * inv[None, :]
    cos = np.repeat(np.cos(ang), 2, axis=1)
    sin = np.stack([-np.sin(ang), np.sin(ang)], axis=-1).reshape(n_l * tt, HEAD_DIM)
    tile = lambda t: jnp.asarray(
        np.tile(t.reshape(n_l, 1, tt, HEAD_DIM), (1, sb, 1, 1)).reshape(n_l, sb * tt, HEAD_DIM), F32)
    return tile(cos), tile(sin)


def _mixer(x, cstate, s0, r0, pos0, wts, sb, tt):
    n_b, seq, _ = x.shape
    chunk = CHUNK if seq % CHUNK == 0 else seq
    chained = chunk == BLK
    rows = sb * tt
    n_l = seq // tt
    assert n_b % sb == 0 and seq % tt == 0 and rows % BLK == 0 and BLK % chunk == 0
    assert (sb == 1 and tt % chunk == 0) if chained else (tt == seq == chunk)
    cos, sin = _rope_tables(pos0, tt, sb, n_l)
    dmat, rscale, cdec = _retention_tables(*((rows, rows) if chained else (chunk, BLK)), rows)
    tri = _chunk_sum_matrix(chunk)
    state_spec = pl.BlockSpec((sb, N_HEADS, HEAD_DIM, HEAD_DIM), lambda b, l: (b, 0, 0, 0))
    x_spec = pl.BlockSpec((sb, tt, D_MODEL), lambda b, l: (b, l, 0))
    conv_spec = pl.BlockSpec((sb, DN_CONV - 1, 3 * QK), lambda b, l: (b, 0, 0))
    rope_spec = pl.BlockSpec((1, rows, HEAD_DIM), lambda b, l: (l, 0, 0))
    consts = (tri, dmat, rscale, cdec) + tuple(wts)
    return pl.pallas_call(
        functools.partial(_mixer_kernel, (sb, tt, chunk, chained)),
        grid=(n_b // sb, n_l),
        in_specs=[x_spec, conv_spec, state_spec, state_spec, rope_spec, rope_spec]
        + [_const_spec(c.shape) for c in consts],
        out_specs=[x_spec, conv_spec, state_spec, state_spec],
        out_shape=[jax.ShapeDtypeStruct(x.shape, F32),
                   jax.ShapeDtypeStruct(cstate.shape, F32),
                   jax.ShapeDtypeStruct(s0.shape, F32),
                   jax.ShapeDtypeStruct(r0.shape, F32)],
        scratch_shapes=[pltpu.VMEM((sb, SUBLANES + tt, 3 * QK), F32)],
        compiler_params=pltpu.CompilerParams(
            dimension_semantics=("arbitrary", "arbitrary"), vmem_limit_bytes=VMEM_LIMIT),
        name="mixer",
    )(x, cstate, s0, r0, cos, sin, *consts)


def _ffn(h, p, fstate, wts, final_norm, sb, tt):
    n_b, seq, _ = h.shape
    n_l = seq // tt
    assert n_b % sb == 0 and seq % tt == 0
    x_spec = pl.BlockSpec((sb, tt, D_MODEL), lambda b, l: (b, l, 0))
    p_spec = pl.BlockSpec((sb, tt, PLE_DIM), lambda b, l: (b, l, 0))
    st_spec = pl.BlockSpec((sb, FFN_CONV - 1, 2 * D_FF), lambda b, l: (b, 0, 0))
    return pl.pallas_call(
        functools.partial(_ffn_kernel, (sb, tt, final_norm)),
        grid=(n_b // sb, n_l),
        in_specs=[x_spec, p_spec, st_spec] + [_const_spec(w.shape) for w in wts],
        out_specs=[x_spec, st_spec],
        out_shape=[jax.ShapeDtypeStruct(h.shape, F32), jax.ShapeDtypeStruct(fstate.shape, F32)],
        scratch_shapes=[pltpu.VMEM((sb, SUBLANES + tt, 2 * D_FF), F32)],
        compiler_params=pltpu.CompilerParams(
            dimension_semantics=("arbitrary", "arbitrary"), vmem_limit_bytes=VMEM_LIMIT),
        name="ffn",
    )(h, p, fstate, *wts)


def kernel(x_prompt, x_sample, p_prompt, p_sample, state_dn_conv, state_dn, state_ret, state_ffn_conv, attn_norm_w, w_in, dn_conv_w, dn_A_log, dn_dt_bias, dn_norm_w, ret_norm_w, w_out, ffn_norm_w, w_up, ffn_conv_w, ffn_conv_b, w_down, ple_norm_w, w_ple_gate, w_ple, final_norm_w):
    depth = w_in.shape[0]
    n_bp = x_prompt.shape[0]
    dec_len = x_sample.shape[1]
    row = lambda v: v.reshape(1, -1).astype(F32)
    hp, hs = x_prompt, x_sample
    outs_p = ([], [], [], [])
    outs_s = ([], [], [], [])
    for i in range(depth):
        wi = w_in[i].astype(BF16)
        wmain = jnp.concatenate([wi[:, :4 * QK], wi[:, 4 * QK + 2 * N_HEADS:]], axis=1)
        wba = jnp.pad(wi[:, 4 * QK:4 * QK + 2 * N_HEADS], ((0, 0), (0, LANES - 2 * N_HEADS)))
        gpar = jnp.zeros((2, LANES), F32)
        gpar = gpar.at[0, GATE_LANE0:GATE_LANE0 + N_HEADS].set(dn_A_log[i])
        gpar = gpar.at[1, GATE_LANE0:GATE_LANE0 + N_HEADS].set(dn_dt_bias[i])
        mixer_w = (row(attn_norm_w[i]), wmain, wba, dn_conv_w[i], gpar, row(dn_norm_w[i]),
                   row(ret_norm_w[i]), w_out[i].astype(BF16))
        ffn_w = (row(ffn_norm_w[i]), w_up[i].astype(BF16), ffn_conv_w[i], row(ffn_conv_b[i]),
                 w_down[i].astype(BF16), row(ple_norm_w[i]), w_ple_gate[i].astype(BF16),
                 w_ple[i].astype(BF16), row(final_norm_w))
        last = i == depth - 1
        zc = jnp.zeros((n_bp, DN_CONV - 1, 3 * QK), F32)
        zs = jnp.zeros((n_bp, N_HEADS, HEAD_DIM, HEAD_DIM), F32)
        zf = jnp.zeros((n_bp, FFN_CONV - 1, 2 * D_FF), F32)
        hp, c1, s1, r1 = _mixer(hp, zc, zs, zs, 0, mixer_w, *PROMPT_MIXER_TILE)
        hp, f1 = _ffn(hp, p_prompt[i], zf, ffn_w, last, *PROMPT_FFN_TILE)
        hs, c2, s2, r2 = _mixer(hs, state_dn_conv[i], state_dn[i], state_ret[i], PAST_LEN,
                                mixer_w, SAMPLE_MIXER_SEQS, dec_len)
        hs, f2 = _ffn(hs, p_sample[i], state_ffn_conv[i], ffn_w, last, SAMPLE_FFN_SEQS, dec_len)
        for lst, val in zip(outs_p, (c1, s1, r1, f1)):
            lst.append(val)
        for lst, val in zip(outs_s, (c2, s2, r2, f2)):
            lst.append(val)
    return (hp, hs,
            jnp.stack(outs_p[0]), jnp.stack(outs_p[1]), jnp.stack(outs_p[2]), jnp.stack(outs_p[3]),
            jnp.stack(outs_s[0]), jnp.stack(outs_s[1]), jnp.stack(outs_s[2]), jnp.stack(outs_s[3]))
```

```python
import functools
import math

import jax
import jax.numpy as jnp
import numpy as np
from jax import lax
from jax.experimental import pallas as pl
from jax.experimental.pallas import tpu as pltpu

F32 = jnp.float32
BF16 = jnp.bfloat16

D_MODEL = 1024
HEAD_DIM = 128
N_HEADS = 4
QK = N_HEADS * HEAD_DIM
DN_CONV = 4
FFN_CONV = 3
D_FF = 2816
PLE_DIM = 256
CHUNK = 64
PAST_LEN = 16384
ROPE_BASE = 10000.0
EPS = 1e-6
LANES = 128
SUBLANES = 8
MXU_DIM = 256
BLK = 64
GATE_LANE0 = N_HEADS
FF_SLABS = ((0, 1280), (1280, 1536))
VMEM_LIMIT = 56 * 1024 * 1024
PROMPT_MIXER_TILE = (1, 256)
PROMPT_FFN_TILE = (1, 512)
SAMPLE_MIXER_SEQS = 16
SAMPLE_FFN_SEQS = 32


def _split2(x):
    hi = x.astype(BF16)
    lo = (x - hi.astype(F32)).astype(BF16)
    return hi, lo


def _rmsnorm(x, w):
    return x * lax.rsqrt(jnp.mean(x * x, axis=-1, keepdims=True) + EPS) * w


def _silu(x):
    return x * jax.nn.sigmoid(x)


def _softplus(x):
    return jnp.maximum(x, 0.0) + jnp.log(1.0 + jnp.exp(-jnp.abs(x)))


def _lane_bcast(x, lane, width=LANES):
    return jnp.broadcast_to(x[:, lane:lane + 1], (x.shape[0], width))


def _bmm(a, b):
    return lax.dot_general(a, b, (((2,), (1,)), ((0,), (0,))), preferred_element_type=F32)


def _bmm_nt(a, b):
    return lax.dot_general(a, b, (((2,), (2,)), ((0,), (0,))), preferred_element_type=F32)


def _bmm_tn(a, b):
    return lax.dot_general(a, b, (((1,), (1,)), ((0,), (0,))), preferred_element_type=F32)


def _pack_lhs(m):
    hi = m.astype(BF16).astype(F32)
    return jnp.concatenate([hi, m - hi, hi], axis=-1).astype(BF16)


def _pack_rhs(m):
    hi, lo = _split2(m)
    return jnp.concatenate([hi, hi, lo], axis=-2)


def _unit_lower_inverse_packed(a, eye, n_steps):
    c = a.shape[-1]
    upper = lax.broadcasted_iota(jnp.int32, a.shape[:-1] + (2 * c,), a.ndim - 1) >= c
    a2 = _bmm(_pack_lhs(a), _pack_rhs(a))
    mp = jnp.concatenate([a2, eye - a], axis=-1)
    for _ in range(n_steps):
        res = _bmm(_pack_lhs(mp[..., :c]), _pack_rhs(mp))
        mp = res + jnp.where(upper, mp, 0.0)
    return mp[..., c:]


def _mixer_kernel(cfg, x_ref, cst_ref, s0_ref, r0_ref, cos_ref, sin_ref, tri_ref, dmat_ref,
                  rscale_ref, cdec_ref, anw_ref, wdn_ref, wret_ref, wba_ref, convw_ref, gpar_ref,
                  dnw_ref, retw_ref, wout_ref, h_ref, cnew_ref, s_ref, r_ref, e_ref):
    sb, tt, chunk, chained = cfg
    rows = sb * tt
    nblk = rows // BLK
    n_bat = N_HEADS * nblk
    shift = int(math.log2(chunk))
    n_steps = shift - 1
    base = SUBLANES - (DN_CONV - 1)
    l_idx = pl.program_id(1)

    @pl.when(l_idx == 0)
    def _():
        e_ref[:, base:SUBLANES, :] = cst_ref[...]
        if chained:
            s_ref[...] = s0_ref[...]
            r_ref[...] = r0_ref[...]

    x = x_ref[...].reshape(rows, D_MODEL)
    a = _rmsnorm(x, anw_ref[...]).astype(BF16)
    e_ref[:, SUBLANES:, :] = jnp.dot(a, wdn_ref[:, 0:3 * QK],
                                     preferred_element_type=F32).reshape(sb, tt, 3 * QK)
    slabs = [wdn_ref.at[:, 3 * QK:4 * QK]] + [wret_ref.at[:, i * QK:(i + 1) * QK] for i in range(4)]
    proj = lambda i: jnp.dot(a, slabs[i][...], preferred_element_type=F32)
    pba = jnp.dot(a, wba_ref[...], preferred_element_type=F32)

    conv = e_ref[:, base:base + tt, :] * convw_ref[0:1, :]
    for j in range(1, DN_CONV):
        conv = conv + e_ref[:, base + j:base + j + tt, :] * convw_ref[j:j + 1, :]
    tail = e_ref[:, tt + base:tt + SUBLANES, :]
    cnew_ref[...] = tail
    e_ref[:, base:SUBLANES, :] = tail
    rq = proj(1)
    qkv = _silu(conv).reshape(rows, 3 * QK)
    rk = proj(2)

    beta_all = jax.nn.sigmoid(pba)
    g_all = -jnp.exp(gpar_ref[0:1, :]) * _softplus(pba + gpar_ref[1:2, :])
    g_hi = g_all.astype(BF16)
    g_r1 = g_all - g_hi.astype(F32)
    g_mid = g_r1.astype(BF16)
    g_lo = (g_r1 - g_mid.astype(F32)).astype(BF16)
    g_split = jnp.concatenate([t.reshape(nblk, BLK, LANES) for t in (g_hi, g_mid, g_lo)], axis=1)
    sums = _bmm(jnp.broadcast_to(tri_ref[...][None], (nblk, 2 * BLK, 3 * BLK)), g_split)
    gc_all = sums[:, :BLK].reshape(rows, LANES)
    gl_all = sums[:, BLK:].reshape(rows, LANES)
    gct_all = gc_all.T

    ri = lax.broadcasted_iota(jnp.int32, (BLK, BLK), 0)
    ci = lax.broadcasted_iota(jnp.int32, (BLK, BLK), 1)
    same = (ri >> shift) == (ci >> shift)
    causal = same & (ci <= ri)
    strict = same & (ci < ri)
    eye = (ri == ci).astype(F32)

    cos = cos_ref[0]
    sin = sin_ref[0]
    even = (lax.broadcasted_iota(jnp.int32, (rows, LANES), 1) & 1) == 0

    def rotary(t):
        swapped = jnp.where(even, pltpu.roll(t, LANES - 1, 1), pltpu.roll(t, 1, 1))
        return t * cos + swapped * sin

    per = {k: [] for k in ("qb", "kb", "kbeta", "rhs", "qd", "kt", "gcc", "gct", "egl",
                           "rq", "rk", "rqd", "rkt", "rv")}
    blocks = lambda t: t.reshape(nblk, BLK, t.shape[-1])
    for h in range(N_HEADS):
        qh = qkv[:, h * HEAD_DIM:(h + 1) * HEAD_DIM]
        kh = qkv[:, QK + h * HEAD_DIM:QK + (h + 1) * HEAD_DIM]
        vh = qkv[:, 2 * QK + h * HEAD_DIM:2 * QK + (h + 1) * HEAD_DIM]
        qb = qh * lax.rsqrt(jnp.sum(qh * qh, axis=-1, keepdims=True) + EPS) * (HEAD_DIM ** -0.5)
        kb = kh * lax.rsqrt(jnp.sum(kh * kh, axis=-1, keepdims=True) + EPS)
        bcol = _lane_bcast(beta_all, h)
        gcc = _lane_bcast(gc_all, GATE_LANE0 + h)
        glc = _lane_bcast(gl_all, GATE_LANE0 + h)
        egc = jnp.exp(gcc)
        kbeta = kb * bcol
        per["qb"].append(blocks(qb))
        per["kb"].append(blocks(kb))
        per["kbeta"].append(blocks(kbeta))
        per["rhs"].append(blocks(jnp.concatenate([kbeta * egc, vh * bcol], axis=1)))
        per["qd"].append(blocks(qb * egc))
        per["kt"].append(blocks(kb * jnp.exp(glc - gcc)))
        per["gcc"].append(blocks(gcc))
        per["egl"].append(jnp.exp(glc).reshape(rows // chunk, chunk, HEAD_DIM)[:, 0:1, :])
        per["gct"].extend(gct_all[GATE_LANE0 + h:GATE_LANE0 + h + 1, b * BLK:(b + 1) * BLK][None]
                          for b in range(nblk))
    bat = {k: jnp.concatenate(v, axis=0) for k, v in per.items() if v}
    bf = lambda k: bat[k].astype(BF16)
    rv = proj(3)

    delta = bat["gcc"][:, :, :BLK] - bat["gct"]
    decay = jnp.exp(jnp.where(causal, delta, -jnp.inf))
    kb16 = bf("kb")
    kq = _bmm_nt(jnp.concatenate([bat["kbeta"], bat["qb"]], axis=1).astype(BF16), kb16)
    amat = jnp.where(strict, kq[:, :BLK] * decay, 0.0)
    tinv = _unit_lower_inverse_packed(amat, eye, n_steps)
    z = proj(0)
    for h in range(N_HEADS):
        hs = slice(h * HEAD_DIM, (h + 1) * HEAD_DIM)
        rqh = rotary(rq[:, hs])
        rkh = rotary(rk[:, hs]) * (HEAD_DIM ** -0.5)
        per["rq"].append(blocks(rqh))
        per["rk"].append(blocks(rkh))
        per["rqd"].append(blocks(rqh * rscale_ref[0, :, hs]))
        per["rkt"].append(blocks(rkh * rscale_ref[1, :, hs]))
        per["rv"].append(blocks(rv[:, hs]))
    bat.update({k: jnp.concatenate(per[k], axis=0) for k in ("rq", "rk", "rqd", "rkt", "rv")})
    rg = proj(4)
    wu = _bmm(_pack_lhs(tinv), _pack_rhs(bat["rhs"]))
    qk = (kq[:, BLK:] * decay).astype(BF16)
    rv16 = bf("rv")
    dmat = jnp.broadcast_to(dmat_ref[...][:, None], (N_HEADS, nblk, BLK, BLK)).reshape(n_bat, BLK, BLK)
    rqk = (_bmm_nt(bf("rq"), bf("rk")) * dmat).astype(BF16)

    if chained:
        by_head = lambda t: t.reshape((N_HEADS, nblk) + t.shape[1:])
        transposed16 = lambda k: jnp.swapaxes(bat[k], 1, 2).astype(BF16)
        qg_wu = _bmm(jnp.concatenate([qk, transposed16("kt")], axis=1), wu.astype(BF16))
        qo = qg_wu[:, :BLK]
        gh = by_head(qg_wu[:, BLK:])
        qp = by_head((bat["qd"] - qo[:, :, :HEAD_DIM]).astype(BF16))
        op = by_head(qo[:, :, HEAD_DIM:])
        egl = by_head(bat["egl"])
        ro_rh = _bmm(jnp.concatenate([rqk, transposed16("rkt")], axis=1), rv16)
        ro = by_head(ro_rh[:, :BLK])
        rh = by_head(ro_rh[:, BLK:])
        rqd = by_head(bf("rqd"))
        s_cur = s_ref[0]
        r_cur = r_ref[0]
        o_dn = []
        o_ret = []
        for c in range(nblk):
            s16 = s_cur.astype(BF16)
            qg = jnp.concatenate([qp[:, c], gh[:, c, :, :HEAD_DIM].astype(BF16)], axis=1)
            qgs = _bmm(qg, s16)
            o_dn.append(op[:, c] + qgs[:, :BLK])
            s_cur = s_cur * egl[:, c] - qgs[:, BLK:] + gh[:, c, :, HEAD_DIM:]
            o_ret.append(ro[:, c] + _bmm(rqd[:, c], r_cur.astype(BF16)))
            r_cur = r_cur * cdec_ref[...] + rh[:, c]
        s_ref[0] = s_cur
        r_ref[0] = r_cur
        head_rows = lambda parts, h: jnp.concatenate([p[h] for p in parts], axis=0)
        o_dn = [head_rows(o_dn, h) for h in range(N_HEADS)]
        o_ret = [head_rows(o_ret, h) for h in range(N_HEADS)]
    else:
        n_seq = N_HEADS * sb
        seqs = lambda t: t.reshape(n_seq, tt, t.shape[-1])
        s_old = jnp.concatenate([s0_ref[:, h] for h in range(N_HEADS)], axis=0)
        lhs = jnp.concatenate([seqs(wu[:, :, :HEAD_DIM]), seqs(bat["qd"])], axis=1).astype(BF16)
        res = _bmm(lhs, s_old.astype(BF16))
        v_new = seqs(wu[:, :, HEAD_DIM:]) - res[:, :tt]
        v16 = v_new.astype(BF16)
        s_new = s_old * bat["egl"] + _bmm_tn(seqs(bat["kt"]).astype(BF16), v16)
        o = res[:, tt:].reshape(n_bat, BLK, HEAD_DIM) + _bmm(qk, v16.reshape(n_bat, BLK, HEAD_DIM))
        r_old = jnp.concatenate([r0_ref[:, h] for h in range(N_HEADS)], axis=0)
        ro = _bmm(rqk, rv16) + _bmm(seqs(bat["rqd"]).astype(BF16),
                                    r_old.astype(BF16)).reshape(n_bat, BLK, HEAD_DIM)
        cdec = jnp.broadcast_to(cdec_ref[...][:, None], (N_HEADS, sb, 1, HEAD_DIM))
        r_new = (r_old * cdec.reshape(n_seq, 1, HEAD_DIM)
                 + _bmm_tn(seqs(bat["rkt"]).astype(BF16), seqs(bat["rv"]).astype(BF16)))
        for h in range(N_HEADS):
            s_ref[:, h] = s_new[h * sb:(h + 1) * sb]
            r_ref[:, h] = r_new[h * sb:(h + 1) * sb]
        o_dn = [o[h * nblk:(h + 1) * nblk].reshape(rows, HEAD_DIM) for h in range(N_HEADS)]
        o_ret = [ro[h * nblk:(h + 1) * nblk].reshape(rows, HEAD_DIM) for h in range(N_HEADS)]

    outs = []
    for h in range(N_HEADS):
        hs = slice(h * HEAD_DIM, (h + 1) * HEAD_DIM)
        outs.append(_rmsnorm(o_dn[h], dnw_ref[...]) * _silu(z[:, hs]))
    for h in range(N_HEADS):
        hs = slice(h * HEAD_DIM, (h + 1) * HEAD_DIM)
        o = o_ret[h]
        mu = jnp.mean(o, axis=-1, keepdims=True)
        var = jnp.mean(jnp.square(o - mu), axis=-1, keepdims=True)
        outs.append((o - mu) * lax.rsqrt(var + EPS) * retw_ref[:, hs] * _silu(rg[:, hs]))
    mix = jnp.concatenate(outs, axis=1).astype(BF16)
    h_new = x + jnp.dot(mix, wout_ref[...], preferred_element_type=F32)
    h_ref[...] = h_new.reshape(sb, tt, D_MODEL)


def _ffn_kernel(cfg, h_ref, p_ref, fst_ref, fnw_ref, wup_ref, fcw_ref, fcb_ref, wdown_ref,
                pnw_ref, wgate_ref, wple_ref, finw_ref,
                y_ref, fnew_ref, e_ref):
    sb, tt, final_norm = cfg
    rows = sb * tt
    l_idx = pl.program_id(1)
    base = SUBLANES - (FFN_CONV - 1)

    @pl.when(l_idx == 0)
    def _():
        e_ref[:, base:SUBLANES, :] = fst_ref[...]

    h = h_ref[...].reshape(rows, D_MODEL)
    m = _rmsnorm(h, fnw_ref[...]).astype(BF16)
    for half in (0, D_FF):
        for c0, cw in FF_SLABS:
            cs = slice(half + c0, half + c0 + cw)
            e_ref[:, SUBLANES:, cs] = jnp.dot(m, wup_ref[:, cs],
                                              preferred_element_type=F32).reshape(sb, tt, cw)

    def conv(cs):
        acc = e_ref[:, base:base + tt, cs] * fcw_ref[0:1, cs]
        for j in range(1, FFN_CONV):
            acc = acc + e_ref[:, base + j:base + j + tt, cs] * fcw_ref[j:j + 1, cs]
        return (acc + fcb_ref[:, cs]).reshape(rows, cs.stop - cs.start)

    acc = h
    for c0, cw in FF_SLABS:
        ug = conv(slice(c0, c0 + cw))
        uv = conv(slice(D_FF + c0, D_FF + c0 + cw))
        act = (_silu(ug) * uv).astype(BF16)
        acc = acc + jnp.dot(act, wdown_ref[c0:c0 + cw, :], preferred_element_type=F32)
    tail = e_ref[:, tt + base:tt + SUBLANES, :]
    fnew_ref[...] = tail
    e_ref[:, base:SUBLANES, :] = tail

    h = acc
    gate = jax.nn.sigmoid(jnp.dot(_rmsnorm(h, pnw_ref[...]).astype(BF16), wgate_ref[...],
                                  preferred_element_type=F32))
    pp = jnp.dot(p_ref[...].reshape(rows, PLE_DIM).astype(BF16), wple_ref[...],
                 preferred_element_type=F32)
    h = h + gate * pp
    if final_norm:
        h = _rmsnorm(h, finw_ref[...])
    y_ref[...] = h.reshape(sb, tt, D_MODEL)


def _const_spec(shape):
    zeros = (0,) * len(shape)
    return pl.BlockSpec(shape, lambda b, l: zeros, pipeline_mode=pl.Buffered(1))


def _retention_tables(chunk, rows):
    hh = np.arange(N_HEADS, dtype=np.float64)
    log_gamma = np.log(1.0 - 2.0 ** (-5.0 - hh))
    bpos = (np.arange(chunk, dtype=np.float64) + 1.0)[None, :] * log_gamma[:, None]
    causal = np.tril(np.ones((chunk, chunk), dtype=bool))
    dmat = np.where(causal, np.exp(np.where(causal, bpos[:, :, None] - bpos[:, None, :], 0.0)), 0.0)
    n = BLK // chunk
    dblk = np.einsum('ab,hij->haibj', np.eye(n), dmat).reshape(N_HEADS, BLK, BLK)
    qscale = np.exp(bpos)
    kscale = np.exp(bpos[:, -1:] - bpos)
    expand = lambda t: np.repeat(np.tile(t.T, (rows // chunk, 1)), HEAD_DIM, axis=1)
    rscale = np.stack([expand(qscale), expand(kscale)])
    cdec = np.broadcast_to(np.exp(bpos[:, -1]).reshape(N_HEADS, 1, 1), (N_HEADS, 1, HEAD_DIM))
    return tuple(jnp.asarray(t, F32) for t in (dblk, rscale, cdec))


def _chunk_sum_matrix(chunk):
    r = np.arange(BLK)
    same = (r[:, None] // chunk) == (r[None, :] // chunk)
    causal = same & (r[None, :] <= r[:, None])
    return jnp.asarray(np.tile(np.concatenate([causal, same], axis=0), (1, 3)), BF16)


def _rope_tables(pos0, tt, sb, n_l):
    inv = ROPE_BASE ** (-np.arange(0, HEAD_DIM, 2, dtype=np.float64) / HEAD_DIM)
    ang = (pos0 + np.arange(n_l * tt, dtype=np.float64))[:, None] * inv[None, :]
    cos = np.repeat(np.cos(ang), 2, axis=1)
    sin = np.stack([-np.sin(ang), np.sin(ang)], axis=-1).reshape(n_l * tt, HEAD_DIM)
    tile = lambda t: jnp.asarray(
        np.tile(t.reshape(n_l, 1, tt, HEAD_DIM), (1, sb, 1, 1)).reshape(n_l, sb * tt, HEAD_DIM), F32)
    return tile(cos), tile(sin)


def _mixer(x, cstate, s0, r0, pos0, wts, sb, tt):
    n_b, seq, _ = x.shape
    chunk = CHUNK if seq % CHUNK == 0 else seq
    chained = chunk == BLK
    rows = sb * tt
    n_l = seq // tt
    assert n_b % sb == 0 and seq % tt == 0 and rows % BLK == 0 and BLK % chunk == 0
    assert (sb == 1 and tt % chunk == 0) if chained else (tt == seq == chunk)
    cos, sin = _rope_tables(pos0, tt, sb, n_l)
    dmat, rscale, cdec = _retention_tables(chunk, rows)
    tri = _chunk_sum_matrix(chunk)
    state_spec = pl.BlockSpec((sb, N_HEADS, HEAD_DIM, HEAD_DIM), lambda b, l: (b, 0, 0, 0))
    x_spec = pl.BlockSpec((sb, tt, D_MODEL), lambda b, l: (b, l, 0))
    conv_spec = pl.BlockSpec((sb, DN_CONV - 1, 3 * QK), lambda b, l: (b, 0, 0))
    rope_spec = pl.BlockSpec((1, rows, HEAD_DIM), lambda b, l: (l, 0, 0))
    consts = (tri, dmat, rscale, cdec) + tuple(wts)
    return pl.pallas_call(
        functools.partial(_mixer_kernel, (sb, tt, chunk, chained)),
        grid=(n_b // sb, n_l),
        in_specs=[x_spec, conv_spec, state_spec, state_spec, rope_spec, rope_spec]
        + [_const_spec(c.shape) for c in consts],
        out_specs=[x_spec, conv_spec, state_spec, state_spec],
        out_shape=[jax.ShapeDtypeStruct(x.shape, F32),
                   jax.ShapeDtypeStruct(cstate.shape, F32),
                   jax.ShapeDtypeStruct(s0.shape, F32),
                   jax.ShapeDtypeStruct(r0.shape, F32)],
        scratch_shapes=[pltpu.VMEM((sb, SUBLANES + tt, 3 * QK), F32)],
        compiler_params=pltpu.CompilerParams(
            dimension_semantics=("arbitrary", "arbitrary"), vmem_limit_bytes=VMEM_LIMIT),
        name="mixer",
    )(x, cstate, s0, r0, cos, sin, *consts)


def _ffn(h, p, fstate, wts, final_norm, sb, tt):
    n_b, seq, _ = h.shape
    n_l = seq // tt
    assert n_b % sb == 0 and seq % tt == 0
    x_spec = pl.BlockSpec((sb, tt, D_MODEL), lambda b, l: (b, l, 0))
    p_spec = pl.BlockSpec((sb, tt, PLE_DIM), lambda b, l: (b, l, 0))
    st_spec = pl.BlockSpec((sb, FFN_CONV - 1, 2 * D_FF), lambda b, l: (b, 0, 0))
    return pl.pallas_call(
        functools.partial(_ffn_kernel, (sb, tt, final_norm)),
        grid=(n_b // sb, n_l),
        in_specs=[x_spec, p_spec, st_spec] + [_const_spec(w.shape) for w in wts],
        out_specs=[x_spec, st_spec],
        out_shape=[jax.ShapeDtypeStruct(h.shape, F32), jax.ShapeDtypeStruct(fstate.shape, F32)],
        scratch_shapes=[pltpu.VMEM((sb, SUBLANES + tt, 2 * D_FF), F32)],
        compiler_params=pltpu.CompilerParams(
            dimension_semantics=("arbitrary", "arbitrary"), vmem_limit_bytes=VMEM_LIMIT),
        name="ffn",
    )(h, p, fstate, *wts)


def kernel(x_prompt, x_sample, p_prompt, p_sample, state_dn_conv, state_dn, state_ret, state_ffn_conv, attn_norm_w, w_in, dn_conv_w, dn_A_log, dn_dt_bias, dn_norm_w, ret_norm_w, w_out, ffn_norm_w, w_up, ffn_conv_w, ffn_conv_b, w_down, ple_norm_w, w_ple_gate, w_ple, final_norm_w):
    depth = w_in.shape[0]
    n_bp = x_prompt.shape[0]
    dec_len = x_sample.shape[1]
    row = lambda v: v.reshape(1, -1).astype(F32)
    hp, hs = x_prompt, x_sample
    outs_p = ([], [], [], [])
    outs_s = ([], [], [], [])
    for i in range(depth):
        wi = w_in[i].astype(BF16)
        wdn, wret = wi[:, :4 * QK], wi[:, 4 * QK + 2 * N_HEADS:]
        wba = jnp.pad(wi[:, 4 * QK:4 * QK + 2 * N_HEADS], ((0, 0), (0, LANES - 2 * N_HEADS)))
        gpar = jnp.zeros((2, LANES), F32)
        gpar = gpar.at[0, GATE_LANE0:GATE_LANE0 + N_HEADS].set(dn_A_log[i])
        gpar = gpar.at[1, GATE_LANE0:GATE_LANE0 + N_HEADS].set(dn_dt_bias[i])
        mixer_w = (row(attn_norm_w[i]), wdn, wret, wba, dn_conv_w[i], gpar, row(dn_norm_w[i]),
                   row(ret_norm_w[i]), w_out[i].astype(BF16))
        ffn_w = (row(ffn_norm_w[i]), w_up[i].astype(BF16), ffn_conv_w[i], row(ffn_conv_b[i]),
                 w_down[i].astype(BF16), row(ple_norm_w[i]), w_ple_gate[i].astype(BF16),
                 w_ple[i].astype(BF16), row(final_norm_w))
        last = i == depth - 1
        zc = jnp.zeros((n_bp, DN_CONV - 1, 3 * QK), F32)
        zs = jnp.zeros((n_bp, N_HEADS, HEAD_DIM, HEAD_DIM), F32)
        zf = jnp.zeros((n_bp, FFN_CONV - 1, 2 * D_FF), F32)
        hp, c1, s1, r1 = _mixer(hp, zc, zs, zs, 0, mixer_w, *PROMPT_MIXER_TILE)
        hp, f1 = _ffn(hp, p_prompt[i], zf, ffn_w, last, *PROMPT_FFN_TILE)
        hs, c2, s2, r2 = _mixer(hs, state_dn_conv[i], state_dn[i], state_ret[i], PAST_LEN,
                                mixer_w, SAMPLE_MIXER_SEQS, dec_len)
        hs, f2 = _ffn(hs, p_sample[i], state_ffn_conv[i], ffn_w, last, SAMPLE_FFN_SEQS, dec_len)
        for lst, val in zip(outs_p, (c1, s1, r1, f1)):
            lst.append(val)
        for lst, val in zip(outs_s, (c2, s2, r2, f2)):
            lst.append(val)
    return (hp, hs,
            jnp.stack(outs_p[0]), jnp.stack(outs_p[1]), jnp.stack(outs_p[2]), jnp.stack(outs_p[3]),
            jnp.stack(outs_s[0]), jnp.stack(outs_s[1]), jnp.stack(outs_s[2]), jnp.stack(outs_s[3]))
```

```python
import functools
import math

import jax
import jax.numpy as jnp
import numpy as np
from jax import lax
from jax.experimental import pallas as pl
from jax.experimental.pallas import tpu as pltpu

F32 = jnp.float32
BF16 = jnp.bfloat16

D_MODEL = 1024
HEAD_DIM = 128
N_HEADS = 4
QK = N_HEADS * HEAD_DIM
DN_CONV = 4
FFN_CONV = 3
D_FF = 2816
PLE_DIM = 256
CHUNK = 64
PAST_LEN = 16384
ROPE_BASE = 10000.0
EPS = 1e-6
LANES = 128
SUBLANES = 8
MXU_DIM = 256
BLK = 64
GATE_LANE0 = N_HEADS
FF_SLABS = ((0, 1280), (1280, 1536))
VMEM_LIMIT = 56 * 1024 * 1024
PROMPT_MIXER_TILE = (1, 256)
PROMPT_FFN_TILE = (1, 512)
SAMPLE_MIXER_SEQS = 16
SAMPLE_FFN_SEQS = 32


def _split2(x):
    hi = x.astype(BF16)
    lo = (x - hi.astype(F32)).astype(BF16)
    return hi, lo


def _rmsnorm(x, w):
    return x * lax.rsqrt(jnp.mean(x * x, axis=-1, keepdims=True) + EPS) * w


def _silu(x):
    return x * jax.nn.sigmoid(x)


def _softplus(x):
    return jnp.maximum(x, 0.0) + jnp.log(1.0 + jnp.exp(-jnp.abs(x)))


def _lane_bcast(x, lane, width=LANES):
    return jnp.broadcast_to(x[:, lane:lane + 1], (x.shape[0], width))


def _bmm(a, b):
    return lax.dot_general(a, b, (((2,), (1,)), ((0,), (0,))), preferred_element_type=F32)


def _bmm_nt(a, b):
    return lax.dot_general(a, b, (((2,), (2,)), ((0,), (0,))), preferred_element_type=F32)


def _bmm_tn(a, b):
    return lax.dot_general(a, b, (((1,), (1,)), ((0,), (0,))), preferred_element_type=F32)


def _pack_lhs(m):
    hi = m.astype(BF16).astype(F32)
    return jnp.concatenate([hi, m - hi, hi], axis=-1).astype(BF16)


def _pack_rhs(m):
    hi, lo = _split2(m)
    return jnp.concatenate([hi, hi, lo], axis=-2)


def _unit_lower_inverse_packed(a, eye, n_steps):
    c = a.shape[-1]
    upper = lax.broadcasted_iota(jnp.int32, a.shape[:-1] + (2 * c,), a.ndim - 1) >= c
    a2 = _bmm(_pack_lhs(a), _pack_rhs(a))
    mp = jnp.concatenate([a2, eye - a], axis=-1)
    for _ in range(n_steps):
        res = _bmm(_pack_lhs(mp[..., :c]), _pack_rhs(mp))
        mp = res + jnp.where(upper, mp, 0.0)
    return mp[..., c:]


def _mixer_kernel(cfg, x_ref, cst_ref, s0_ref, r0_ref, cos_ref, sin_ref, tri_ref, dmat_ref,
                  rscale_ref, cdec_ref, anw_ref, wdn_ref, wret_ref, wba_ref, convw_ref, gpar_ref,
                  dnw_ref, retw_ref, wout_ref, h_ref, cnew_ref, s_ref, r_ref, e_ref):
    sb, tt, chunk, chained = cfg
    rows = sb * tt
    nblk = rows // BLK
    n_bat = N_HEADS * nblk
    shift = int(math.log2(chunk))
    n_steps = shift - 1
    base = SUBLANES - (DN_CONV - 1)
    l_idx = pl.program_id(1)

    @pl.when(l_idx == 0)
    def _():
        e_ref[:, base:SUBLANES, :] = cst_ref[...]
        if chained:
            s_ref[...] = s0_ref[...]
            r_ref[...] = r0_ref[...]

    x = x_ref[...].reshape(rows, D_MODEL)
    a = _rmsnorm(x, anw_ref[...]).astype(BF16)
    e_ref[:, SUBLANES:, :] = jnp.dot(a, wdn_ref[:, 0:3 * QK],
                                     preferred_element_type=F32).reshape(sb, tt, 3 * QK)
    slabs = [wdn_ref.at[:, 3 * QK:4 * QK]] + [wret_ref.at[:, i * QK:(i + 1) * QK] for i in range(4)]
    proj = lambda i: jnp.dot(a, slabs[i][...], preferred_element_type=F32)
    pba = jnp.dot(a, wba_ref[...], preferred_element_type=F32)

    conv = e_ref[:, base:base + tt, :] * convw_ref[0:1, :]
    for j in range(1, DN_CONV):
        conv = conv + e_ref[:, base + j:base + j + tt, :] * convw_ref[j:j + 1, :]
    tail = e_ref[:, tt + base:tt + SUBLANES, :]
    cnew_ref[...] = tail
    e_ref[:, base:SUBLANES, :] = tail
    rq = proj(1)
    qkv = _silu(conv).reshape(rows, 3 * QK)
    rk = proj(2)

    beta_all = jax.nn.sigmoid(pba)
    g_all = -jnp.exp(gpar_ref[0:1, :]) * _softplus(pba + gpar_ref[1:2, :])
    g_hi = g_all.astype(BF16)
    g_r1 = g_all - g_hi.astype(F32)
    g_mid = g_r1.astype(BF16)
    g_lo = (g_r1 - g_mid.astype(F32)).astype(BF16)
    g_split = jnp.concatenate([t.reshape(nblk, BLK, LANES) for t in (g_hi, g_mid, g_lo)], axis=1)
    sums = _bmm(jnp.broadcast_to(tri_ref[...][None], (nblk, 2 * BLK, 3 * BLK)), g_split)
    gc_all = sums[:, :BLK].reshape(rows, LANES)
    gl_all = sums[:, BLK:].reshape(rows, LANES)
    gct_all = gc_all.T

    ri = lax.broadcasted_iota(jnp.int32, (BLK, BLK), 0)
    ci = lax.broadcasted_iota(jnp.int32, (BLK, BLK), 1)
    same = (ri >> shift) == (ci >> shift)
    causal = same & (ci <= ri)
    strict = same & (ci < ri)
    eye = (ri == ci).astype(F32)

    cos = cos_ref[0]
    sin = sin_ref[0]
    even = (lax.broadcasted_iota(jnp.int32, (rows, LANES), 1) & 1) == 0

    def rotary(t):
        swapped = jnp.where(even, pltpu.roll(t, LANES - 1, 1), pltpu.roll(t, 1, 1))
        return t * cos + swapped * sin

    per = {k: [] for k in ("qb", "kb", "kbeta", "rhs", "qd", "kt", "gcc", "gct", "egl",
                           "rq", "rk", "rqd", "rkt", "rv")}
    blocks = lambda t: t.reshape(nblk, BLK, t.shape[-1])
    for h in range(N_HEADS):
        qh = qkv[:, h * HEAD_DIM:(h + 1) * HEAD_DIM]
        kh = qkv[:, QK + h * HEAD_DIM:QK + (h + 1) * HEAD_DIM]
        vh = qkv[:, 2 * QK + h * HEAD_DIM:2 * QK + (h + 1) * HEAD_DIM]
        qb = qh * lax.rsqrt(jnp.sum(qh * qh, axis=-1, keepdims=True) + EPS) * (HEAD_DIM ** -0.5)
        kb = kh * lax.rsqrt(jnp.sum(kh * kh, axis=-1, keepdims=True) + EPS)
        bcol = _lane_bcast(beta_all, h)
        gcc = _lane_bcast(gc_all, GATE_LANE0 + h)
        glc = _lane_bcast(gl_all, GATE_LANE0 + h)
        egc = jnp.exp(gcc)
        kbeta = kb * bcol
        per["qb"].append(blocks(qb))
        per["kb"].append(blocks(kb))
        per["kbeta"].append(blocks(kbeta))
        per["rhs"].append(blocks(jnp.concatenate([kbeta * egc, vh * bcol], axis=1)))
        per["qd"].append(blocks(qb * egc))
        per["kt"].append(blocks(kb * jnp.exp(glc - gcc)))
        per["gcc"].append(blocks(gcc))
        per["egl"].append(jnp.exp(glc).reshape(rows // chunk, chunk, HEAD_DIM)[:, 0:1, :])
        per["gct"].extend(gct_all[GATE_LANE0 + h:GATE_LANE0 + h + 1, b * BLK:(b + 1) * BLK][None]
                          for b in range(nblk))
    bat = {k: jnp.concatenate(v, axis=0) for k, v in per.items() if v}
    bf = lambda k: bat[k].astype(BF16)
    rv = proj(3)

    delta = bat["gcc"][:, :, :BLK] - bat["gct"]
    decay = jnp.exp(jnp.where(causal, delta, -jnp.inf))
    kb16 = bf("kb")
    kq = _bmm_nt(jnp.concatenate([bat["kbeta"], bat["qb"]], axis=1).astype(BF16), kb16)
    amat = jnp.where(strict, kq[:, :BLK] * decay, 0.0)
    tinv = _unit_lower_inverse_packed(amat, eye, n_steps)
    z = proj(0)
    for h in range(N_HEADS):
        hs = slice(h * HEAD_DIM, (h + 1) * HEAD_DIM)
        rqh = rotary(rq[:, hs])
        rkh = rotary(rk[:, hs]) * (HEAD_DIM ** -0.5)
        per["rq"].append(blocks(rqh))
        per["rk"].append(blocks(rkh))
        per["rqd"].append(blocks(rqh * rscale_ref[0, :, hs]))
        per["rkt"].append(blocks(rkh * rscale_ref[1, :, hs]))
        per["rv"].append(blocks(rv[:, hs]))
    bat.update({k: jnp.concatenate(per[k], axis=0) for k in ("rq", "rk", "rqd", "rkt", "rv")})
    rg = proj(4)
    wu = _bmm(_pack_lhs(tinv), _pack_rhs(bat["rhs"]))
    qk = (kq[:, BLK:] * decay).astype(BF16)
    rv16 = bf("rv")
    dmat = jnp.broadcast_to(dmat_ref[...][:, None], (N_HEADS, nblk, BLK, BLK)).reshape(n_bat, BLK, BLK)
    rqk = (_bmm_nt(bf("rq"), bf("rk")) * dmat).astype(BF16)

    if chained:
        by_head = lambda t: t.reshape((N_HEADS, nblk) + t.shape[1:])
        transposed16 = lambda k: jnp.swapaxes(bat[k], 1, 2).astype(BF16)
        qg_wu = _bmm(jnp.concatenate([qk, transposed16("kt")], axis=1), wu.astype(BF16))
        qo = qg_wu[:, :BLK]
        gh = by_head(qg_wu[:, BLK:])
        qp = by_head((bat["qd"] - qo[:, :, :HEAD_DIM]).astype(BF16))
        op = by_head(qo[:, :, HEAD_DIM:])
        egl = by_head(bat["egl"])
        ro_rh = _bmm(jnp.concatenate([rqk, transposed16("rkt")], axis=1), rv16)
        ro = by_head(ro_rh[:, :BLK])
        rh = by_head(ro_rh[:, BLK:])
        rqd = by_head(bf("rqd"))
        s_cur = s_ref[0]
        r_cur = r_ref[0]
        o_dn = []
        o_ret = []
        for c in range(nblk):
            s16 = s_cur.astype(BF16)
            qg = jnp.concatenate([qp[:, c], gh[:, c, :, :HEAD_DIM].astype(BF16)], axis=1)
            qgs = _bmm(qg, s16)
            o_dn.append(op[:, c] + qgs[:, :BLK])
            s_cur = s_cur * egl[:, c] - qgs[:, BLK:] + gh[:, c, :, HEAD_DIM:]
            o_ret.append(ro[:, c] + _bmm(rqd[:, c], r_cur.astype(BF16)))
            r_cur = r_cur * cdec_ref[...] + rh[:, c]
        s_ref[0] = s_cur
        r_ref[0] = r_cur
        head_rows = lambda parts, h: jnp.concatenate([p[h] for p in parts], axis=0)
        o_dn = [head_rows(o_dn, h) for h in range(N_HEADS)]
        o_ret = [head_rows(o_ret, h) for h in range(N_HEADS)]
    else:
        n_seq = N_HEADS * sb
        seqs = lambda t: t.reshape(n_seq, tt, t.shape[-1])
        s_old = jnp.concatenate([s0_ref[:, h] for h in range(N_HEADS)], axis=0)
        lhs = jnp.concatenate([seqs(wu[:, :, :HEAD_DIM]), seqs(bat["qd"])], axis=1).astype(BF16)
        res = _bmm(lhs, s_old.astype(BF16))
        v_new = seqs(wu[:, :, HEAD_DIM:]) - res[:, :tt]
        v16 = v_new.astype(BF16)
        s_new = s_old * bat["egl"] + _bmm_tn(seqs(bat["kt"]).astype(BF16), v16)
        o = res[:, tt:].reshape(n_bat, BLK, HEAD_DIM) + _bmm(qk, v16.reshape(n_bat, BLK, HEAD_DIM))
        r_old = jnp.concatenate([r0_ref[:, h] for h in range(N_HEADS)], axis=0)
        ro = _bmm(rqk, rv16) + _bmm(seqs(bat["rqd"]).astype(BF16),
                                    r_old.astype(BF16)).reshape(n_bat, BLK, HEAD_DIM)
        cdec = jnp.broadcast_to(cdec_ref[...][:, None], (N_HEADS, sb, 1, HEAD_DIM))
        r_new = (r_old * cdec.reshape(n_seq, 1, HEAD_DIM)
                 + _bmm_tn(seqs(bat["rkt"]).astype(BF16), seqs(bat["rv"]).astype(BF16)))
        for h in range(N_HEADS):
            s_ref[:, h] = s_new[h * sb:(h + 1) * sb]
            r_ref[:, h] = r_new[h * sb:(h + 1) * sb]
        o_dn = [o[h * nblk:(h + 1) * nblk].reshape(rows, HEAD_DIM) for h in range(N_HEADS)]
        o_ret = [ro[h * nblk:(h + 1) * nblk].reshape(rows, HEAD_DIM) for h in range(N_HEADS)]

    outs = []
    for h in range(N_HEADS):
        hs = slice(h * HEAD_DIM, (h + 1) * HEAD_DIM)
        outs.append(_rmsnorm(o_dn[h], dnw_ref[...]) * _silu(z[:, hs]))
    for h in range(N_HEADS):
        hs = slice(h * HEAD_DIM, (h + 1) * HEAD_DIM)
        o = o_ret[h]
        mu = jnp.mean(o, axis=-1, keepdims=True)
        var = jnp.mean(jnp.square(o - mu), axis=-1, keepdims=True)
        outs.append((o - mu) * lax.rsqrt(var + EPS) * retw_ref[:, hs] * _silu(rg[:, hs]))
    mix = jnp.concatenate(outs, axis=1).astype(BF16)
    h_new = x + jnp.dot(mix, wout_ref[...], preferred_element_type=F32)
    h_ref[...] = h_new.reshape(sb, tt, D_MODEL)


def _ffn_kernel(cfg, h_ref, p_ref, fst_ref, fnw_ref, wup_ref, fcw_ref, fcb_ref, wdown_ref,
                pnw_ref, wgate_ref, wple_ref, finw_ref,
                y_ref, fnew_ref, e_ref):
    sb, tt, final_norm = cfg
    rows = sb * tt
    l_idx = pl.program_id(1)
    base = SUBLANES - (FFN_CONV - 1)

    @pl.when(l_idx == 0)
    def _():
        e_ref[:, base:SUBLANES, :] = fst_ref[...]

    h = h_ref[...].reshape(rows, D_MODEL)
    m = _rmsnorm(h, fnw_ref[...]).astype(BF16)
    for half in (0, D_FF):
        for c0, cw in FF_SLABS:
            cs = slice(half + c0, half + c0 + cw)
            e_ref[:, SUBLANES:, cs] = jnp.dot(m, wup_ref[:, cs],
                                              preferred_element_type=F32).reshape(sb, tt, cw)

    def conv(cs):
        acc = e_ref[:, base:base + tt, cs] * fcw_ref[0:1, cs]
        for j in range(1, FFN_CONV):
            acc = acc + e_ref[:, base + j:base + j + tt, cs] * fcw_ref[j:j + 1, cs]
        return (acc + fcb_ref[:, cs]).reshape(rows, cs.stop - cs.start)

    acc = h
    for c0, cw in FF_SLABS:
        ug = conv(slice(c0, c0 + cw))
        uv = conv(slice(D_FF + c0, D_FF + c0 + cw))
        act = (_silu(ug) * uv).astype(BF16)
        acc = acc + jnp.dot(act, wdown_ref[c0:c0 + cw, :], preferred_element_type=F32)
    tail = e_ref[:, tt + base:tt + SUBLANES, :]
    fnew_ref[...] = tail
    e_ref[:, base:SUBLANES, :] = tail

    h = acc
    gate = jax.nn.sigmoid(jnp.dot(_rmsnorm(h, pnw_ref[...]).astype(BF16), wgate_ref[...],
                                  preferred_element_type=F32))
    pp = jnp.dot(p_ref[...].reshape(rows, PLE_DIM).astype(BF16), wple_ref[...],
                 preferred_element_type=F32)
    h = h + gate * pp
    if final_norm:
        h = _rmsnorm(h, finw_ref[...])
    y_ref[...] = h.reshape(sb, tt, D_MODEL)


def _const_spec(shape):
    zeros = (0,) * len(shape)
    return pl.BlockSpec(shape, lambda b, l: zeros, pipeline_mode=pl.Buffered(1))


def _retention_tables(chunk, rows):
    hh = np.arange(N_HEADS, dtype=np.float64)
    log_gamma = np.log(1.0 - 2.0 ** (-5.0 - hh))
    bpos = (np.arange(chunk, dtype=np.float64) + 1.0)[None, :] * log_gamma[:, None]
    causal = np.tril(np.ones((chunk, chunk), dtype=bool))
    dmat = np.where(causal, np.exp(np.where(causal, bpos[:, :, None] - bpos[:, None, :], 0.0)), 0.0)
    n = BLK // chunk
    dblk = np.einsum('ab,hij->haibj', np.eye(n), dmat).reshape(N_HEADS, BLK, BLK)
    qscale = np.exp(bpos)
    kscale = np.exp(bpos[:, -1:] - bpos)
    expand = lambda t: np.repeat(np.tile(t.T, (rows // chunk, 1)), HEAD_DIM, axis=1)
    rscale = np.stack([expand(qscale), expand(kscale)])
    cdec = np.broadcast_to(np.exp(bpos[:, -1]).reshape(N_HEADS, 1, 1), (N_HEADS, 1, HEAD_DIM))
    return tuple(jnp.asarray(t, F32) for t in (dblk, rscale, cdec))


def _chunk_sum_matrix(chunk):
    r = np.arange(BLK)
    same = (r[:, None] // chunk) == (r[None, :] // chunk)
    causal = same & (r[None, :] <= r[:, None])
    return jnp.asarray(np.tile(np.concatenate([causal, same], axis=0), (1, 3)), BF16)


def _rope_tables(pos0, tt, sb, n_l):
    inv = ROPE_BASE ** (-np.arange(0, HEAD_DIM, 2, dtype=np.float64) / HEAD_DIM)
    ang = (pos0 + np.arange(n_l * tt, dtype=np.float64))[:, None] * inv[None, :]
    cos = np.repeat(np.cos(ang), 2, axis=1)
    sin = np.stack([-np.sin(ang), np.sin(ang)], axis=-1).reshape(n_l * tt, HEAD_DIM)
    tile = lambda t: jnp.asarray(
        np.tile(t.reshape(n_l, 1, tt, HEAD_DIM), (1, sb, 1, 1)).reshape(n_l, sb * tt, HEAD_DIM), F32)
    return tile(cos), tile(sin)


def _mixer(x, cstate, s0, r0, pos0, wts, sb, tt):
    n_b, seq, _ = x.shape
    chunk = CHUNK if seq % CHUNK == 0 else seq
    chained = chunk == BLK
    rows = sb * tt
    n_l = seq // tt
    assert n_b % sb == 0 and seq % tt == 0 and rows % BLK == 0 and BLK % chunk == 0
    assert (sb == 1 and tt % chunk == 0) if chained else (tt == seq == chunk)
    cos, sin = _rope_tables(pos0, tt, sb, n_l)
    dmat, rscale, cdec = _retention_tables(chunk, rows)
    tri = _chunk_sum_matrix(chunk)
    state_spec = pl.BlockSpec((sb, N_HEADS, HEAD_DIM, HEAD_DIM), lambda b, l: (b, 0, 0, 0))
    x_spec = pl.BlockSpec((sb, tt, D_MODEL), lambda b, l: (b, l, 0))
    conv_spec = pl.BlockSpec((sb, DN_CONV - 1, 3 * QK), lambda b, l: (b, 0, 0))
    rope_spec = pl.BlockSpec((1, rows, HEAD_DIM), lambda b, l: (l, 0, 0))
    consts = (tri, dmat, rscale, cdec) + tuple(wts)
    return pl.pallas_call(
        functools.partial(_mixer_kernel, (sb, tt, chunk, chained)),
        grid=(n_b // sb, n_l),
        in_specs=[x_spec, conv_spec, state_spec, state_spec, rope_spec, rope_spec]
        + [_const_spec(c.shape) for c in consts],
        out_specs=[x_spec, conv_spec, state_spec, state_spec],
        out_shape=[jax.ShapeDtypeStruct(x.shape, F32),
                   jax.ShapeDtypeStruct(cstate.shape, F32),
                   jax.ShapeDtypeStruct(s0.shape, F32),
                   jax.ShapeDtypeStruct(r0.shape, F32)],
        scratch_shapes=[pltpu.VMEM((sb, SUBLANES + tt, 3 * QK), F32)],
        compiler_params=pltpu.CompilerParams(
            dimension_semantics=("arbitrary", "arbitrary"), vmem_limit_bytes=VMEM_LIMIT),
        name="mixer",
    )(x, cstate, s0, r0, cos, sin, *consts)


def _ffn(h, p, fstate, wts, final_norm, sb, tt):
    n_b, seq, _ = h.shape
    n_l = seq // tt
    assert n_b % sb == 0 and seq % tt == 0
    x_spec = pl.BlockSpec((sb, tt, D_MODEL), lambda b, l: (b, l, 0))
    p_spec = pl.BlockSpec((sb, tt, PLE_DIM), lambda b, l: (b, l, 0))
    st_spec = pl.BlockSpec((sb, FFN_CONV - 1, 2 * D_FF), lambda b, l: (b, 0, 0))
    return pl.pallas_call(
        functools.partial(_ffn_kernel, (sb, tt, final_norm)),
        grid=(n_b // sb, n_l),
        in_specs=[x_spec, p_spec, st_spec] + [_const_spec(w.shape) for w in wts],
        out_specs=[x_spec, st_spec],
        out_shape=[jax.ShapeDtypeStruct(h.shape, F32), jax.ShapeDtypeStruct(fstate.shape, F32)],
        scratch_shapes=[pltpu.VMEM((sb, SUBLANES + tt, 2 * D_FF), F32)],
        compiler_params=pltpu.CompilerParams(
            dimension_semantics=("arbitrary", "arbitrary"), vmem_limit_bytes=VMEM_LIMIT),
        name="ffn",
    )(h, p, fstate, *wts)


def kernel(x_prompt, x_sample, p_prompt, p_sample, state_dn_conv, state_dn, state_ret, state_ffn_conv, attn_norm_w, w_in, dn_conv_w, dn_A_log, dn_dt_bias, dn_norm_w, ret_norm_w, w_out, ffn_norm_w, w_up, ffn_conv_w, ffn_conv_b, w_down, ple_norm_w, w_ple_gate, w_ple, final_norm_w):
    depth = w_in.shape[0]
    n_bp = x_prompt.shape[0]
    dec_len = x_sample.shape[1]
    row = lambda v: v.reshape(1, -1).astype(F32)
    hp, hs = x_prompt, x_sample
    outs_p = ([], [], [], [])
    outs_s = ([], [], [], [])
    for i in range(depth):
        wi = w_in[i]
        wdn = wi[:, :4 * QK].astype(BF16)
        wret = wi[:, 4 * QK + 2 * N_HEADS:].astype(BF16)
        wba = jnp.pad(wi[:, 4 * QK:4 * QK + 2 * N_HEADS].astype(BF16),
                      ((0, 0), (0, LANES - 2 * N_HEADS)))
        gpar = jnp.zeros((2, LANES), F32)
        gpar = gpar.at[0, GATE_LANE0:GATE_LANE0 + N_HEADS].set(dn_A_log[i])
        gpar = gpar.at[1, GATE_LANE0:GATE_LANE0 + N_HEADS].set(dn_dt_bias[i])
        mixer_w = (row(attn_norm_w[i]), wdn, wret, wba, dn_conv_w[i], gpar, row(dn_norm_w[i]),
                   row(ret_norm_w[i]), w_out[i].astype(BF16))
        ffn_w = (row(ffn_norm_w[i]), w_up[i].astype(BF16), ffn_conv_w[i], row(ffn_conv_b[i]),
                 w_down[i].astype(BF16), row(ple_norm_w[i]), w_ple_gate[i].astype(BF16),
                 w_ple[i].astype(BF16), row(final_norm_w))
        last = i == depth - 1
        zc = jnp.zeros((n_bp, DN_CONV - 1, 3 * QK), F32)
        zs = jnp.zeros((n_bp, N_HEADS, HEAD_DIM, HEAD_DIM), F32)
        zf = jnp.zeros((n_bp, FFN_CONV - 1, 2 * D_FF), F32)
        hp, c1, s1, r1 = _mixer(hp, zc, zs, zs, 0, mixer_w, *PROMPT_MIXER_TILE)
        hp, f1 = _ffn(hp, p_prompt[i], zf, ffn_w, last, *PROMPT_FFN_TILE)
        hs, c2, s2, r2 = _mixer(hs, state_dn_conv[i], state_dn[i], state_ret[i], PAST_LEN,
                                mixer_w, SAMPLE_MIXER_SEQS, dec_len)
        hs, f2 = _ffn(hs, p_sample[i], state_ffn_conv[i], ffn_w, last, SAMPLE_FFN_SEQS, dec_len)
        for lst, val in zip(outs_p, (c1, s1, r1, f1)):
            lst.append(val)
        for lst, val in zip(outs_s, (c2, s2, r2, f2)):
            lst.append(val)
    return (hp, hs,
            jnp.stack(outs_p[0]), jnp.stack(outs_p[1]), jnp.stack(outs_p[2]), jnp.stack(outs_p[3]),
            jnp.stack(outs_s[0]), jnp.stack(outs_s[1]), jnp.stack(outs_s[2]), jnp.stack(outs_s[3]))
```

```python
import functools
import math

import jax
import jax.numpy as jnp
import numpy as np
from jax import lax
from jax.experimental import pallas as pl
from jax.experimental.pallas import tpu as pltpu

F32 = jnp.float32
BF16 = jnp.bfloat16

D_MODEL = 1024
HEAD_DIM = 128
N_HEADS = 4
QK = N_HEADS * HEAD_DIM
DN_CONV = 4
FFN_CONV = 3
D_FF = 2816
PLE_DIM = 256
CHUNK = 64
PAST_LEN = 16384
ROPE_BASE = 10000.0
EPS = 1e-6
LANES = 128
SUBLANES = 8
MXU_DIM = 256
BLK = 64
GATE_LANE0 = N_HEADS
FF_SLABS = ((0, 1280), (1280, 1536))
VMEM_LIMIT = 56 * 1024 * 1024
PROMPT_MIXER_TILE = (1, 256)
PROMPT_FFN_TILE = (1, 512)
SAMPLE_MIXER_SEQS = 16
SAMPLE_FFN_SEQS = 32


def _split2(x):
    hi = x.astype(BF16)
    lo = (x - hi.astype(F32)).astype(BF16)
    return hi, lo


def _rmsnorm(x, w):
    return x * lax.rsqrt(jnp.mean(x * x, axis=-1, keepdims=True) + EPS) * w


def _silu(x):
    return x * jax.nn.sigmoid(x)


def _softplus(x):
    return jnp.maximum(x, 0.0) + jnp.log(1.0 + jnp.exp(-jnp.abs(x)))


def _lane_bcast(x, lane, width=LANES):
    return jnp.broadcast_to(x[:, lane:lane + 1], (x.shape[0], width))


def _bmm(a, b):
    return lax.dot_general(a, b, (((2,), (1,)), ((0,), (0,))), preferred_element_type=F32)


def _bmm_nt(a, b):
    return lax.dot_general(a, b, (((2,), (2,)), ((0,), (0,))), preferred_element_type=F32)


def _bmm_tn(a, b):
    return lax.dot_general(a, b, (((1,), (1,)), ((0,), (0,))), preferred_element_type=F32)


def _pack_lhs(m):
    hi = m.astype(BF16).astype(F32)
    return jnp.concatenate([hi, m - hi, hi], axis=-1).astype(BF16)


def _pack_rhs(m):
    hi, lo = _split2(m)
    return jnp.concatenate([hi, hi, lo], axis=-2)


def _unit_lower_inverse_packed(a, eye, n_steps, fillers=()):
    c = a.shape[-1]
    upper = lax.broadcasted_iota(jnp.int32, a.shape[:-1] + (2 * c,), a.ndim - 1) >= c
    fillers = list(fillers)
    filled = []
    a2 = _bmm(_pack_lhs(a), _pack_rhs(a))
    mp = jnp.concatenate([a2, eye - a], axis=-1)
    for _ in range(n_steps):
        if fillers:
            filled.append(fillers.pop(0)())
        res = _bmm(_pack_lhs(mp[..., :c]), _pack_rhs(mp))
        mp = res + jnp.where(upper, mp, 0.0)
    filled.extend(f() for f in fillers)
    return mp[..., c:], filled


def _mixer_kernel(cfg, x_ref, cst_ref, s0_ref, r0_ref, cos_ref, sin_ref, tri_ref, dmat_ref,
                  rscale_ref, cdec_ref, anw_ref, wdn_ref, wret_ref, wba_ref, convw_ref, gpar_ref,
                  dnw_ref, retw_ref, wout_ref, h_ref, cnew_ref, s_ref, r_ref, e_ref):
    sb, tt, chunk, chained = cfg
    rows = sb * tt
    nblk = rows // BLK
    n_bat = N_HEADS * nblk
    shift = int(math.log2(chunk))
    n_steps = shift - 1
    base = SUBLANES - (DN_CONV - 1)
    l_idx = pl.program_id(1)

    @pl.when(l_idx == 0)
    def _():
        e_ref[:, base:SUBLANES, :] = cst_ref[...]
        if chained:
            s_ref[...] = s0_ref[...]
            r_ref[...] = r0_ref[...]

    x = x_ref[...].reshape(rows, D_MODEL)
    a = _rmsnorm(x, anw_ref[...]).astype(BF16)
    e_ref[:, SUBLANES:, :] = jnp.dot(a, wdn_ref[:, 0:3 * QK],
                                     preferred_element_type=F32).reshape(sb, tt, 3 * QK)
    slabs = [wdn_ref.at[:, 3 * QK:4 * QK]] + [wret_ref.at[:, i * QK:(i + 1) * QK] for i in range(4)]
    proj = lambda i: jnp.dot(a, slabs[i][...], preferred_element_type=F32)
    pba = jnp.dot(a, wba_ref[...], preferred_element_type=F32)

    conv = e_ref[:, base:base + tt, :] * convw_ref[0:1, :]
    for j in range(1, DN_CONV):
        conv = conv + e_ref[:, base + j:base + j + tt, :] * convw_ref[j:j + 1, :]
    tail = e_ref[:, tt + base:tt + SUBLANES, :]
    cnew_ref[...] = tail
    e_ref[:, base:SUBLANES, :] = tail
    rq = proj(1)
    qkv = _silu(conv).reshape(rows, 3 * QK)
    rk = proj(2)

    beta_all = jax.nn.sigmoid(pba)
    g_all = -jnp.exp(gpar_ref[0:1, :]) * _softplus(pba + gpar_ref[1:2, :])
    g_hi = g_all.astype(BF16)
    g_r1 = g_all - g_hi.astype(F32)
    g_mid = g_r1.astype(BF16)
    g_lo = (g_r1 - g_mid.astype(F32)).astype(BF16)
    g_split = jnp.concatenate([t.reshape(nblk, BLK, LANES) for t in (g_hi, g_mid, g_lo)], axis=1)
    sums = _bmm(jnp.broadcast_to(tri_ref[...][None], (nblk, 2 * BLK, 3 * BLK)), g_split)
    gc_all = sums[:, :BLK].reshape(rows, LANES)
    gl_all = sums[:, BLK:].reshape(rows, LANES)
    gct_all = gc_all.T

    ri = lax.broadcasted_iota(jnp.int32, (BLK, BLK), 0)
    ci = lax.broadcasted_iota(jnp.int32, (BLK, BLK), 1)
    same = (ri >> shift) == (ci >> shift)
    causal = same & (ci <= ri)
    strict = same & (ci < ri)
    eye = (ri == ci).astype(F32)

    cos = cos_ref[0]
    sin = sin_ref[0]
    even = (lax.broadcasted_iota(jnp.int32, (rows, LANES), 1) & 1) == 0

    def rotary(t):
        swapped = jnp.where(even, pltpu.roll(t, LANES - 1, 1), pltpu.roll(t, 1, 1))
        return t * cos + swapped * sin

    per = {k: [] for k in ("qb", "kb", "kbeta", "rhs", "qd", "kt", "gcc", "gct", "egl",
                           "rq", "rk", "rqd", "rkt", "rv")}
    blocks = lambda t: t.reshape(nblk, BLK, t.shape[-1])
    for h in range(N_HEADS):
        qh = qkv[:, h * HEAD_DIM:(h + 1) * HEAD_DIM]
        kh = qkv[:, QK + h * HEAD_DIM:QK + (h + 1) * HEAD_DIM]
        vh = qkv[:, 2 * QK + h * HEAD_DIM:2 * QK + (h + 1) * HEAD_DIM]
        qb = qh * lax.rsqrt(jnp.sum(qh * qh, axis=-1, keepdims=True) + EPS) * (HEAD_DIM ** -0.5)
        kb = kh * lax.rsqrt(jnp.sum(kh * kh, axis=-1, keepdims=True) + EPS)
        bcol = _lane_bcast(beta_all, h)
        gcc = _lane_bcast(gc_all, GATE_LANE0 + h)
        glc = _lane_bcast(gl_all, GATE_LANE0 + h)
        egc = jnp.exp(gcc)
        kbeta = kb * bcol
        per["qb"].append(blocks(qb))
        per["kb"].append(blocks(kb))
        per["kbeta"].append(blocks(kbeta))
        per["rhs"].append(blocks(jnp.concatenate([kbeta * egc, vh * bcol], axis=1)))
        per["qd"].append(blocks(qb * egc))
        per["kt"].append(blocks(kb * jnp.exp(glc - gcc)))
        per["gcc"].append(blocks(gcc))
        per["egl"].append(jnp.exp(glc).reshape(rows // chunk, chunk, HEAD_DIM)[:, 0:1, :])
        per["gct"].extend(gct_all[GATE_LANE0 + h:GATE_LANE0 + h + 1, b * BLK:(b + 1) * BLK][None]
                          for b in range(nblk))
    bat = {k: jnp.concatenate(v, axis=0) for k, v in per.items() if v}
    bf = lambda k: bat[k].astype(BF16)
    rv = proj(3)

    delta = bat["gcc"][:, :, :BLK] - bat["gct"]
    decay = jnp.exp(jnp.where(causal, delta, -jnp.inf))
    kb16 = bf("kb")
    kq = _bmm_nt(jnp.concatenate([bat["kbeta"], bat["qb"]], axis=1).astype(BF16), kb16)
    amat = jnp.where(strict, kq[:, :BLK] * decay, 0.0)
    half = lambda i, j: (lambda: jnp.dot(a, slabs[i][:, j * MXU_DIM:(j + 1) * MXU_DIM],
                                         preferred_element_type=F32))
    tinv, (z0, z1, rg0, rg1) = _unit_lower_inverse_packed(
        amat, eye, n_steps, fillers=[half(0, 0), half(0, 1), half(4, 0), half(4, 1)])
    z = jnp.concatenate([z0, z1], axis=1)
    rg = jnp.concatenate([rg0, rg1], axis=1)
    for h in range(N_HEADS):
        hs = slice(h * HEAD_DIM, (h + 1) * HEAD_DIM)
        rqh = rotary(rq[:, hs])
        rkh = rotary(rk[:, hs]) * (HEAD_DIM ** -0.5)
        per["rq"].append(blocks(rqh))
        per["rk"].append(blocks(rkh))
        per["rqd"].append(blocks(rqh * rscale_ref[0, :, hs]))
        per["rkt"].append(blocks(rkh * rscale_ref[1, :, hs]))
        per["rv"].append(blocks(rv[:, hs]))
    bat.update({k: jnp.concatenate(per[k], axis=0) for k in ("rq", "rk", "rqd", "rkt", "rv")})
    wu = _bmm(_pack_lhs(tinv), _pack_rhs(bat["rhs"]))
    qk = (kq[:, BLK:] * decay).astype(BF16)
    rv16 = bf("rv")
    dmat = jnp.broadcast_to(dmat_ref[...][:, None], (N_HEADS, nblk, BLK, BLK)).reshape(n_bat, BLK, BLK)
    rqk = (_bmm_nt(bf("rq"), bf("rk")) * dmat).astype(BF16)

    if chained:
        by_head = lambda t: t.reshape((N_HEADS, nblk) + t.shape[1:])
        transposed16 = lambda k: jnp.swapaxes(bat[k], 1, 2).astype(BF16)
        qg_wu = _bmm(jnp.concatenate([qk, transposed16("kt")], axis=1), wu.astype(BF16))
        qo = qg_wu[:, :BLK]
        gh = by_head(qg_wu[:, BLK:])
        qp = by_head((bat["qd"] - qo[:, :, :HEAD_DIM]).astype(BF16))
        op = by_head(qo[:, :, HEAD_DIM:])
        egl = by_head(bat["egl"])
        ro_rh = _bmm(jnp.concatenate([rqk, transposed16("rkt")], axis=1), rv16)
        ro = by_head(ro_rh[:, :BLK])
        rh = by_head(ro_rh[:, BLK:])
        rqd = by_head(bf("rqd"))
        s_cur = s_ref[0]
        r_cur = r_ref[0]
        o_dn = []
        o_ret = []
        for c in range(nblk):
            s16 = s_cur.astype(BF16)
            qg = jnp.concatenate([qp[:, c], gh[:, c, :, :HEAD_DIM].astype(BF16)], axis=1)
            qgs = _bmm(qg, s16)
            o_dn.append(op[:, c] + qgs[:, :BLK])
            s_cur = s_cur * egl[:, c] - qgs[:, BLK:] + gh[:, c, :, HEAD_DIM:]
            o_ret.append(ro[:, c] + _bmm(rqd[:, c], r_cur.astype(BF16)))
            r_cur = r_cur * cdec_ref[...] + rh[:, c]
        s_ref[0] = s_cur
        r_ref[0] = r_cur
        head_rows = lambda parts, h: jnp.concatenate([p[h] for p in parts], axis=0)
        o_dn = [head_rows(o_dn, h) for h in range(N_HEADS)]
        o_ret = [head_rows(o_ret, h) for h in range(N_HEADS)]
    else:
        n_seq = N_HEADS * sb
        seqs = lambda t: t.reshape(n_seq, tt, t.shape[-1])
        s_old = jnp.concatenate([s0_ref[:, h] for h in range(N_HEADS)], axis=0)
        lhs = jnp.concatenate([seqs(wu[:, :, :HEAD_DIM]), seqs(bat["qd"])], axis=1).astype(BF16)
        res = _bmm(lhs, s_old.astype(BF16))
        v_new = seqs(wu[:, :, HEAD_DIM:]) - res[:, :tt]
        v16 = v_new.astype(BF16)
        s_new = s_old * bat["egl"] + _bmm_tn(seqs(bat["kt"]).astype(BF16), v16)
        o = res[:, tt:].reshape(n_bat, BLK, HEAD_DIM) + _bmm(qk, v16.reshape(n_bat, BLK, HEAD_DIM))
        r_old = jnp.concatenate([r0_ref[:, h] for h in range(N_HEADS)], axis=0)
        ro = _bmm(rqk, rv16) + _bmm(seqs(bat["rqd"]).astype(BF16),
                                    r_old.astype(BF16)).reshape(n_bat, BLK, HEAD_DIM)
        cdec = jnp.broadcast_to(cdec_ref[...][:, None], (N_HEADS, sb, 1, HEAD_DIM))
        r_new = (r_old * cdec.reshape(n_seq, 1, HEAD_DIM)
                 + _bmm_tn(seqs(bat["rkt"]).astype(BF16), seqs(bat["rv"]).astype(BF16)))
        for h in range(N_HEADS):
            s_ref[:, h] = s_new[h * sb:(h + 1) * sb]
            r_ref[:, h] = r_new[h * sb:(h + 1) * sb]
        o_dn = [o[h * nblk:(h + 1) * nblk].reshape(rows, HEAD_DIM) for h in range(N_HEADS)]
        o_ret = [ro[h * nblk:(h + 1) * nblk].reshape(rows, HEAD_DIM) for h in range(N_HEADS)]

    outs = []
    for h in range(N_HEADS):
        hs = slice(h * HEAD_DIM, (h + 1) * HEAD_DIM)
        outs.append(_rmsnorm(o_dn[h], dnw_ref[...]) * _silu(z[:, hs]))
    for h in range(N_HEADS):
        hs = slice(h * HEAD_DIM, (h + 1) * HEAD_DIM)
        o = o_ret[h]
        mu = jnp.mean(o, axis=-1, keepdims=True)
        var = jnp.mean(jnp.square(o - mu), axis=-1, keepdims=True)
        outs.append((o - mu) * lax.rsqrt(var + EPS) * retw_ref[:, hs] * _silu(rg[:, hs]))
    mix = jnp.concatenate(outs, axis=1).astype(BF16)
    h_new = x + jnp.dot(mix, wout_ref[...], preferred_element_type=F32)
    h_ref[...] = h_new.reshape(sb, tt, D_MODEL)


def _ffn_kernel(cfg, h_ref, p_ref, fst_ref, fnw_ref, wup_ref, fcw_ref, fcb_ref, wdown_ref,
                pnw_ref, wgate_ref, wple_ref, finw_ref,
                y_ref, fnew_ref, e_ref):
    sb, tt, final_norm = cfg
    rows = sb * tt
    l_idx = pl.program_id(1)
    base = SUBLANES - (FFN_CONV - 1)

    @pl.when(l_idx == 0)
    def _():
        e_ref[:, base:SUBLANES, :] = fst_ref[...]

    h = h_ref[...].reshape(rows, D_MODEL)
    m = _rmsnorm(h, fnw_ref[...]).astype(BF16)
    for half in (0, D_FF):
        for c0, cw in FF_SLABS:
            cs = slice(half + c0, half + c0 + cw)
            e_ref[:, SUBLANES:, cs] = jnp.dot(m, wup_ref[:, cs],
                                              preferred_element_type=F32).reshape(sb, tt, cw)

    def conv(cs):
        acc = e_ref[:, base:base + tt, cs] * fcw_ref[0:1, cs]
        for j in range(1, FFN_CONV):
            acc = acc + e_ref[:, base + j:base + j + tt, cs] * fcw_ref[j:j + 1, cs]
        return (acc + fcb_ref[:, cs]).reshape(rows, cs.stop - cs.start)

    acc = h
    for c0, cw in FF_SLABS:
        ug = conv(slice(c0, c0 + cw))
        uv = conv(slice(D_FF + c0, D_FF + c0 + cw))
        act = (_silu(ug) * uv).astype(BF16)
        acc = acc + jnp.dot(act, wdown_ref[c0:c0 + cw, :], preferred_element_type=F32)
    tail = e_ref[:, tt + base:tt + SUBLANES, :]
    fnew_ref[...] = tail
    e_ref[:, base:SUBLANES, :] = tail

    h = acc
    gate = jax.nn.sigmoid(jnp.dot(_rmsnorm(h, pnw_ref[...]).astype(BF16), wgate_ref[...],
                                  preferred_element_type=F32))
    pp = jnp.dot(p_ref[...].reshape(rows, PLE_DIM).astype(BF16), wple_ref[...],
                 preferred_element_type=F32)
    h = h + gate * pp
    if final_norm:
        h = _rmsnorm(h, finw_ref[...])
    y_ref[...] = h.reshape(sb, tt, D_MODEL)


def _const_spec(shape):
    zeros = (0,) * len(shape)
    return pl.BlockSpec(shape, lambda b, l: zeros, pipeline_mode=pl.Buffered(1))


def _retention_tables(chunk, rows):
    hh = np.arange(N_HEADS, dtype=np.float64)
    log_gamma = np.log(1.0 - 2.0 ** (-5.0 - hh))
    bpos = (np.arange(chunk, dtype=np.float64) + 1.0)[None, :] * log_gamma[:, None]
    causal = np.tril(np.ones((chunk, chunk), dtype=bool))
    dmat = np.where(causal, np.exp(np.where(causal, bpos[:, :, None] - bpos[:, None, :], 0.0)), 0.0)
    n = BLK // chunk
    dblk = np.einsum('ab,hij->haibj', np.eye(n), dmat).reshape(N_HEADS, BLK, BLK)
    qscale = np.exp(bpos)
    kscale = np.exp(bpos[:, -1:] - bpos)
    expand = lambda t: np.repeat(np.tile(t.T, (rows // chunk, 1)), HEAD_DIM, axis=1)
    rscale = np.stack([expand(qscale), expand(kscale)])
    cdec = np.broadcast_to(np.exp(bpos[:, -1]).reshape(N_HEADS, 1, 1), (N_HEADS, 1, HEAD_DIM))
    return tuple(jnp.asarray(t, F32) for t in (dblk, rscale, cdec))


def _chunk_sum_matrix(chunk):
    r = np.arange(BLK)
    same = (r[:, None] // chunk) == (r[None, :] // chunk)
    causal = same & (r[None, :] <= r[:, None])
    return jnp.asarray(np.tile(np.concatenate([causal, same], axis=0), (1, 3)), BF16)


def _rope_tables(pos0, tt, sb, n_l):
    inv = ROPE_BASE ** (-np.arange(0, HEAD_DIM, 2, dtype=np.float64) / HEAD_DIM)
    ang = (pos0 + np.arange(n_l * tt, dtype=np.float64))[:, None] * inv[None, :]
    cos = np.repeat(np.cos(ang), 2, axis=1)
    sin = np.stack([-np.sin(ang), np.sin(ang)], axis=-1).reshape(n_l * tt, HEAD_DIM)
    tile = lambda t: jnp.asarray(
        np.tile(t.reshape(n_l, 1, tt, HEAD_DIM), (1, sb, 1, 1)).reshape(n_l, sb * tt, HEAD_DIM), F32)
    return tile(cos), tile(sin)


def _mixer(x, cstate, s0, r0, pos0, wts, sb, tt):
    n_b, seq, _ = x.shape
    chunk = CHUNK if seq % CHUNK == 0 else seq
    chained = chunk == BLK
    rows = sb * tt
    n_l = seq // tt
    assert n_b % sb == 0 and seq % tt == 0 and rows % BLK == 0 and BLK % chunk == 0
    assert (sb == 1 and tt % chunk == 0) if chained else (tt == seq == chunk)
    cos, sin = _rope_tables(pos0, tt, sb, n_l)
    dmat, rscale, cdec = _retention_tables(chunk, rows)
    tri = _chunk_sum_matrix(chunk)
    state_spec = pl.BlockSpec((sb, N_HEADS, HEAD_DIM, HEAD_DIM), lambda b, l: (b, 0, 0, 0))
    x_spec = pl.BlockSpec((sb, tt, D_MODEL), lambda b, l: (b, l, 0))
    conv_spec = pl.BlockSpec((sb, DN_CONV - 1, 3 * QK), lambda b, l: (b, 0, 0))
    rope_spec = pl.BlockSpec((1, rows, HEAD_DIM), lambda b, l: (l, 0, 0))
    consts = (tri, dmat, rscale, cdec) + tuple(wts)
    return pl.pallas_call(
        functools.partial(_mixer_kernel, (sb, tt, chunk, chained)),
        grid=(n_b // sb, n_l),
        in_specs=[x_spec, conv_spec, state_spec, state_spec, rope_spec, rope_spec]
        + [_const_spec(c.shape) for c in consts],
        out_specs=[x_spec, conv_spec, state_spec, state_spec],
        out_shape=[jax.ShapeDtypeStruct(x.shape, F32),
                   jax.ShapeDtypeStruct(cstate.shape, F32),
                   jax.ShapeDtypeStruct(s0.shape, F32),
                   jax.ShapeDtypeStruct(r0.shape, F32)],
        scratch_shapes=[pltpu.VMEM((sb, SUBLANES + tt, 3 * QK), F32)],
        compiler_params=pltpu.CompilerParams(
            dimension_semantics=("arbitrary", "arbitrary"), vmem_limit_bytes=VMEM_LIMIT),
        name="mixer",
    )(x, cstate, s0, r0, cos, sin, *consts)


def _ffn(h, p, fstate, wts, final_norm, sb, tt):
    n_b, seq, _ = h.shape
    n_l = seq // tt
    assert n_b % sb == 0 and seq % tt == 0
    x_spec = pl.BlockSpec((sb, tt, D_MODEL), lambda b, l: (b, l, 0))
    p_spec = pl.BlockSpec((sb, tt, PLE_DIM), lambda b, l: (b, l, 0))
    st_spec = pl.BlockSpec((sb, FFN_CONV - 1, 2 * D_FF), lambda b, l: (b, 0, 0))
    return pl.pallas_call(
        functools.partial(_ffn_kernel, (sb, tt, final_norm)),
        grid=(n_b // sb, n_l),
        in_specs=[x_spec, p_spec, st_spec] + [_const_spec(w.shape) for w in wts],
        out_specs=[x_spec, st_spec],
        out_shape=[jax.ShapeDtypeStruct(h.shape, F32), jax.ShapeDtypeStruct(fstate.shape, F32)],
        scratch_shapes=[pltpu.VMEM((sb, SUBLANES + tt, 2 * D_FF), F32)],
        compiler_params=pltpu.CompilerParams(
            dimension_semantics=("arbitrary", "arbitrary"), vmem_limit_bytes=VMEM_LIMIT),
        name="ffn",
    )(h, p, fstate, *wts)


def kernel(x_prompt, x_sample, p_prompt, p_sample, state_dn_conv, state_dn, state_ret, state_ffn_conv, attn_norm_w, w_in, dn_conv_w, dn_A_log, dn_dt_bias, dn_norm_w, ret_norm_w, w_out, ffn_norm_w, w_up, ffn_conv_w, ffn_conv_b, w_down, ple_norm_w, w_ple_gate, w_ple, final_norm_w):
    depth = w_in.shape[0]
    n_bp = x_prompt.shape[0]
    dec_len = x_sample.shape[1]
    row = lambda v: v.reshape(1, -1).astype(F32)
    hp, hs = x_prompt, x_sample
    outs_p = ([], [], [], [])
    outs_s = ([], [], [], [])
    for i in range(depth):
        wi = w_in[i]
        wdn = wi[:, :4 * QK].astype(BF16)
        wret = wi[:, 4 * QK + 2 * N_HEADS:].astype(BF16)
        wba = jnp.pad(wi[:, 4 * QK:4 * QK + 2 * N_HEADS].astype(BF16),
                      ((0, 0), (0, LANES - 2 * N_HEADS)))
        gpar = jnp.zeros((2, LANES), F32)
        gpar = gpar.at[0, GATE_LANE0:GATE_LANE0 + N_HEADS].set(dn_A_log[i])
        gpar = gpar.at[1, GATE_LANE0:GATE_LANE0 + N_HEADS].set(dn_dt_bias[i])
        mixer_w = (row(attn_norm_w[i]), wdn, wret, wba, dn_conv_w[i], gpar, row(dn_norm_w[i]),
                   row(ret_norm_w[i]), w_out[i].astype(BF16))
        ffn_w = (row(ffn_norm_w[i]), w_up[i].astype(BF16), ffn_conv_w[i], row(ffn_conv_b[i]),
                 w_down[i].astype(BF16), row(ple_norm_w[i]), w_ple_gate[i].astype(BF16),
                 w_ple[i].astype(BF16), row(final_norm_w))
        last = i == depth - 1
        zc = jnp.zeros((n_bp, DN_CONV - 1, 3 * QK), F32)
        zs = jnp.zeros((n_bp, N_HEADS, HEAD_DIM, HEAD_DIM), F32)
        zf = jnp.zeros((n_bp, FFN_CONV - 1, 2 * D_FF), F32)
        hp, c1, s1, r1 = _mixer(hp, zc, zs, zs, 0, mixer_w, *PROMPT_MIXER_TILE)
        hp, f1 = _ffn(hp, p_prompt[i], zf, ffn_w, last, *PROMPT_FFN_TILE)
        hs, c2, s2, r2 = _mixer(hs, state_dn_conv[i], state_dn[i], state_ret[i], PAST_LEN,
                                mixer_w, SAMPLE_MIXER_SEQS, dec_len)
        hs, f2 = _ffn(hs, p_sample[i], state_ffn_conv[i], ffn_w, last, SAMPLE_FFN_SEQS, dec_len)
        for lst, val in zip(outs_p, (c1, s1, r1, f1)):
            lst.append(val)
        for lst, val in zip(outs_s, (c2, s2, r2, f2)):
            lst.append(val)
    return (hp, hs,
            jnp.stack(outs_p[0]), jnp.stack(outs_p[1]), jnp.stack(outs_p[2]), jnp.stack(outs_p[3]),
            jnp.stack(outs_s[0]), jnp.stack(outs_s[1]), jnp.stack(outs_s[2]), jnp.stack(outs_s[3]))
```

```python
import functools
import math

import jax
import jax.numpy as jnp
import numpy as np
from jax import lax
from jax.experimental import pallas as pl
from jax.experimental.pallas import tpu as pltpu

F32 = jnp.float32
BF16 = jnp.bfloat16

D_MODEL = 1024
HEAD_DIM = 128
N_HEADS = 4
QK = N_HEADS * HEAD_DIM
DN_CONV = 4
FFN_CONV = 3
D_FF = 2816
PLE_DIM = 256
CHUNK = 64
PAST_LEN = 16384
ROPE_BASE = 10000.0
EPS = 1e-6
LANES = 128
SUBLANES = 8
MXU_DIM = 256
BLK = 64
GATE_LANE0 = N_HEADS
FF_SLABS = ((0, 1280), (1280, 1536))
VMEM_LIMIT = 56 * 1024 * 1024
PROMPT_MIXER_TILE = (1, 256)
PROMPT_FFN_TILE = (1, 512)
SAMPLE_MIXER_SEQS = 16
SAMPLE_FFN_SEQS = 32


def _split2(x):
    hi = x.astype(BF16)
    lo = (x - hi.astype(F32)).astype(BF16)
    return hi, lo


def _rmsnorm(x, w):
    return x * lax.rsqrt(jnp.mean(x * x, axis=-1, keepdims=True) + EPS) * w


def _silu(x):
    return x * jax.nn.sigmoid(x)


def _softplus(x):
    return jnp.maximum(x, 0.0) + jnp.log(1.0 + jnp.exp(-jnp.abs(x)))


def _lane_bcast(x, lane, width=LANES):
    return jnp.broadcast_to(x[:, lane:lane + 1], (x.shape[0], width))


def _bmm(a, b):
    return lax.dot_general(a, b, (((2,), (1,)), ((0,), (0,))), preferred_element_type=F32)


def _bmm_nt(a, b):
    return lax.dot_general(a, b, (((2,), (2,)), ((0,), (0,))), preferred_element_type=F32)


def _bmm_tn(a, b):
    return lax.dot_general(a, b, (((1,), (1,)), ((0,), (0,))), preferred_element_type=F32)


def _pack_lhs(m):
    hi = m.astype(BF16).astype(F32)
    return jnp.concatenate([hi, m - hi, hi], axis=-1).astype(BF16)


def _pack_rhs(m):
    hi, lo = _split2(m)
    return jnp.concatenate([hi, hi, lo], axis=-2)


def _unit_lower_inverse_packed(a, eye, n_steps):
    c = a.shape[-1]
    upper = lax.broadcasted_iota(jnp.int32, a.shape[:-1] + (2 * c,), a.ndim - 1) >= c
    a2 = _bmm(_pack_lhs(a), _pack_rhs(a))
    mp = jnp.concatenate([a2, eye - a], axis=-1)
    for _ in range(n_steps):
        res = _bmm(_pack_lhs(mp[..., :c]), _pack_rhs(mp))
        mp = res + jnp.where(upper, mp, 0.0)
    return mp[..., c:]


def _mixer_kernel(cfg, x_ref, cst_ref, s0_ref, r0_ref, cos_ref, sin_ref, tri_ref, dmat_ref,
                  rscale_ref, cdec_ref, anw_ref, wdn_ref, wret_ref, wba_ref, convw_ref, gpar_ref,
                  dnw_ref, retw_ref, wout_ref, h_ref, cnew_ref, s_ref, r_ref, e_ref):
    sb, tt, chunk, chained = cfg
    rows = sb * tt
    nblk = rows // BLK
    n_bat = N_HEADS * nblk
    shift = int(math.log2(chunk))
    n_steps = shift - 1
    base = SUBLANES - (DN_CONV - 1)
    l_idx = pl.program_id(1)

    @pl.when(l_idx == 0)
    def _():
        e_ref[:, base:SUBLANES, :] = cst_ref[...]
        if chained:
            s_ref[...] = s0_ref[...]
            r_ref[...] = r0_ref[...]

    x = x_ref[...].reshape(rows, D_MODEL)
    a = _rmsnorm(x, anw_ref[...]).astype(BF16)
    e_ref[:, SUBLANES:, :] = jnp.dot(a, wdn_ref[:, 0:3 * QK],
                                     preferred_element_type=F32).reshape(sb, tt, 3 * QK)
    slabs = [wdn_ref.at[:, 3 * QK:4 * QK]] + [wret_ref.at[:, i * QK:(i + 1) * QK] for i in range(4)]
    proj = lambda i: jnp.dot(a, slabs[i][...], preferred_element_type=F32)
    pba = jnp.dot(a, wba_ref[...], preferred_element_type=F32)

    conv = e_ref[:, base:base + tt, :] * convw_ref[0:1, :]
    for j in range(1, DN_CONV):
        conv = conv + e_ref[:, base + j:base + j + tt, :] * convw_ref[j:j + 1, :]
    tail = e_ref[:, tt + base:tt + SUBLANES, :]
    cnew_ref[...] = tail
    e_ref[:, base:SUBLANES, :] = tail
    rq = proj(1)
    qkv = _silu(conv).reshape(rows, 3 * QK)
    rk = proj(2)

    beta_all = jax.nn.sigmoid(pba)
    g_all = -jnp.exp(gpar_ref[0:1, :]) * _softplus(pba + gpar_ref[1:2, :])
    g_hi = g_all.astype(BF16)
    g_r1 = g_all - g_hi.astype(F32)
    g_mid = g_r1.astype(BF16)
    g_lo = (g_r1 - g_mid.astype(F32)).astype(BF16)
    g_split = jnp.concatenate([t.reshape(nblk, BLK, LANES) for t in (g_hi, g_mid, g_lo)], axis=1)
    sums = _bmm(jnp.broadcast_to(tri_ref[...][None], (nblk, 2 * BLK, 3 * BLK)), g_split)
    gc_all = sums[:, :BLK].reshape(rows, LANES)
    gl_all = sums[:, BLK:].reshape(rows, LANES)
    gct_all = gc_all.T

    ri = lax.broadcasted_iota(jnp.int32, (BLK, BLK), 0)
    ci = lax.broadcasted_iota(jnp.int32, (BLK, BLK), 1)
    same = (ri >> shift) == (ci >> shift)
    causal = same & (ci <= ri)
    strict = same & (ci < ri)
    eye = (ri == ci).astype(F32)

    cos = cos_ref[0]
    sin = sin_ref[0]
    even = (lax.broadcasted_iota(jnp.int32, (rows, LANES), 1) & 1) == 0

    def rotary(t):
        swapped = jnp.where(even, pltpu.roll(t, LANES - 1, 1), pltpu.roll(t, 1, 1))
        return t * cos + swapped * sin

    per = {k: [] for k in ("qb", "kb", "kbeta", "rhs", "qd", "kt", "gcc", "gct", "egl",
                           "rq", "rk", "rqd", "rkt", "rv")}
    blocks = lambda t: t.reshape(nblk, BLK, t.shape[-1])
    for h in range(N_HEADS):
        qh = qkv[:, h * HEAD_DIM:(h + 1) * HEAD_DIM]
        kh = qkv[:, QK + h * HEAD_DIM:QK + (h + 1) * HEAD_DIM]
        vh = qkv[:, 2 * QK + h * HEAD_DIM:2 * QK + (h + 1) * HEAD_DIM]
        qb = qh * lax.rsqrt(jnp.sum(qh * qh, axis=-1, keepdims=True) + EPS) * (HEAD_DIM ** -0.5)
        kb = kh * lax.rsqrt(jnp.sum(kh * kh, axis=-1, keepdims=True) + EPS)
        bcol = _lane_bcast(beta_all, h)
        gcc = _lane_bcast(gc_all, GATE_LANE0 + h)
        glc = _lane_bcast(gl_all, GATE_LANE0 + h)
        egc = jnp.exp(gcc)
        kbeta = kb * bcol
        per["qb"].append(blocks(qb))
        per["kb"].append(blocks(kb))
        per["kbeta"].append(blocks(kbeta))
        per["rhs"].append(blocks(jnp.concatenate([kbeta * egc, vh * bcol], axis=1)))
        per["qd"].append(blocks(qb * egc))
        per["kt"].append(blocks(kb * jnp.exp(glc - gcc)))
        per["gcc"].append(blocks(gcc))
        per["egl"].append(jnp.exp(glc).reshape(rows // chunk, chunk, HEAD_DIM)[:, 0:1, :])
        per["gct"].extend(gct_all[GATE_LANE0 + h:GATE_LANE0 + h + 1, b * BLK:(b + 1) * BLK][None]
                          for b in range(nblk))
    bat = {k: jnp.concatenate(v, axis=0) for k, v in per.items() if v}
    bf = lambda k: bat[k].astype(BF16)
    rv = proj(3)

    delta = bat["gcc"][:, :, :BLK] - bat["gct"]
    decay = jnp.exp(jnp.where(causal, delta, -jnp.inf))
    kb16 = bf("kb")
    kq = _bmm_nt(jnp.concatenate([bat["kbeta"], bat["qb"]], axis=1).astype(BF16), kb16)
    amat = jnp.where(strict, kq[:, :BLK] * decay, 0.0)
    tinv = _unit_lower_inverse_packed(amat, eye, n_steps)
    z = proj(0)
    for h in range(N_HEADS):
        hs = slice(h * HEAD_DIM, (h + 1) * HEAD_DIM)
        rqh = rotary(rq[:, hs])
        rkh = rotary(rk[:, hs]) * (HEAD_DIM ** -0.5)
        per["rq"].append(blocks(rqh))
        per["rk"].append(blocks(rkh))
        per["rqd"].append(blocks(rqh * rscale_ref[0, :, hs]))
        per["rkt"].append(blocks(rkh * rscale_ref[1, :, hs]))
        per["rv"].append(blocks(rv[:, hs]))
    bat.update({k: jnp.concatenate(per[k], axis=0) for k in ("rq", "rk", "rqd", "rkt", "rv")})
    rg = proj(4)
    wu = _bmm(_pack_lhs(tinv), _pack_rhs(bat["rhs"]))
    qk = (kq[:, BLK:] * decay).astype(BF16)
    rv16 = bf("rv")
    dmat = jnp.broadcast_to(dmat_ref[...][:, None], (N_HEADS, nblk, BLK, BLK)).reshape(n_bat, BLK, BLK)
    rqk = (_bmm_nt(bf("rq"), bf("rk")) * dmat).astype(BF16)

    if chained:
        by_head = lambda t: t.reshape((N_HEADS, nblk) + t.shape[1:])
        transposed16 = lambda k: jnp.swapaxes(bat[k], 1, 2).astype(BF16)
        wq = by_head(jnp.concatenate([wu[:, :, :HEAD_DIM], bat["qd"]], axis=1).astype(BF16))
        u4 = by_head(wu[:, :, HEAD_DIM:])
        qkt = by_head(jnp.concatenate([qk, transposed16("kt")], axis=1))
        egl = by_head(bat["egl"])
        ro_rh = _bmm(jnp.concatenate([rqk, transposed16("rkt")], axis=1), rv16)
        ro = by_head(ro_rh[:, :BLK])
        rh = by_head(ro_rh[:, BLK:])
        rqd = by_head(bf("rqd"))
        s_cur = s_ref[0]
        r_cur = r_ref[0]
        o_dn = []
        o_ret = []
        for c in range(nblk):
            s16 = s_cur.astype(BF16)
            res = _bmm(wq[:, c], s16)
            v_new = (u4[:, c] - res[:, :BLK]).astype(BF16)
            tail = _bmm(qkt[:, c], v_new)
            o_dn.append(res[:, BLK:] + tail[:, :BLK])
            s_cur = s_cur * egl[:, c] + tail[:, BLK:]
            o_ret.append(ro[:, c] + _bmm(rqd[:, c], r_cur.astype(BF16)))
            r_cur = r_cur * cdec_ref[...] + rh[:, c]
        s_ref[0] = s_cur
        r_ref[0] = r_cur
        head_rows = lambda parts, h: jnp.concatenate([p[h] for p in parts], axis=0)
        o_dn = [head_rows(o_dn, h) for h in range(N_HEADS)]
        o_ret = [head_rows(o_ret, h) for h in range(N_HEADS)]
    else:
        n_seq = N_HEADS * sb
        seqs = lambda t: t.reshape(n_seq, tt, t.shape[-1])
        s_old = jnp.concatenate([s0_ref[:, h] for h in range(N_HEADS)], axis=0)
        lhs = jnp.concatenate([seqs(wu[:, :, :HEAD_DIM]), seqs(bat["qd"])], axis=1).astype(BF16)
        res = _bmm(lhs, s_old.astype(BF16))
        v_new = seqs(wu[:, :, HEAD_DIM:]) - res[:, :tt]
        v16 = v_new.astype(BF16)
        s_new = s_old * bat["egl"] + _bmm_tn(seqs(bat["kt"]).astype(BF16), v16)
        o = res[:, tt:].reshape(n_bat, BLK, HEAD_DIM) + _bmm(qk, v16.reshape(n_bat, BLK, HEAD_DIM))
        r_old = jnp.concatenate([r0_ref[:, h] for h in range(N_HEADS)], axis=0)
        ro = _bmm(rqk, rv16) + _bmm(seqs(bat["rqd"]).astype(BF16),
                                    r_old.astype(BF16)).reshape(n_bat, BLK, HEAD_DIM)
        cdec = jnp.broadcast_to(cdec_ref[...][:, None], (N_HEADS, sb, 1, HEAD_DIM))
        r_new = (r_old * cdec.reshape(n_seq, 1, HEAD_DIM)
                 + _bmm_tn(seqs(bat["rkt"]).astype(BF16), seqs(bat["rv"]).astype(BF16)))
        for h in range(N_HEADS):
            s_ref[:, h] = s_new[h * sb:(h + 1) * sb]
            r_ref[:, h] = r_new[h * sb:(h + 1) * sb]
        o_dn = [o[h * nblk:(h + 1) * nblk].reshape(rows, HEAD_DIM) for h in range(N_HEADS)]
        o_ret = [ro[h * nblk:(h + 1) * nblk].reshape(rows, HEAD_DIM) for h in range(N_HEADS)]

    outs = []
    for h in range(N_HEADS):
        hs = slice(h * HEAD_DIM, (h + 1) * HEAD_DIM)
        outs.append(_rmsnorm(o_dn[h], dnw_ref[...]) * _silu(z[:, hs]))
    for h in range(N_HEADS):
        hs = slice(h * HEAD_DIM, (h + 1) * HEAD_DIM)
        o = o_ret[h]
        mu = jnp.mean(o, axis=-1, keepdims=True)
        var = jnp.mean(jnp.square(o - mu), axis=-1, keepdims=True)
        outs.append((o - mu) * lax.rsqrt(var + EPS) * retw_ref[:, hs] * _silu(rg[:, hs]))
    mix = jnp.concatenate(outs, axis=1).astype(BF16)
    h_new = x + jnp.dot(mix, wout_ref[...], preferred_element_type=F32)
    h_ref[...] = h_new.reshape(sb, tt, D_MODEL)


def _ffn_kernel(cfg, h_ref, p_ref, fst_ref, fnw_ref, wup_ref, fcw_ref, fcb_ref, wdown_ref,
                pnw_ref, wgate_ref, wple_ref, finw_ref,
                y_ref, fnew_ref, e_ref):
    sb, tt, final_norm = cfg
    rows = sb * tt
    l_idx = pl.program_id(1)
    base = SUBLANES - (FFN_CONV - 1)

    @pl.when(l_idx == 0)
    def _():
        e_ref[:, base:SUBLANES, :] = fst_ref[...]

    h = h_ref[...].reshape(rows, D_MODEL)
    m = _rmsnorm(h, fnw_ref[...]).astype(BF16)
    for half in (0, D_FF):
        for c0, cw in FF_SLABS:
            cs = slice(half + c0, half + c0 + cw)
            e_ref[:, SUBLANES:, cs] = jnp.dot(m, wup_ref[:, cs],
                                              preferred_element_type=F32).reshape(sb, tt, cw)

    def conv(cs):
        acc = e_ref[:, base:base + tt, cs] * fcw_ref[0:1, cs]
        for j in range(1, FFN_CONV):
            acc = acc + e_ref[:, base + j:base + j + tt, cs] * fcw_ref[j:j + 1, cs]
        return (acc + fcb_ref[:, cs]).reshape(rows, cs.stop - cs.start)

    acc = h
    for c0, cw in FF_SLABS:
        ug = conv(slice(c0, c0 + cw))
        uv = conv(slice(D_FF + c0, D_FF + c0 + cw))
        act = (_silu(ug) * uv).astype(BF16)
        acc = acc + jnp.dot(act, wdown_ref[c0:c0 + cw, :], preferred_element_type=F32)
    tail = e_ref[:, tt + base:tt + SUBLANES, :]
    fnew_ref[...] = tail
    e_ref[:, base:SUBLANES, :] = tail

    h = acc
    gate = jax.nn.sigmoid(jnp.dot(_rmsnorm(h, pnw_ref[...]).astype(BF16), wgate_ref[...],
                                  preferred_element_type=F32))
    pp = jnp.dot(p_ref[...].reshape(rows, PLE_DIM).astype(BF16), wple_ref[...],
                 preferred_element_type=F32)
    h = h + gate * pp
    if final_norm:
        h = _rmsnorm(h, finw_ref[...])
    y_ref[...] = h.reshape(sb, tt, D_MODEL)


def _const_spec(shape):
    zeros = (0,) * len(shape)
    return pl.BlockSpec(shape, lambda b, l: zeros, pipeline_mode=pl.Buffered(1))


def _retention_tables(chunk, rows):
    hh = np.arange(N_HEADS, dtype=np.float64)
    log_gamma = np.log(1.0 - 2.0 ** (-5.0 - hh))
    bpos = (np.arange(chunk, dtype=np.float64) + 1.0)[None, :] * log_gamma[:, None]
    causal = np.tril(np.ones((chunk, chunk), dtype=bool))
    dmat = np.where(causal, np.exp(np.where(causal, bpos[:, :, None] - bpos[:, None, :], 0.0)), 0.0)
    n = BLK // chunk
    dblk = np.einsum('ab,hij->haibj', np.eye(n), dmat).reshape(N_HEADS, BLK, BLK)
    qscale = np.exp(bpos)
    kscale = np.exp(bpos[:, -1:] - bpos)
    expand = lambda t: np.repeat(np.tile(t.T, (rows // chunk, 1)), HEAD_DIM, axis=1)
    rscale = np.stack([expand(qscale), expand(kscale)])
    cdec = np.broadcast_to(np.exp(bpos[:, -1]).reshape(N_HEADS, 1, 1), (N_HEADS, 1, HEAD_DIM))
    return tuple(jnp.asarray(t, F32) for t in (dblk, rscale, cdec))


def _chunk_sum_matrix(chunk):
    r = np.arange(BLK)
    same = (r[:, None] // chunk) == (r[None, :] // chunk)
    causal = same & (r[None, :] <= r[:, None])
    return jnp.asarray(np.tile(np.concatenate([causal, same], axis=0), (1, 3)), BF16)


def _rope_tables(pos0, tt, sb, n_l):
    inv = ROPE_BASE ** (-np.arange(0, HEAD_DIM, 2, dtype=np.float64) / HEAD_DIM)
    ang = (pos0 + np.arange(n_l * tt, dtype=np.float64))[:, None] * inv[None, :]
    cos = np.repeat(np.cos(ang), 2, axis=1)
    sin = np.stack([-np.sin(ang), np.sin(ang)], axis=-1).reshape(n_l * tt, HEAD_DIM)
    tile = lambda t: jnp.asarray(
        np.tile(t.reshape(n_l, 1, tt, HEAD_DIM), (1, sb, 1, 1)).reshape(n_l, sb * tt, HEAD_DIM), F32)
    return tile(cos), tile(sin)


def _mixer(x, cstate, s0, r0, pos0, wts, sb, tt):
    n_b, seq, _ = x.shape
    chunk = CHUNK if seq % CHUNK == 0 else seq
    chained = chunk == BLK
    rows = sb * tt
    n_l = seq // tt
    assert n_b % sb == 0 and seq % tt == 0 and rows % BLK == 0 and BLK % chunk == 0
    assert (sb == 1 and tt % chunk == 0) if chained else (tt == seq == chunk)
    cos, sin = _rope_tables(pos0, tt, sb, n_l)
    dmat, rscale, cdec = _retention_tables(chunk, rows)
    tri = _chunk_sum_matrix(chunk)
    state_spec = pl.BlockSpec((sb, N_HEADS, HEAD_DIM, HEAD_DIM), lambda b, l: (b, 0, 0, 0))
    x_spec = pl.BlockSpec((sb, tt, D_MODEL), lambda b, l: (b, l, 0))
    conv_spec = pl.BlockSpec((sb, DN_CONV - 1, 3 * QK), lambda b, l: (b, 0, 0))
    rope_spec = pl.BlockSpec((1, rows, HEAD_DIM), lambda b, l: (l, 0, 0))
    consts = (tri, dmat, rscale, cdec) + tuple(wts)
    return pl.pallas_call(
        functools.partial(_mixer_kernel, (sb, tt, chunk, chained)),
        grid=(n_b // sb, n_l),
        in_specs=[x_spec, conv_spec, state_spec, state_spec, rope_spec, rope_spec]
        + [_const_spec(c.shape) for c in consts],
        out_specs=[x_spec, conv_spec, state_spec, state_spec],
        out_shape=[jax.ShapeDtypeStruct(x.shape, F32),
                   jax.ShapeDtypeStruct(cstate.shape, F32),
                   jax.ShapeDtypeStruct(s0.shape, F32),
                   jax.ShapeDtypeStruct(r0.shape, F32)],
        scratch_shapes=[pltpu.VMEM((sb, SUBLANES + tt, 3 * QK), F32)],
        compiler_params=pltpu.CompilerParams(
            dimension_semantics=("arbitrary", "arbitrary"), vmem_limit_bytes=VMEM_LIMIT),
        name="mixer",
    )(x, cstate, s0, r0, cos, sin, *consts)


def _ffn(h, p, fstate, wts, final_norm, sb, tt):
    n_b, seq, _ = h.shape
    n_l = seq // tt
    assert n_b % sb == 0 and seq % tt == 0
    x_spec = pl.BlockSpec((sb, tt, D_MODEL), lambda b, l: (b, l, 0))
    p_spec = pl.BlockSpec((sb, tt, PLE_DIM), lambda b, l: (b, l, 0))
    st_spec = pl.BlockSpec((sb, FFN_CONV - 1, 2 * D_FF), lambda b, l: (b, 0, 0))
    return pl.pallas_call(
        functools.partial(_ffn_kernel, (sb, tt, final_norm)),
        grid=(n_b // sb, n_l),
        in_specs=[x_spec, p_spec, st_spec] + [_const_spec(w.shape) for w in wts],
        out_specs=[x_spec, st_spec],
        out_shape=[jax.ShapeDtypeStruct(h.shape, F32), jax.ShapeDtypeStruct(fstate.shape, F32)],
        scratch_shapes=[pltpu.VMEM((sb, SUBLANES + tt, 2 * D_FF), F32)],
        compiler_params=pltpu.CompilerParams(
            dimension_semantics=("arbitrary", "arbitrary"), vmem_limit_bytes=VMEM_LIMIT),
        name="ffn",
    )(h, p, fstate, *wts)


def kernel(x_prompt, x_sample, p_prompt, p_sample, state_dn_conv, state_dn, state_ret, state_ffn_conv, attn_norm_w, w_in, dn_conv_w, dn_A_log, dn_dt_bias, dn_norm_w, ret_norm_w, w_out, ffn_norm_w, w_up, ffn_conv_w, ffn_conv_b, w_down, ple_norm_w, w_ple_gate, w_ple, final_norm_w):
    depth = w_in.shape[0]
    n_bp = x_prompt.shape[0]
    dec_len = x_sample.shape[1]
    row = lambda v: v.reshape(1, -1).astype(F32)
    hp, hs = x_prompt, x_sample
    outs_p = ([], [], [], [])
    outs_s = ([], [], [], [])
    for i in range(depth):
        wi = w_in[i]
        wdn = wi[:, :4 * QK].astype(BF16)
        wret = wi[:, 4 * QK + 2 * N_HEADS:].astype(BF16)
        wba = jnp.pad(wi[:, 4 * QK:4 * QK + 2 * N_HEADS].astype(BF16),
                      ((0, 0), (0, LANES - 2 * N_HEADS)))
        gpar = jnp.zeros((2, LANES), F32)
        gpar = gpar.at[0, GATE_LANE0:GATE_LANE0 + N_HEADS].set(dn_A_log[i])
        gpar = gpar.at[1, GATE_LANE0:GATE_LANE0 + N_HEADS].set(dn_dt_bias[i])
        mixer_w = (row(attn_norm_w[i]), wdn, wret, wba, dn_conv_w[i], gpar, row(dn_norm_w[i]),
                   row(ret_norm_w[i]), w_out[i].astype(BF16))
        ffn_w = (row(ffn_norm_w[i]), w_up[i].astype(BF16), ffn_conv_w[i], row(ffn_conv_b[i]),
                 w_down[i].astype(BF16), row(ple_norm_w[i]), w_ple_gate[i].astype(BF16),
                 w_ple[i].astype(BF16), row(final_norm_w))
        last = i == depth - 1
        zc = jnp.zeros((n_bp, DN_CONV - 1, 3 * QK), F32)
        zs = jnp.zeros((n_bp, N_HEADS, HEAD_DIM, HEAD_DIM), F32)
        zf = jnp.zeros((n_bp, FFN_CONV - 1, 2 * D_FF), F32)
        hp, c1, s1, r1 = _mixer(hp, zc, zs, zs, 0, mixer_w, *PROMPT_MIXER_TILE)
        hp, f1 = _ffn(hp, p_prompt[i], zf, ffn_w, last, *PROMPT_FFN_TILE)
        hs, c2, s2, r2 = _mixer(hs, state_dn_conv[i], state_dn[i], state_ret[i], PAST_LEN,
                                mixer_w, SAMPLE_MIXER_SEQS, dec_len)
        hs, f2 = _ffn(hs, p_sample[i], state_ffn_conv[i], ffn_w, last, SAMPLE_FFN_SEQS, dec_len)
        for lst, val in zip(outs_p, (c1, s1, r1, f1)):
            lst.append(val)
        for lst, val in zip(outs_s, (c2, s2, r2, f2)):
            lst.append(val)
    return (hp, hs,
            jnp.stack(outs_p[0]), jnp.stack(outs_p[1]), jnp.stack(outs_p[2]), jnp.stack(outs_p[3]),
            jnp.stack(outs_s[0]), jnp.stack(outs_s[1]), jnp.stack(outs_s[2]), jnp.stack(outs_s[3]))
```

```python
import functools
import math

import jax
import jax.numpy as jnp
import numpy as np
from jax import lax
from jax.experimental import pallas as pl
from jax.experimental.pallas import tpu as pltpu

F32 = jnp.float32
BF16 = jnp.bfloat16

D_MODEL = 1024
HEAD_DIM = 128
N_HEADS = 4
QK = N_HEADS * HEAD_DIM
DN_CONV = 4
FFN_CONV = 3
D_FF = 2816
PLE_DIM = 256
CHUNK = 64
PAST_LEN = 16384
ROPE_BASE = 10000.0
EPS = 1e-6
LANES = 128
SUBLANES = 8
MXU_DIM = 256
BLK = 64
GATE_LANE0 = N_HEADS
FF_SLABS = tuple((c, MXU_DIM) for c in range(0, D_FF, MXU_DIM))
VMEM_LIMIT = 56 * 1024 * 1024
PROMPT_MIXER_TILE = (1, 256)
PROMPT_FFN_TILE = (1, 512)
SAMPLE_MIXER_SEQS = 16
SAMPLE_FFN_SEQS = 32


def _split2(x):
    hi = x.astype(BF16)
    lo = (x - hi.astype(F32)).astype(BF16)
    return hi, lo


def _rmsnorm(x, w):
    return x * lax.rsqrt(jnp.mean(x * x, axis=-1, keepdims=True) + EPS) * w


def _silu(x):
    return x * jax.nn.sigmoid(x)


def _softplus(x):
    return jnp.maximum(x, 0.0) + jnp.log(1.0 + jnp.exp(-jnp.abs(x)))


def _lane_bcast(x, lane, width=LANES):
    return jnp.broadcast_to(x[:, lane:lane + 1], (x.shape[0], width))


def _bmm(a, b):
    return lax.dot_general(a, b, (((2,), (1,)), ((0,), (0,))), preferred_element_type=F32)


def _bmm_nt(a, b):
    return lax.dot_general(a, b, (((2,), (2,)), ((0,), (0,))), preferred_element_type=F32)


def _bmm_tn(a, b):
    return lax.dot_general(a, b, (((1,), (1,)), ((0,), (0,))), preferred_element_type=F32)


def _pack_lhs(m):
    hi = m.astype(BF16).astype(F32)
    return jnp.concatenate([hi, m - hi, hi], axis=-1).astype(BF16)


def _pack_rhs(m):
    hi, lo = _split2(m)
    return jnp.concatenate([hi, hi, lo], axis=-2)


def _unit_lower_inverse_packed(a, eye, n_steps):
    c = a.shape[-1]
    upper = lax.broadcasted_iota(jnp.int32, a.shape[:-1] + (2 * c,), a.ndim - 1) >= c
    a2 = _bmm(_pack_lhs(a), _pack_rhs(a))
    mp = jnp.concatenate([a2, eye - a], axis=-1)
    for _ in range(n_steps):
        res = _bmm(_pack_lhs(mp[..., :c]), _pack_rhs(mp))
        mp = res + jnp.where(upper, mp, 0.0)
    return mp[..., c:]


def _mixer_kernel(cfg, x_ref, cst_ref, s0_ref, r0_ref, cos_ref, sin_ref, tri_ref, dmat_ref,
                  rscale_ref, cdec_ref, anw_ref, wdn_ref, wret_ref, wba_ref, convw_ref, gpar_ref,
                  dnw_ref, retw_ref, wout_ref, h_ref, cnew_ref, s_ref, r_ref, e_ref):
    sb, tt, chunk, chained = cfg
    rows = sb * tt
    nblk = rows // BLK
    n_bat = N_HEADS * nblk
    shift = int(math.log2(chunk))
    n_steps = shift - 1
    base = SUBLANES - (DN_CONV - 1)
    l_idx = pl.program_id(1)

    @pl.when(l_idx == 0)
    def _():
        e_ref[:, base:SUBLANES, :] = cst_ref[...]
        if chained:
            s_ref[...] = s0_ref[...]
            r_ref[...] = r0_ref[...]

    x = x_ref[...].reshape(rows, D_MODEL)
    a = _rmsnorm(x, anw_ref[...]).astype(BF16)
    e_ref[:, SUBLANES:, :] = jnp.dot(a, wdn_ref[:, 0:3 * QK],
                                     preferred_element_type=F32).reshape(sb, tt, 3 * QK)
    slabs = [wdn_ref.at[:, 3 * QK:4 * QK]] + [wret_ref.at[:, i * QK:(i + 1) * QK] for i in range(4)]
    proj = lambda i: jnp.dot(a, slabs[i][...], preferred_element_type=F32)
    pba = jnp.dot(a, wba_ref[...], preferred_element_type=F32)

    conv = e_ref[:, base:base + tt, :] * convw_ref[0:1, :]
    for j in range(1, DN_CONV):
        conv = conv + e_ref[:, base + j:base + j + tt, :] * convw_ref[j:j + 1, :]
    tail = e_ref[:, tt + base:tt + SUBLANES, :]
    cnew_ref[...] = tail
    e_ref[:, base:SUBLANES, :] = tail
    rq = proj(1)
    qkv = _silu(conv).reshape(rows, 3 * QK)
    rk = proj(2)

    beta_all = jax.nn.sigmoid(pba)
    g_all = -jnp.exp(gpar_ref[0:1, :]) * _softplus(pba + gpar_ref[1:2, :])
    g_hi = g_all.astype(BF16)
    g_r1 = g_all - g_hi.astype(F32)
    g_mid = g_r1.astype(BF16)
    g_lo = (g_r1 - g_mid.astype(F32)).astype(BF16)
    g_split = jnp.concatenate([t.reshape(nblk, BLK, LANES) for t in (g_hi, g_mid, g_lo)], axis=1)
    sums = _bmm(jnp.broadcast_to(tri_ref[...][None], (nblk, 2 * BLK, 3 * BLK)), g_split)
    gc_all = sums[:, :BLK].reshape(rows, LANES)
    gl_all = sums[:, BLK:].reshape(rows, LANES)
    gct_all = gc_all.T

    ri = lax.broadcasted_iota(jnp.int32, (BLK, BLK), 0)
    ci = lax.broadcasted_iota(jnp.int32, (BLK, BLK), 1)
    same = (ri >> shift) == (ci >> shift)
    causal = same & (ci <= ri)
    strict = same & (ci < ri)
    eye = (ri == ci).astype(F32)

    cos = cos_ref[0]
    sin = sin_ref[0]
    even = (lax.broadcasted_iota(jnp.int32, (rows, LANES), 1) & 1) == 0

    def rotary(t):
        swapped = jnp.where(even, pltpu.roll(t, LANES - 1, 1), pltpu.roll(t, 1, 1))
        return t * cos + swapped * sin

    per = {k: [] for k in ("qb", "kb", "kbeta", "rhs", "qd", "kt", "gcc", "gct", "egl",
                           "rq", "rk", "rqd", "rkt", "rv")}
    blocks = lambda t: t.reshape(nblk, BLK, t.shape[-1])
    for h in range(N_HEADS):
        qh = qkv[:, h * HEAD_DIM:(h + 1) * HEAD_DIM]
        kh = qkv[:, QK + h * HEAD_DIM:QK + (h + 1) * HEAD_DIM]
        vh = qkv[:, 2 * QK + h * HEAD_DIM:2 * QK + (h + 1) * HEAD_DIM]
        qb = qh * lax.rsqrt(jnp.sum(qh * qh, axis=-1, keepdims=True) + EPS) * (HEAD_DIM ** -0.5)
        kb = kh * lax.rsqrt(jnp.sum(kh * kh, axis=-1, keepdims=True) + EPS)
        bcol = _lane_bcast(beta_all, h)
        gcc = _lane_bcast(gc_all, GATE_LANE0 + h)
        glc = _lane_bcast(gl_all, GATE_LANE0 + h)
        egc = jnp.exp(gcc)
        kbeta = kb * bcol
        per["qb"].append(blocks(qb))
        per["kb"].append(blocks(kb))
        per["kbeta"].append(blocks(kbeta))
        per["rhs"].append(blocks(jnp.concatenate([kbeta * egc, vh * bcol], axis=1)))
        per["qd"].append(blocks(qb * egc))
        per["kt"].append(blocks(kb * jnp.exp(glc - gcc)))
        per["gcc"].append(blocks(gcc))
        per["egl"].append(jnp.exp(glc).reshape(rows // chunk, chunk, HEAD_DIM)[:, 0:1, :])
        per["gct"].extend(gct_all[GATE_LANE0 + h:GATE_LANE0 + h + 1, b * BLK:(b + 1) * BLK][None]
                          for b in range(nblk))
    bat = {k: jnp.concatenate(v, axis=0) for k, v in per.items() if v}
    bf = lambda k: bat[k].astype(BF16)
    rv = proj(3)

    delta = bat["gcc"][:, :, :BLK] - bat["gct"]
    decay = jnp.exp(jnp.where(causal, delta, -jnp.inf))
    kb16 = bf("kb")
    kq = _bmm_nt(jnp.concatenate([bat["kbeta"], bat["qb"]], axis=1).astype(BF16), kb16)
    amat = jnp.where(strict, kq[:, :BLK] * decay, 0.0)
    tinv = _unit_lower_inverse_packed(amat, eye, n_steps)
    z = proj(0)
    for h in range(N_HEADS):
        hs = slice(h * HEAD_DIM, (h + 1) * HEAD_DIM)
        rqh = rotary(rq[:, hs])
        rkh = rotary(rk[:, hs]) * (HEAD_DIM ** -0.5)
        per["rq"].append(blocks(rqh))
        per["rk"].append(blocks(rkh))
        per["rqd"].append(blocks(rqh * rscale_ref[0, :, hs]))
        per["rkt"].append(blocks(rkh * rscale_ref[1, :, hs]))
        per["rv"].append(blocks(rv[:, hs]))
    bat.update({k: jnp.concatenate(per[k], axis=0) for k in ("rq", "rk", "rqd", "rkt", "rv")})
    rg = proj(4)
    wu = _bmm(_pack_lhs(tinv), _pack_rhs(bat["rhs"]))
    qk = (kq[:, BLK:] * decay).astype(BF16)
    rv16 = bf("rv")
    dmat = jnp.broadcast_to(dmat_ref[...][:, None], (N_HEADS, nblk, BLK, BLK)).reshape(n_bat, BLK, BLK)
    rqk = (_bmm_nt(bf("rq"), bf("rk")) * dmat).astype(BF16)

    if chained:
        by_head = lambda t: t.reshape((N_HEADS, nblk) + t.shape[1:])
        transposed16 = lambda k: jnp.swapaxes(bat[k], 1, 2).astype(BF16)
        qg_wu = _bmm(jnp.concatenate([qk, transposed16("kt")], axis=1), wu.astype(BF16))
        qo = qg_wu[:, :BLK]
        gh = by_head(qg_wu[:, BLK:])
        qp = by_head((bat["qd"] - qo[:, :, :HEAD_DIM]).astype(BF16))
        op = by_head(qo[:, :, HEAD_DIM:])
        egl = by_head(bat["egl"])
        ro_rh = _bmm(jnp.concatenate([rqk, transposed16("rkt")], axis=1), rv16)
        ro = by_head(ro_rh[:, :BLK])
        rh = by_head(ro_rh[:, BLK:])
        rqd = by_head(bf("rqd"))
        s_cur = s_ref[0]
        r_cur = r_ref[0]
        o_dn = []
        o_ret = []
        for c in range(nblk):
            s16 = s_cur.astype(BF16)
            qg = jnp.concatenate([qp[:, c], gh[:, c, :, :HEAD_DIM].astype(BF16)], axis=1)
            qgs = _bmm(qg, s16)
            o_dn.append(op[:, c] + qgs[:, :BLK])
            s_cur = s_cur * egl[:, c] - qgs[:, BLK:] + gh[:, c, :, HEAD_DIM:]
            o_ret.append(ro[:, c] + _bmm(rqd[:, c], r_cur.astype(BF16)))
            r_cur = r_cur * cdec_ref[...] + rh[:, c]
        s_ref[0] = s_cur
        r_ref[0] = r_cur
        head_rows = lambda parts, h: jnp.concatenate([p[h] for p in parts], axis=0)
        o_dn = [head_rows(o_dn, h) for h in range(N_HEADS)]
        o_ret = [head_rows(o_ret, h) for h in range(N_HEADS)]
    else:
        n_seq = N_HEADS * sb
        seqs = lambda t: t.reshape(n_seq, tt, t.shape[-1])
        s_old = jnp.concatenate([s0_ref[:, h] for h in range(N_HEADS)], axis=0)
        lhs = jnp.concatenate([seqs(wu[:, :, :HEAD_DIM]), seqs(bat["qd"])], axis=1).astype(BF16)
        res = _bmm(lhs, s_old.astype(BF16))
        v_new = seqs(wu[:, :, HEAD_DIM:]) - res[:, :tt]
        v16 = v_new.astype(BF16)
        s_new = s_old * bat["egl"] + _bmm_tn(seqs(bat["kt"]).astype(BF16), v16)
        o = res[:, tt:].reshape(n_bat, BLK, HEAD_DIM) + _bmm(qk, v16.reshape(n_bat, BLK, HEAD_DIM))
        r_old = jnp.concatenate([r0_ref[:, h] for h in range(N_HEADS)], axis=0)
        ro = _bmm(rqk, rv16) + _bmm(seqs(bat["rqd"]).astype(BF16),
                                    r_old.astype(BF16)).reshape(n_bat, BLK, HEAD_DIM)
        cdec = jnp.broadcast_to(cdec_ref[...][:, None], (N_HEADS, sb, 1, HEAD_DIM))
        r_new = (r_old * cdec.reshape(n_seq, 1, HEAD_DIM)
                 + _bmm_tn(seqs(bat["rkt"]).astype(BF16), seqs(bat["rv"]).astype(BF16)))
        for h in range(N_HEADS):
            s_ref[:, h] = s_new[h * sb:(h + 1) * sb]
            r_ref[:, h] = r_new[h * sb:(h + 1) * sb]
        o_dn = [o[h * nblk:(h + 1) * nblk].reshape(rows, HEAD_DIM) for h in range(N_HEADS)]
        o_ret = [ro[h * nblk:(h + 1) * nblk].reshape(rows, HEAD_DIM) for h in range(N_HEADS)]

    outs = []
    for h in range(N_HEADS):
        hs = slice(h * HEAD_DIM, (h + 1) * HEAD_DIM)
        outs.append(_rmsnorm(o_dn[h], dnw_ref[...]) * _silu(z[:, hs]))
    for h in range(N_HEADS):
        hs = slice(h * HEAD_DIM, (h + 1) * HEAD_DIM)
        o = o_ret[h]
        mu = jnp.mean(o, axis=-1, keepdims=True)
        var = jnp.mean(jnp.square(o - mu), axis=-1, keepdims=True)
        outs.append((o - mu) * lax.rsqrt(var + EPS) * retw_ref[:, hs] * _silu(rg[:, hs]))
    mix = jnp.concatenate(outs, axis=1).astype(BF16)
    h_new = x + jnp.dot(mix, wout_ref[...], preferred_element_type=F32)
    h_ref[...] = h_new.reshape(sb, tt, D_MODEL)


def _ffn_kernel(cfg, h_ref, p_ref, fst_ref, fnw_ref, wup_ref, fcw_ref, fcb_ref, wdown_ref,
                pnw_ref, wgate_ref, wple_ref, finw_ref,
                y_ref, fnew_ref, e_ref):
    sb, tt, final_norm = cfg
    rows = sb * tt
    l_idx = pl.program_id(1)
    base = SUBLANES - (FFN_CONV - 1)

    @pl.when(l_idx == 0)
    def _():
        e_ref[:, base:SUBLANES, :] = fst_ref[...]

    h = h_ref[...].reshape(rows, D_MODEL)
    m = _rmsnorm(h, fnw_ref[...]).astype(BF16)
    for half in (0, D_FF):
        for c0, cw in FF_SLABS:
            cs = slice(half + c0, half + c0 + cw)
            e_ref[:, SUBLANES:, cs] = jnp.dot(m, wup_ref[:, cs],
                                              preferred_element_type=F32).reshape(sb, tt, cw)

    def conv(cs):
        acc = e_ref[:, base:base + tt, cs] * fcw_ref[0:1, cs]
        for j in range(1, FFN_CONV):
            acc = acc + e_ref[:, base + j:base + j + tt, cs] * fcw_ref[j:j + 1, cs]
        return (acc + fcb_ref[:, cs]).reshape(rows, cs.stop - cs.start)

    acc = h
    for c0, cw in FF_SLABS:
        ug = conv(slice(c0, c0 + cw))
        uv = conv(slice(D_FF + c0, D_FF + c0 + cw))
        act = (_silu(ug) * uv).astype(BF16)
        acc = acc + jnp.dot(act, wdown_ref[c0:c0 + cw, :], preferred_element_type=F32)
    tail = e_ref[:, tt + base:tt + SUBLANES, :]
    fnew_ref[...] = tail
    e_ref[:, base:SUBLANES, :] = tail

    h = acc
    gate = jax.nn.sigmoid(jnp.dot(_rmsnorm(h, pnw_ref[...]).astype(BF16), wgate_ref[...],
                                  preferred_element_type=F32))
    pp = jnp.dot(p_ref[...].reshape(rows, PLE_DIM).astype(BF16), wple_ref[...],
                 preferred_element_type=F32)
    h = h + gate * pp
    if final_norm:
        h = _rmsnorm(h, finw_ref[...])
    y_ref[...] = h.reshape(sb, tt, D_MODEL)


def _const_spec(shape):
    zeros = (0,) * len(shape)
    return pl.BlockSpec(shape, lambda b, l: zeros, pipeline_mode=pl.Buffered(1))


def _retention_tables(chunk, rows):
    hh = np.arange(N_HEADS, dtype=np.float64)
    log_gamma = np.log(1.0 - 2.0 ** (-5.0 - hh))
    bpos = (np.arange(chunk, dtype=np.float64) + 1.0)[None, :] * log_gamma[:, None]
    causal = np.tril(np.ones((chunk, chunk), dtype=bool))
    dmat = np.where(causal, np.exp(np.where(causal, bpos[:, :, None] - bpos[:, None, :], 0.0)), 0.0)
    n = BLK // chunk
    dblk = np.einsum('ab,hij->haibj', np.eye(n), dmat).reshape(N_HEADS, BLK, BLK)
    qscale = np.exp(bpos)
    kscale = np.exp(bpos[:, -1:] - bpos)
    expand = lambda t: np.repeat(np.tile(t.T, (rows // chunk, 1)), HEAD_DIM, axis=1)
    rscale = np.stack([expand(qscale), expand(kscale)])
    cdec = np.broadcast_to(np.exp(bpos[:, -1]).reshape(N_HEADS, 1, 1), (N_HEADS, 1, HEAD_DIM))
    return tuple(jnp.asarray(t, F32) for t in (dblk, rscale, cdec))


def _chunk_sum_matrix(chunk):
    r = np.arange(BLK)
    same = (r[:, None] // chunk) == (r[None, :] // chunk)
    causal = same & (r[None, :] <= r[:, None])
    return jnp.asarray(np.tile(np.concatenate([causal, same], axis=0), (1, 3)), BF16)


def _rope_tables(pos0, tt, sb, n_l):
    inv = ROPE_BASE ** (-np.arange(0, HEAD_DIM, 2, dtype=np.float64) / HEAD_DIM)
    ang = (pos0 + np.arange(n_l * tt, dtype=np.float64))[:, None] * inv[None, :]
    cos = np.repeat(np.cos(ang), 2, axis=1)
    sin = np.stack([-np.sin(ang), np.sin(ang)], axis=-1).reshape(n_l * tt, HEAD_DIM)
    tile = lambda t: jnp.asarray(
        np.tile(t.reshape(n_l, 1, tt, HEAD_DIM), (1, sb, 1, 1)).reshape(n_l, sb * tt, HEAD_DIM), F32)
    return tile(cos), tile(sin)


def _mixer(x, cstate, s0, r0, pos0, wts, sb, tt):
    n_b, seq, _ = x.shape
    chunk = CHUNK if seq % CHUNK == 0 else seq
    chained = chunk == BLK
    rows = sb * tt
    n_l = seq // tt
    assert n_b % sb == 0 and seq % tt == 0 and rows % BLK == 0 and BLK % chunk == 0
    assert (sb == 1 and tt % chunk == 0) if chained else (tt == seq == chunk)
    cos, sin = _rope_tables(pos0, tt, sb, n_l)
    dmat, rscale, cdec = _retention_tables(chunk, rows)
    tri = _chunk_sum_matrix(chunk)
    state_spec = pl.BlockSpec((sb, N_HEADS, HEAD_DIM, HEAD_DIM), lambda b, l: (b, 0, 0, 0))
    x_spec = pl.BlockSpec((sb, tt, D_MODEL), lambda b, l: (b, l, 0))
    conv_spec = pl.BlockSpec((sb, DN_CONV - 1, 3 * QK), lambda b, l: (b, 0, 0))
    rope_spec = pl.BlockSpec((1, rows, HEAD_DIM), lambda b, l: (l, 0, 0))
    consts = (tri, dmat, rscale, cdec) + tuple(wts)
    return pl.pallas_call(
        functools.partial(_mixer_kernel, (sb, tt, chunk, chained)),
        grid=(n_b // sb, n_l),
        in_specs=[x_spec, conv_spec, state_spec, state_spec, rope_spec, rope_spec]
        + [_const_spec(c.shape) for c in consts],
        out_specs=[x_spec, conv_spec, state_spec, state_spec],
        out_shape=[jax.ShapeDtypeStruct(x.shape, F32),
                   jax.ShapeDtypeStruct(cstate.shape, F32),
                   jax.ShapeDtypeStruct(s0.shape, F32),
                   jax.ShapeDtypeStruct(r0.shape, F32)],
        scratch_shapes=[pltpu.VMEM((sb, SUBLANES + tt, 3 * QK), F32)],
        compiler_params=pltpu.CompilerParams(
            dimension_semantics=("arbitrary", "arbitrary"), vmem_limit_bytes=VMEM_LIMIT),
        name="mixer",
    )(x, cstate, s0, r0, cos, sin, *consts)


def _ffn(h, p, fstate, wts, final_norm, sb, tt):
    n_b, seq, _ = h.shape
    n_l = seq // tt
    assert n_b % sb == 0 and seq % tt == 0
    x_spec = pl.BlockSpec((sb, tt, D_MODEL), lambda b, l: (b, l, 0))
    p_spec = pl.BlockSpec((sb, tt, PLE_DIM), lambda b, l: (b, l, 0))
    st_spec = pl.BlockSpec((sb, FFN_CONV - 1, 2 * D_FF), lambda b, l: (b, 0, 0))
    return pl.pallas_call(
        functools.partial(_ffn_kernel, (sb, tt, final_norm)),
        grid=(n_b // sb, n_l),
        in_specs=[x_spec, p_spec, st_spec] + [_const_spec(w.shape) for w in wts],
        out_specs=[x_spec, st_spec],
        out_shape=[jax.ShapeDtypeStruct(h.shape, F32), jax.ShapeDtypeStruct(fstate.shape, F32)],
        scratch_shapes=[pltpu.VMEM((sb, SUBLANES + tt, 2 * D_FF), F32)],
        compiler_params=pltpu.CompilerParams(
            dimension_semantics=("arbitrary", "arbitrary"), vmem_limit_bytes=VMEM_LIMIT),
        name="ffn",
    )(h, p, fstate, *wts)


def kernel(x_prompt, x_sample, p_prompt, p_sample, state_dn_conv, state_dn, state_ret, state_ffn_conv, attn_norm_w, w_in, dn_conv_w, dn_A_log, dn_dt_bias, dn_norm_w, ret_norm_w, w_out, ffn_norm_w, w_up, ffn_conv_w, ffn_conv_b, w_down, ple_norm_w, w_ple_gate, w_ple, final_norm_w):
    depth = w_in.shape[0]
    n_bp = x_prompt.shape[0]
    dec_len = x_sample.shape[1]
    row = lambda v: v.reshape(1, -1).astype(F32)
    hp, hs = x_prompt, x_sample
    outs_p = ([], [], [], [])
    outs_s = ([], [], [], [])
    for i in range(depth):
        wi = w_in[i]
        wdn = wi[:, :4 * QK].astype(BF16)
        wret = wi[:, 4 * QK + 2 * N_HEADS:].astype(BF16)
        wba = jnp.pad(wi[:, 4 * QK:4 * QK + 2 * N_HEADS].astype(BF16),
                      ((0, 0), (0, LANES - 2 * N_HEADS)))
        gpar = jnp.zeros((2, LANES), F32)
        gpar = gpar.at[0, GATE_LANE0:GATE_LANE0 + N_HEADS].set(dn_A_log[i])
        gpar = gpar.at[1, GATE_LANE0:GATE_LANE0 + N_HEADS].set(dn_dt_bias[i])
        mixer_w = (row(attn_norm_w[i]), wdn, wret, wba, dn_conv_w[i], gpar, row(dn_norm_w[i]),
                   row(ret_norm_w[i]), w_out[i].astype(BF16))
        ffn_w = (row(ffn_norm_w[i]), w_up[i].astype(BF16), ffn_conv_w[i], row(ffn_conv_b[i]),
                 w_down[i].astype(BF16), row(ple_norm_w[i]), w_ple_gate[i].astype(BF16),
                 w_ple[i].astype(BF16), row(final_norm_w))
        last = i == depth - 1
        zc = jnp.zeros((n_bp, DN_CONV - 1, 3 * QK), F32)
        zs = jnp.zeros((n_bp, N_HEADS, HEAD_DIM, HEAD_DIM), F32)
        zf = jnp.zeros((n_bp, FFN_CONV - 1, 2 * D_FF), F32)
        hp, c1, s1, r1 = _mixer(hp, zc, zs, zs, 0, mixer_w, *PROMPT_MIXER_TILE)
        hp, f1 = _ffn(hp, p_prompt[i], zf, ffn_w, last, *PROMPT_FFN_TILE)
        hs, c2, s2, r2 = _mixer(hs, state_dn_conv[i], state_dn[i], state_ret[i], PAST_LEN,
                                mixer_w, SAMPLE_MIXER_SEQS, dec_len)
        hs, f2 = _ffn(hs, p_sample[i], state_ffn_conv[i], ffn_w, last, SAMPLE_FFN_SEQS, dec_len)
        for lst, val in zip(outs_p, (c1, s1, r1, f1)):
            lst.append(val)
        for lst, val in zip(outs_s, (c2, s2, r2, f2)):
            lst.append(val)
    return (hp, hs,
            jnp.stack(outs_p[0]), jnp.stack(outs_p[1]), jnp.stack(outs_p[2]), jnp.stack(outs_p[3]),
            jnp.stack(outs_s[0]), jnp.stack(outs_s[1]), jnp.stack(outs_s[2]), jnp.stack(outs_s[3]))
```

```python
import functools
import math

import jax
import jax.numpy as jnp
import numpy as np
from jax import lax
from jax.experimental import pallas as pl
from jax.experimental.pallas import tpu as pltpu

F32 = jnp.float32
BF16 = jnp.bfloat16

D_MODEL = 1024
HEAD_DIM = 128
N_HEADS = 4
QK = N_HEADS * HEAD_DIM
DN_CONV = 4
FFN_CONV = 3
D_FF = 2816
PLE_DIM = 256
CHUNK = 64
PAST_LEN = 16384
ROPE_BASE = 10000.0
EPS = 1e-6
LANES = 128
SUBLANES = 8
MXU_DIM = 256
BLK = 64
GATE_LANE0 = N_HEADS
FF_SLABS = tuple((c, MXU_DIM) for c in range(0, D_FF, MXU_DIM))
VMEM_LIMIT = 56 * 1024 * 1024
PROMPT_MIXER_TILE = (1, 512)
PROMPT_FFN_TILE = (1, 512)
SAMPLE_MIXER_SEQS = 16
SAMPLE_FFN_SEQS = 32


def _split2(x):
    hi = x.astype(BF16)
    lo = (x - hi.astype(F32)).astype(BF16)
    return hi, lo


def _rmsnorm(x, w):
    return x * lax.rsqrt(jnp.mean(x * x, axis=-1, keepdims=True) + EPS) * w


def _silu(x):
    return x * jax.nn.sigmoid(x)


def _softplus(x):
    return jnp.maximum(x, 0.0) + jnp.log(1.0 + jnp.exp(-jnp.abs(x)))


def _lane_bcast(x, lane, width=LANES):
    return jnp.broadcast_to(x[:, lane:lane + 1], (x.shape[0], width))


def _bmm(a, b):
    return lax.dot_general(a, b, (((2,), (1,)), ((0,), (0,))), preferred_element_type=F32)


def _bmm_nt(a, b):
    return lax.dot_general(a, b, (((2,), (2,)), ((0,), (0,))), preferred_element_type=F32)


def _bmm_tn(a, b):
    return lax.dot_general(a, b, (((1,), (1,)), ((0,), (0,))), preferred_element_type=F32)


def _pack_lhs(m):
    hi = m.astype(BF16).astype(F32)
    return jnp.concatenate([hi, m - hi, hi], axis=-1).astype(BF16)


def _pack_rhs(m):
    hi, lo = _split2(m)
    return jnp.concatenate([hi, hi, lo], axis=-2)


def _unit_lower_inverse_packed(a, eye, n_steps):
    c = a.shape[-1]
    upper = lax.broadcasted_iota(jnp.int32, a.shape[:-1] + (2 * c,), a.ndim - 1) >= c
    a2 = _bmm(_pack_lhs(a), _pack_rhs(a))
    mp = jnp.concatenate([a2, eye - a], axis=-1)
    for _ in range(n_steps):
        res = _bmm(_pack_lhs(mp[..., :c]), _pack_rhs(mp))
        mp = res + jnp.where(upper, mp, 0.0)
    return mp[..., c:]


def _mixer_kernel(cfg, x_ref, cst_ref, s0_ref, r0_ref, cos_ref, sin_ref, tri_ref, dmat_ref,
                  rscale_ref, cdec_ref, anw_ref, wdn_ref, wret_ref, wba_ref, convw_ref, gpar_ref,
                  dnw_ref, retw_ref, wout_ref, h_ref, cnew_ref, s_ref, r_ref, e_ref):
    sb, tt, chunk, chained = cfg
    rows = sb * tt
    nblk = rows // BLK
    n_bat = N_HEADS * nblk
    shift = int(math.log2(chunk))
    n_steps = shift - 1
    base = SUBLANES - (DN_CONV - 1)
    l_idx = pl.program_id(1)

    @pl.when(l_idx == 0)
    def _():
        e_ref[:, base:SUBLANES, :] = cst_ref[...]
        if chained:
            s_ref[...] = s0_ref[...]
            r_ref[...] = r0_ref[...]

    x = x_ref[...].reshape(rows, D_MODEL)
    a = _rmsnorm(x, anw_ref[...]).astype(BF16)
    e_ref[:, SUBLANES:, :] = jnp.dot(a, wdn_ref[:, 0:3 * QK],
                                     preferred_element_type=F32).reshape(sb, tt, 3 * QK)
    slabs = [wdn_ref.at[:, 3 * QK:4 * QK]] + [wret_ref.at[:, i * QK:(i + 1) * QK] for i in range(4)]
    proj = lambda i: jnp.dot(a, slabs[i][...], preferred_element_type=F32)
    pba = jnp.dot(a, wba_ref[...], preferred_element_type=F32)

    conv = e_ref[:, base:base + tt, :] * convw_ref[0:1, :]
    for j in range(1, DN_CONV):
        conv = conv + e_ref[:, base + j:base + j + tt, :] * convw_ref[j:j + 1, :]
    tail = e_ref[:, tt + base:tt + SUBLANES, :]
    cnew_ref[...] = tail
    e_ref[:, base:SUBLANES, :] = tail
    rq = proj(1)
    qkv = _silu(conv).reshape(rows, 3 * QK)
    rk = proj(2)

    beta_all = jax.nn.sigmoid(pba)
    g_all = -jnp.exp(gpar_ref[0:1, :]) * _softplus(pba + gpar_ref[1:2, :])
    g_hi = g_all.astype(BF16)
    g_r1 = g_all - g_hi.astype(F32)
    g_mid = g_r1.astype(BF16)
    g_lo = (g_r1 - g_mid.astype(F32)).astype(BF16)
    g_split = jnp.concatenate([t.reshape(nblk, BLK, LANES) for t in (g_hi, g_mid, g_lo)], axis=1)
    sums = _bmm(jnp.broadcast_to(tri_ref[...][None], (nblk, 2 * BLK, 3 * BLK)), g_split)
    gc_all = sums[:, :BLK].reshape(rows, LANES)
    gl_all = sums[:, BLK:].reshape(rows, LANES)
    gct_all = gc_all.T

    ri = lax.broadcasted_iota(jnp.int32, (BLK, BLK), 0)
    ci = lax.broadcasted_iota(jnp.int32, (BLK, BLK), 1)
    same = (ri >> shift) == (ci >> shift)
    causal = same & (ci <= ri)
    strict = same & (ci < ri)
    eye = (ri == ci).astype(F32)

    cos = cos_ref[0]
    sin = sin_ref[0]
    even = (lax.broadcasted_iota(jnp.int32, (rows, LANES), 1) & 1) == 0

    def rotary(t):
        swapped = jnp.where(even, pltpu.roll(t, LANES - 1, 1), pltpu.roll(t, 1, 1))
        return t * cos + swapped * sin

    per = {k: [] for k in ("qb", "kb", "kbeta", "rhs", "qd", "kt", "gcc", "gct", "egl",
                           "rq", "rk", "rqd", "rkt", "rv")}
    blocks = lambda t: t.reshape(nblk, BLK, t.shape[-1])
    for h in range(N_HEADS):
        qh = qkv[:, h * HEAD_DIM:(h + 1) * HEAD_DIM]
        kh = qkv[:, QK + h * HEAD_DIM:QK + (h + 1) * HEAD_DIM]
        vh = qkv[:, 2 * QK + h * HEAD_DIM:2 * QK + (h + 1) * HEAD_DIM]
        qb = qh * lax.rsqrt(jnp.sum(qh * qh, axis=-1, keepdims=True) + EPS) * (HEAD_DIM ** -0.5)
        kb = kh * lax.rsqrt(jnp.sum(kh * kh, axis=-1, keepdims=True) + EPS)
        bcol = _lane_bcast(beta_all, h)
        gcc = _lane_bcast(gc_all, GATE_LANE0 + h)
        glc = _lane_bcast(gl_all, GATE_LANE0 + h)
        egc = jnp.exp(gcc)
        kbeta = kb * bcol
        per["qb"].append(blocks(qb))
        per["kb"].append(blocks(kb))
        per["kbeta"].append(blocks(kbeta))
        per["rhs"].append(blocks(jnp.concatenate([kbeta * egc, vh * bcol], axis=1)))
        per["qd"].append(blocks(qb * egc))
        per["kt"].append(blocks(kb * jnp.exp(glc - gcc)))
        per["gcc"].append(blocks(gcc))
        per["egl"].append(jnp.exp(glc).reshape(rows // chunk, chunk, HEAD_DIM)[:, 0:1, :])
        per["gct"].extend(gct_all[GATE_LANE0 + h:GATE_LANE0 + h + 1, b * BLK:(b + 1) * BLK][None]
                          for b in range(nblk))
    bat = {k: jnp.concatenate(v, axis=0) for k, v in per.items() if v}
    bf = lambda k: bat[k].astype(BF16)
    rv = proj(3)

    delta = bat["gcc"][:, :, :BLK] - bat["gct"]
    decay = jnp.exp(jnp.where(causal, delta, -jnp.inf))
    kb16 = bf("kb")
    kq = _bmm_nt(jnp.concatenate([bat["kbeta"], bat["qb"]], axis=1).astype(BF16), kb16)
    amat = jnp.where(strict, kq[:, :BLK] * decay, 0.0)
    tinv = _unit_lower_inverse_packed(amat, eye, n_steps)
    z = proj(0)
    for h in range(N_HEADS):
        hs = slice(h * HEAD_DIM, (h + 1) * HEAD_DIM)
        rqh = rotary(rq[:, hs])
        rkh = rotary(rk[:, hs]) * (HEAD_DIM ** -0.5)
        per["rq"].append(blocks(rqh))
        per["rk"].append(blocks(rkh))
        per["rqd"].append(blocks(rqh * rscale_ref[0, :, hs]))
        per["rkt"].append(blocks(rkh * rscale_ref[1, :, hs]))
        per["rv"].append(blocks(rv[:, hs]))
    bat.update({k: jnp.concatenate(per[k], axis=0) for k in ("rq", "rk", "rqd", "rkt", "rv")})
    rg = proj(4)
    wu = _bmm(_pack_lhs(tinv), _pack_rhs(bat["rhs"]))
    qk = (kq[:, BLK:] * decay).astype(BF16)
    rv16 = bf("rv")
    dmat = jnp.broadcast_to(dmat_ref[...][:, None], (N_HEADS, nblk, BLK, BLK)).reshape(n_bat, BLK, BLK)
    rqk = (_bmm_nt(bf("rq"), bf("rk")) * dmat).astype(BF16)

    if chained:
        by_head = lambda t: t.reshape((N_HEADS, nblk) + t.shape[1:])
        transposed16 = lambda k: jnp.swapaxes(bat[k], 1, 2).astype(BF16)
        qg_wu = _bmm(jnp.concatenate([qk, transposed16("kt")], axis=1), wu.astype(BF16))
        qo = qg_wu[:, :BLK]
        gh = by_head(qg_wu[:, BLK:])
        qp = by_head((bat["qd"] - qo[:, :, :HEAD_DIM]).astype(BF16))
        op = by_head(qo[:, :, HEAD_DIM:])
        egl = by_head(bat["egl"])
        ro_rh = _bmm(jnp.concatenate([rqk, transposed16("rkt")], axis=1), rv16)
        ro = by_head(ro_rh[:, :BLK])
        rh = by_head(ro_rh[:, BLK:])
        rqd = by_head(bf("rqd"))
        s_cur = s_ref[0]
        r_cur = r_ref[0]
        o_dn = []
        o_ret = []
        for c in range(nblk):
            s16 = s_cur.astype(BF16)
            qg = jnp.concatenate([qp[:, c], gh[:, c, :, :HEAD_DIM].astype(BF16)], axis=1)
            qgs = _bmm(qg, s16)
            o_dn.append(op[:, c] + qgs[:, :BLK])
            s_cur = s_cur * egl[:, c] - qgs[:, BLK:] + gh[:, c, :, HEAD_DIM:]
            o_ret.append(ro[:, c] + _bmm(rqd[:, c], r_cur.astype(BF16)))
            r_cur = r_cur * cdec_ref[...] + rh[:, c]
        s_ref[0] = s_cur
        r_ref[0] = r_cur
        head_rows = lambda parts, h: jnp.concatenate([p[h] for p in parts], axis=0)
        o_dn = [head_rows(o_dn, h) for h in range(N_HEADS)]
        o_ret = [head_rows(o_ret, h) for h in range(N_HEADS)]
    else:
        n_seq = N_HEADS * sb
        seqs = lambda t: t.reshape(n_seq, tt, t.shape[-1])
        s_old = jnp.concatenate([s0_ref[:, h] for h in range(N_HEADS)], axis=0)
        lhs = jnp.concatenate([seqs(wu[:, :, :HEAD_DIM]), seqs(bat["qd"])], axis=1).astype(BF16)
        res = _bmm(lhs, s_old.astype(BF16))
        v_new = seqs(wu[:, :, HEAD_DIM:]) - res[:, :tt]
        v16 = v_new.astype(BF16)
        s_new = s_old * bat["egl"] + _bmm_tn(seqs(bat["kt"]).astype(BF16), v16)
        o = res[:, tt:].reshape(n_bat, BLK, HEAD_DIM) + _bmm(qk, v16.reshape(n_bat, BLK, HEAD_DIM))
        r_old = jnp.concatenate([r0_ref[:, h] for h in range(N_HEADS)], axis=0)
        ro = _bmm(rqk, rv16) + _bmm(seqs(bat["rqd"]).astype(BF16),
                                    r_old.astype(BF16)).reshape(n_bat, BLK, HEAD_DIM)
        cdec = jnp.broadcast_to(cdec_ref[...][:, None], (N_HEADS, sb, 1, HEAD_DIM))
        r_new = (r_old * cdec.reshape(n_seq, 1, HEAD_DIM)
                 + _bmm_tn(seqs(bat["rkt"]).astype(BF16), seqs(bat["rv"]).astype(BF16)))
        for h in range(N_HEADS):
            s_ref[:, h] = s_new[h * sb:(h + 1) * sb]
            r_ref[:, h] = r_new[h * sb:(h + 1) * sb]
        o_dn = [o[h * nblk:(h + 1) * nblk].reshape(rows, HEAD_DIM) for h in range(N_HEADS)]
        o_ret = [ro[h * nblk:(h + 1) * nblk].reshape(rows, HEAD_DIM) for h in range(N_HEADS)]

    outs = []
    for h in range(N_HEADS):
        hs = slice(h * HEAD_DIM, (h + 1) * HEAD_DIM)
        outs.append(_rmsnorm(o_dn[h], dnw_ref[...]) * _silu(z[:, hs]))
    for h in range(N_HEADS):
        hs = slice(h * HEAD_DIM, (h + 1) * HEAD_DIM)
        o = o_ret[h]
        mu = jnp.mean(o, axis=-1, keepdims=True)
        var = jnp.mean(jnp.square(o - mu), axis=-1, keepdims=True)
        outs.append((o - mu) * lax.rsqrt(var + EPS) * retw_ref[:, hs] * _silu(rg[:, hs]))
    mix = jnp.concatenate(outs, axis=1).astype(BF16)
    h_new = x + jnp.dot(mix, wout_ref[...], preferred_element_type=F32)
    h_ref[...] = h_new.reshape(sb, tt, D_MODEL)


def _ffn_kernel(cfg, h_ref, p_ref, fst_ref, fnw_ref, wup_ref, fcw_ref, fcb_ref, wdown_ref,
                pnw_ref, wgate_ref, wple_ref, finw_ref,
                y_ref, fnew_ref, e_ref):
    sb, tt, final_norm = cfg
    rows = sb * tt
    l_idx = pl.program_id(1)
    base = SUBLANES - (FFN_CONV - 1)

    @pl.when(l_idx == 0)
    def _():
        e_ref[:, base:SUBLANES, :] = fst_ref[...]

    h = h_ref[...].reshape(rows, D_MODEL)
    m = _rmsnorm(h, fnw_ref[...]).astype(BF16)
    for half in (0, D_FF):
        for c0, cw in FF_SLABS:
            cs = slice(half + c0, half + c0 + cw)
            e_ref[:, SUBLANES:, cs] = jnp.dot(m, wup_ref[:, cs],
                                              preferred_element_type=F32).reshape(sb, tt, cw)

    def conv(cs):
        acc = e_ref[:, base:base + tt, cs] * fcw_ref[0:1, cs]
        for j in range(1, FFN_CONV):
            acc = acc + e_ref[:, base + j:base + j + tt, cs] * fcw_ref[j:j + 1, cs]
        return (acc + fcb_ref[:, cs]).reshape(rows, cs.stop - cs.start)

    acc = h
    for c0, cw in FF_SLABS:
        ug = conv(slice(c0, c0 + cw))
        uv = conv(slice(D_FF + c0, D_FF + c0 + cw))
        act = (_silu(ug) * uv).astype(BF16)
        acc = acc + jnp.dot(act, wdown_ref[c0:c0 + cw, :], preferred_element_type=F32)
    tail = e_ref[:, tt + base:tt + SUBLANES, :]
    fnew_ref[...] = tail
    e_ref[:, base:SUBLANES, :] = tail

    h = acc
    gate = jax.nn.sigmoid(jnp.dot(_rmsnorm(h, pnw_ref[...]).astype(BF16), wgate_ref[...],
                                  preferred_element_type=F32))
    pp = jnp.dot(p_ref[...].reshape(rows, PLE_DIM).astype(BF16), wple_ref[...],
                 preferred_element_type=F32)
    h = h + gate * pp
    if final_norm:
        h = _rmsnorm(h, finw_ref[...])
    y_ref[...] = h.reshape(sb, tt, D_MODEL)


def _const_spec(shape):
    zeros = (0,) * len(shape)
    return pl.BlockSpec(shape, lambda b, l: zeros, pipeline_mode=pl.Buffered(1))


def _retention_tables(chunk, rows):
    hh = np.arange(N_HEADS, dtype=np.float64)
    log_gamma = np.log(1.0 - 2.0 ** (-5.0 - hh))
    bpos = (np.arange(chunk, dtype=np.float64) + 1.0)[None, :] * log_gamma[:, None]
    causal = np.tril(np.ones((chunk, chunk), dtype=bool))
    dmat = np.where(causal, np.exp(np.where(causal, bpos[:, :, None] - bpos[:, None, :], 0.0)), 0.0)
    n = BLK // chunk
    dblk = np.einsum('ab,hij->haibj', np.eye(n), dmat).reshape(N_HEADS, BLK, BLK)
    qscale = np.exp(bpos)
    kscale = np.exp(bpos[:, -1:] - bpos)
    expand = lambda t: np.repeat(np.tile(t.T, (rows // chunk, 1)), HEAD_DIM, axis=1)
    rscale = np.stack([expand(qscale), expand(kscale)])
    cdec = np.broadcast_to(np.exp(bpos[:, -1]).reshape(N_HEADS, 1, 1), (N_HEADS, 1, HEAD_DIM))
    return tuple(jnp.asarray(t, F32) for t in (dblk, rscale, cdec))


def _chunk_sum_matrix(chunk):
    r = np.arange(BLK)
    same = (r[:, None] // chunk) == (r[None, :] // chunk)
    causal = same & (r[None, :] <= r[:, None])
    return jnp.asarray(np.tile(np.concatenate([causal, same], axis=0), (1, 3)), BF16)


def _rope_tables(pos0, tt, sb, n_l):
    inv = ROPE_BASE ** (-np.arange(0, HEAD_DIM, 2, dtype=np.float64) / HEAD_DIM)
    ang = (pos0 + np.arange(n_l * tt, dtype=np.float64))[:, None] * inv[None, :]
    cos = np.repeat(np.cos(ang), 2, axis=1)
    sin = np.stack([-np.sin(ang), np.sin(ang)], axis=-1).reshape(n_l * tt, HEAD_DIM)
    tile = lambda t: jnp.asarray(
        np.tile(t.reshape(n_l, 1, tt, HEAD_DIM), (1, sb, 1, 1)).reshape(n_l, sb * tt, HEAD_DIM), F32)
    return tile(cos), tile(sin)


def _mixer(x, cstate, s0, r0, pos0, wts, sb, tt):
    n_b, seq, _ = x.shape
    chunk = CHUNK if seq % CHUNK == 0 else seq
    chained = chunk == BLK
    rows = sb * tt
    n_l = seq // tt
    assert n_b % sb == 0 and seq % tt == 0 and rows % BLK == 0 and BLK % chunk == 0
    assert (sb == 1 and tt % chunk == 0) if chained else (tt == seq == chunk)
    cos, sin = _rope_tables(pos0, tt, sb, n_l)
    dmat, rscale, cdec = _retention_tables(chunk, rows)
    tri = _chunk_sum_matrix(chunk)
    state_spec = pl.BlockSpec((sb, N_HEADS, HEAD_DIM, HEAD_DIM), lambda b, l: (b, 0, 0, 0))
    x_spec = pl.BlockSpec((sb, tt, D_MODEL), lambda b, l: (b, l, 0))
    conv_spec = pl.BlockSpec((sb, DN_CONV - 1, 3 * QK), lambda b, l: (b, 0, 0))
    rope_spec = pl.BlockSpec((1, rows, HEAD_DIM), lambda b, l: (l, 0, 0))
    consts = (tri, dmat, rscale, cdec) + tuple(wts)
    return pl.pallas_call(
        functools.partial(_mixer_kernel, (sb, tt, chunk, chained)),
        grid=(n_b // sb, n_l),
        in_specs=[x_spec, conv_spec, state_spec, state_spec, rope_spec, rope_spec]
        + [_const_spec(c.shape) for c in consts],
        out_specs=[x_spec, conv_spec, state_spec, state_spec],
        out_shape=[jax.ShapeDtypeStruct(x.shape, F32),
                   jax.ShapeDtypeStruct(cstate.shape, F32),
                   jax.ShapeDtypeStruct(s0.shape, F32),
                   jax.ShapeDtypeStruct(r0.shape, F32)],
        scratch_shapes=[pltpu.VMEM((sb, SUBLANES + tt, 3 * QK), F32)],
        compiler_params=pltpu.CompilerParams(
            dimension_semantics=("arbitrary", "arbitrary"), vmem_limit_bytes=VMEM_LIMIT),
        name="mixer",
    )(x, cstate, s0, r0, cos, sin, *consts)


def _ffn(h, p, fstate, wts, final_norm, sb, tt):
    n_b, seq, _ = h.shape
    n_l = seq // tt
    assert n_b % sb == 0 and seq % tt == 0
    x_spec = pl.BlockSpec((sb, tt, D_MODEL), lambda b, l: (b, l, 0))
    p_spec = pl.BlockSpec((sb, tt, PLE_DIM), lambda b, l: (b, l, 0))
    st_spec = pl.BlockSpec((sb, FFN_CONV - 1, 2 * D_FF), lambda b, l: (b, 0, 0))
    return pl.pallas_call(
        functools.partial(_ffn_kernel, (sb, tt, final_norm)),
        grid=(n_b // sb, n_l),
        in_specs=[x_spec, p_spec, st_spec] + [_const_spec(w.shape) for w in wts],
        out_specs=[x_spec, st_spec],
        out_shape=[jax.ShapeDtypeStruct(h.shape, F32), jax.ShapeDtypeStruct(fstate.shape, F32)],
        scratch_shapes=[pltpu.VMEM((sb, SUBLANES + tt, 2 * D_FF), F32)],
        compiler_params=pltpu.CompilerParams(
            dimension_semantics=("arbitrary", "arbitrary"), vmem_limit_bytes=VMEM_LIMIT),
        name="ffn",
    )(h, p, fstate, *wts)


def kernel(x_prompt, x_sample, p_prompt, p_sample, state_dn_conv, state_dn, state_ret, state_ffn_conv, attn_norm_w, w_in, dn_conv_w, dn_A_log, dn_dt_bias, dn_norm_w, ret_norm_w, w_out, ffn_norm_w, w_up, ffn_conv_w, ffn_conv_b, w_down, ple_norm_w, w_ple_gate, w_ple, final_norm_w):
    depth = w_in.shape[0]
    n_bp = x_prompt.shape[0]
    dec_len = x_sample.shape[1]
    row = lambda v: v.reshape(1, -1).astype(F32)
    hp, hs = x_prompt, x_sample
    outs_p = ([], [], [], [])
    outs_s = ([], [], [], [])
    for i in range(depth):
        wi = w_in[i]
        wdn = wi[:, :4 * QK].astype(BF16)
        wret = wi[:, 4 * QK + 2 * N_HEADS:].astype(BF16)
        wba = jnp.pad(wi[:, 4 * QK:4 * QK + 2 * N_HEADS].astype(BF16),
                      ((0, 0), (0, LANES - 2 * N_HEADS)))
        gpar = jnp.zeros((2, LANES), F32)
        gpar = gpar.at[0, GATE_LANE0:GATE_LANE0 + N_HEADS].set(dn_A_log[i])
        gpar = gpar.at[1, GATE_LANE0:GATE_LANE0 + N_HEADS].set(dn_dt_bias[i])
        mixer_w = (row(attn_norm_w[i]), wdn, wret, wba, dn_conv_w[i], gpar, row(dn_norm_w[i]),
                   row(ret_norm_w[i]), w_out[i].astype(BF16))
        ffn_w = (row(ffn_norm_w[i]), w_up[i].astype(BF16), ffn_conv_w[i], row(ffn_conv_b[i]),
                 w_down[i].astype(BF16), row(ple_norm_w[i]), w_ple_gate[i].astype(BF16),
                 w_ple[i].astype(BF16), row(final_norm_w))
        last = i == depth - 1
        zc = jnp.zeros((n_bp, DN_CONV - 1, 3 * QK), F32)
        zs = jnp.zeros((n_bp, N_HEADS, HEAD_DIM, HEAD_DIM), F32)
        zf = jnp.zeros((n_bp, FFN_CONV - 1, 2 * D_FF), F32)
        hp, c1, s1, r1 = _mixer(hp, zc, zs, zs, 0, mixer_w, *PROMPT_MIXER_TILE)
        hp, f1 = _ffn(hp, p_prompt[i], zf, ffn_w, last, *PROMPT_FFN_TILE)
        hs, c2, s2, r2 = _mixer(hs, state_dn_conv[i], state_dn[i], state_ret[i], PAST_LEN,
                                mixer_w, SAMPLE_MIXER_SEQS, dec_len)
        hs, f2 = _ffn(hs, p_sample[i], state_ffn_conv[i], ffn_w, last, SAMPLE_FFN_SEQS, dec_len)
        for lst, val in zip(outs_p, (c1, s1, r1, f1)):
            lst.append(val)
        for lst, val in zip(outs_s, (c2, s2, r2, f2)):
            lst.append(val)
    return (hp, hs,
            jnp.stack(outs_p[0]), jnp.stack(outs_p[1]), jnp.stack(outs_p[2]), jnp.stack(outs_p[3]),
            jnp.stack(outs_s[0]), jnp.stack(outs_s[1]), jnp.stack(outs_s[2]), jnp.stack(outs_s[3]))
```

```python
import functools
import math

import jax
import jax.numpy as jnp
import numpy as np
from jax import lax
from jax.experimental import pallas as pl
from jax.experimental.pallas import tpu as pltpu

F32 = jnp.float32
BF16 = jnp.bfloat16

D_MODEL = 1024
HEAD_DIM = 128
N_HEADS = 4
QK = N_HEADS * HEAD_DIM
DN_CONV = 4
FFN_CONV = 3
D_FF = 2816
PLE_DIM = 256
CHUNK = 64
PAST_LEN = 16384
ROPE_BASE = 10000.0
EPS = 1e-6
LANES = 128
SUBLANES = 8
MXU_DIM = 256
BLK = 64
GATE_LANE0 = N_HEADS
FF_SLABS = tuple((c, MXU_DIM) for c in range(0, D_FF, MXU_DIM))
VMEM_LIMIT = 56 * 1024 * 1024
PROMPT_MIXER_TILE = (1, 512)
PROMPT_FFN_TILE = (1, 512)
SAMPLE_MIXER_SEQS = 16
SAMPLE_FFN_SEQS = 32


def _split2(x):
    hi = x.astype(BF16)
    lo = (x - hi.astype(F32)).astype(BF16)
    return hi, lo


def _rmsnorm(x, w):
    return x * lax.rsqrt(jnp.mean(x * x, axis=-1, keepdims=True) + EPS) * w


def _silu(x):
    return x * jax.nn.sigmoid(x)


def _softplus(x):
    return jnp.maximum(x, 0.0) + jnp.log(1.0 + jnp.exp(-jnp.abs(x)))


def _lane_bcast(x, lane, width=LANES):
    return jnp.broadcast_to(x[:, lane:lane + 1], (x.shape[0], width))


def _bmm(a, b):
    return lax.dot_general(a, b, (((2,), (1,)), ((0,), (0,))), preferred_element_type=F32)


def _bmm_nt(a, b):
    return lax.dot_general(a, b, (((2,), (2,)), ((0,), (0,))), preferred_element_type=F32)


def _bmm_tn(a, b):
    return lax.dot_general(a, b, (((1,), (1,)), ((0,), (0,))), preferred_element_type=F32)


def _pack_lhs(m):
    hi = m.astype(BF16).astype(F32)
    return jnp.concatenate([hi, m - hi, hi], axis=-1).astype(BF16)


def _pack_rhs(m):
    hi, lo = _split2(m)
    return jnp.concatenate([hi, hi, lo], axis=-2)


def _unit_lower_inverse_packed(a, eye, n_steps):
    c = a.shape[-1]
    upper = lax.broadcasted_iota(jnp.int32, a.shape[:-1] + (2 * c,), a.ndim - 1) >= c
    a2 = _bmm(_pack_lhs(a), _pack_rhs(a))
    mp = jnp.concatenate([a2, eye - a], axis=-1)
    for _ in range(n_steps):
        res = _bmm(_pack_lhs(mp[..., :c]), _pack_rhs(mp))
        mp = res + jnp.where(upper, mp, 0.0)
    return mp[..., c:]


def _mixer_kernel(cfg, x_ref, xnext_ref, cst_ref, s0_ref, r0_ref, cos_ref, sin_ref, tri_ref, dmat_ref,
                  rscale_ref, cdec_ref, anw_ref, wdn_ref, wret_ref, wba_ref, convw_ref, gpar_ref,
                  dnw_ref, retw_ref, wout_ref, h_ref, cnew_ref, s_ref, r_ref, e_ref, a_ref):
    sb, tt, chunk, chained = cfg
    rows = sb * tt
    nblk = rows // BLK
    n_bat = N_HEADS * nblk
    shift = int(math.log2(chunk))
    n_steps = shift - 1
    base = SUBLANES - (DN_CONV - 1)
    l_idx = pl.program_id(1)

    @pl.when(l_idx == 0)
    def _():
        e_ref[:, base:SUBLANES, :] = cst_ref[...]
        if chained:
            s_ref[...] = s0_ref[...]
            r_ref[...] = r0_ref[...]

    x = x_ref[...].reshape(rows, D_MODEL)

    @pl.when((pl.program_id(0) == 0) & (l_idx == 0))
    def _():
        a_ref[...] = _rmsnorm(x, anw_ref[...]).astype(BF16)

    a = a_ref[...]
    e_ref[:, SUBLANES:, :] = jnp.dot(a, wdn_ref[:, 0:3 * QK],
                                     preferred_element_type=F32).reshape(sb, tt, 3 * QK)
    slabs = [wdn_ref.at[:, 3 * QK:4 * QK]] + [wret_ref.at[:, i * QK:(i + 1) * QK] for i in range(4)]
    proj = lambda i: jnp.dot(a, slabs[i][...], preferred_element_type=F32)
    pba = jnp.dot(a, wba_ref[...], preferred_element_type=F32)

    conv = e_ref[:, base:base + tt, :] * convw_ref[0:1, :]
    for j in range(1, DN_CONV):
        conv = conv + e_ref[:, base + j:base + j + tt, :] * convw_ref[j:j + 1, :]
    tail = e_ref[:, tt + base:tt + SUBLANES, :]
    cnew_ref[...] = tail
    e_ref[:, base:SUBLANES, :] = tail
    rq = proj(1)
    qkv = _silu(conv).reshape(rows, 3 * QK)
    rk = proj(2)

    beta_all = jax.nn.sigmoid(pba)
    g_all = -jnp.exp(gpar_ref[0:1, :]) * _softplus(pba + gpar_ref[1:2, :])
    g_hi = g_all.astype(BF16)
    g_r1 = g_all - g_hi.astype(F32)
    g_mid = g_r1.astype(BF16)
    g_lo = (g_r1 - g_mid.astype(F32)).astype(BF16)
    g_split = jnp.concatenate([t.reshape(nblk, BLK, LANES) for t in (g_hi, g_mid, g_lo)], axis=1)
    sums = _bmm(jnp.broadcast_to(tri_ref[...][None], (nblk, 2 * BLK, 3 * BLK)), g_split)
    gc_all = sums[:, :BLK].reshape(rows, LANES)
    gl_all = sums[:, BLK:].reshape(rows, LANES)
    gct_all = gc_all.T

    ri = lax.broadcasted_iota(jnp.int32, (BLK, BLK), 0)
    ci = lax.broadcasted_iota(jnp.int32, (BLK, BLK), 1)
    same = (ri >> shift) == (ci >> shift)
    causal = same & (ci <= ri)
    strict = same & (ci < ri)
    eye = (ri == ci).astype(F32)

    cos = cos_ref[0]
    sin = sin_ref[0]
    even = (lax.broadcasted_iota(jnp.int32, (rows, LANES), 1) & 1) == 0

    def rotary(t):
        swapped = jnp.where(even, pltpu.roll(t, LANES - 1, 1), pltpu.roll(t, 1, 1))
        return t * cos + swapped * sin

    per = {k: [] for k in ("qb", "kb", "kbeta", "rhs", "qd", "kt", "gcc", "gct", "egl",
                           "rq", "rk", "rqd", "rkt", "rv")}
    blocks = lambda t: t.reshape(nblk, BLK, t.shape[-1])
    for h in range(N_HEADS):
        qh = qkv[:, h * HEAD_DIM:(h + 1) * HEAD_DIM]
        kh = qkv[:, QK + h * HEAD_DIM:QK + (h + 1) * HEAD_DIM]
        vh = qkv[:, 2 * QK + h * HEAD_DIM:2 * QK + (h + 1) * HEAD_DIM]
        qb = qh * lax.rsqrt(jnp.sum(qh * qh, axis=-1, keepdims=True) + EPS) * (HEAD_DIM ** -0.5)
        kb = kh * lax.rsqrt(jnp.sum(kh * kh, axis=-1, keepdims=True) + EPS)
        bcol = _lane_bcast(beta_all, h)
        gcc = _lane_bcast(gc_all, GATE_LANE0 + h)
        glc = _lane_bcast(gl_all, GATE_LANE0 + h)
        egc = jnp.exp(gcc)
        kbeta = kb * bcol
        per["qb"].append(blocks(qb))
        per["kb"].append(blocks(kb))
        per["kbeta"].append(blocks(kbeta))
        per["rhs"].append(blocks(jnp.concatenate([kbeta * egc, vh * bcol], axis=1)))
        per["qd"].append(blocks(qb * egc))
        per["kt"].append(blocks(kb * jnp.exp(glc - gcc)))
        per["gcc"].append(blocks(gcc))
        per["egl"].append(jnp.exp(glc).reshape(rows // chunk, chunk, HEAD_DIM)[:, 0:1, :])
        per["gct"].extend(gct_all[GATE_LANE0 + h:GATE_LANE0 + h + 1, b * BLK:(b + 1) * BLK][None]
                          for b in range(nblk))
    bat = {k: jnp.concatenate(v, axis=0) for k, v in per.items() if v}
    bf = lambda k: bat[k].astype(BF16)
    rv = proj(3)

    delta = bat["gcc"][:, :, :BLK] - bat["gct"]
    decay = jnp.exp(jnp.where(causal, delta, -jnp.inf))
    kb16 = bf("kb")
    kq = _bmm_nt(jnp.concatenate([bat["kbeta"], bat["qb"]], axis=1).astype(BF16), kb16)
    amat = jnp.where(strict, kq[:, :BLK] * decay, 0.0)
    tinv = _unit_lower_inverse_packed(amat, eye, n_steps)
    z = proj(0)
    for h in range(N_HEADS):
        hs = slice(h * HEAD_DIM, (h + 1) * HEAD_DIM)
        rqh = rotary(rq[:, hs])
        rkh = rotary(rk[:, hs]) * (HEAD_DIM ** -0.5)
        per["rq"].append(blocks(rqh))
        per["rk"].append(blocks(rkh))
        per["rqd"].append(blocks(rqh * rscale_ref[0, :, hs]))
        per["rkt"].append(blocks(rkh * rscale_ref[1, :, hs]))
        per["rv"].append(blocks(rv[:, hs]))
    bat.update({k: jnp.concatenate(per[k], axis=0) for k in ("rq", "rk", "rqd", "rkt", "rv")})
    rg = proj(4)
    wu = _bmm(_pack_lhs(tinv), _pack_rhs(bat["rhs"]))
    qk = (kq[:, BLK:] * decay).astype(BF16)
    rv16 = bf("rv")
    dmat = jnp.broadcast_to(dmat_ref[...][:, None], (N_HEADS, nblk, BLK, BLK)).reshape(n_bat, BLK, BLK)
    rqk = (_bmm_nt(bf("rq"), bf("rk")) * dmat).astype(BF16)

    if chained:
        by_head = lambda t: t.reshape((N_HEADS, nblk) + t.shape[1:])
        transposed16 = lambda k: jnp.swapaxes(bat[k], 1, 2).astype(BF16)
        qg_wu = _bmm(jnp.concatenate([qk, transposed16("kt")], axis=1), wu.astype(BF16))
        qo = qg_wu[:, :BLK]
        gh = by_head(qg_wu[:, BLK:])
        qp = by_head((bat["qd"] - qo[:, :, :HEAD_DIM]).astype(BF16))
        op = by_head(qo[:, :, HEAD_DIM:])
        egl = by_head(bat["egl"])
        ro_rh = _bmm(jnp.concatenate([rqk, transposed16("rkt")], axis=1), rv16)
        ro = by_head(ro_rh[:, :BLK])
        rh = by_head(ro_rh[:, BLK:])
        rqd = by_head(bf("rqd"))
        s_cur = s_ref[0]
        r_cur = r_ref[0]
        o_dn = []
        o_ret = []
        for c in range(nblk):
            s16 = s_cur.astype(BF16)
            qg = jnp.concatenate([qp[:, c], gh[:, c, :, :HEAD_DIM].astype(BF16)], axis=1)
            qgs = _bmm(qg, s16)
            o_dn.append(op[:, c] + qgs[:, :BLK])
            s_cur = s_cur * egl[:, c] - qgs[:, BLK:] + gh[:, c, :, HEAD_DIM:]
            o_ret.append(ro[:, c] + _bmm(rqd[:, c], r_cur.astype(BF16)))
            r_cur = r_cur * cdec_ref[...] + rh[:, c]
        s_ref[0] = s_cur
        r_ref[0] = r_cur
        head_rows = lambda parts, h: jnp.concatenate([p[h] for p in parts], axis=0)
        o_dn = [head_rows(o_dn, h) for h in range(N_HEADS)]
        o_ret = [head_rows(o_ret, h) for h in range(N_HEADS)]
    else:
        n_seq = N_HEADS * sb
        seqs = lambda t: t.reshape(n_seq, tt, t.shape[-1])
        s_old = jnp.concatenate([s0_ref[:, h] for h in range(N_HEADS)], axis=0)
        lhs = jnp.concatenate([seqs(wu[:, :, :HEAD_DIM]), seqs(bat["qd"])], axis=1).astype(BF16)
        res = _bmm(lhs, s_old.astype(BF16))
        v_new = seqs(wu[:, :, HEAD_DIM:]) - res[:, :tt]
        v16 = v_new.astype(BF16)
        s_new = s_old * bat["egl"] + _bmm_tn(seqs(bat["kt"]).astype(BF16), v16)
        o = res[:, tt:].reshape(n_bat, BLK, HEAD_DIM) + _bmm(qk, v16.reshape(n_bat, BLK, HEAD_DIM))
        r_old = jnp.concatenate([r0_ref[:, h] for h in range(N_HEADS)], axis=0)
        ro = _bmm(rqk, rv16) + _bmm(seqs(bat["rqd"]).astype(BF16),
                                    r_old.astype(BF16)).reshape(n_bat, BLK, HEAD_DIM)
        cdec = jnp.broadcast_to(cdec_ref[...][:, None], (N_HEADS, sb, 1, HEAD_DIM))
        r_new = (r_old * cdec.reshape(n_seq, 1, HEAD_DIM)
                 + _bmm_tn(seqs(bat["rkt"]).astype(BF16), seqs(bat["rv"]).astype(BF16)))
        for h in range(N_HEADS):
            s_ref[:, h] = s_new[h * sb:(h + 1) * sb]
            r_ref[:, h] = r_new[h * sb:(h + 1) * sb]
        o_dn = [o[h * nblk:(h + 1) * nblk].reshape(rows, HEAD_DIM) for h in range(N_HEADS)]
        o_ret = [ro[h * nblk:(h + 1) * nblk].reshape(rows, HEAD_DIM) for h in range(N_HEADS)]

    outs = []
    for h in range(N_HEADS):
        hs = slice(h * HEAD_DIM, (h + 1) * HEAD_DIM)
        outs.append(_rmsnorm(o_dn[h], dnw_ref[...]) * _silu(z[:, hs]))
    for h in range(N_HEADS):
        hs = slice(h * HEAD_DIM, (h + 1) * HEAD_DIM)
        o = o_ret[h]
        mu = jnp.mean(o, axis=-1, keepdims=True)
        var = jnp.mean(jnp.square(o - mu), axis=-1, keepdims=True)
        outs.append((o - mu) * lax.rsqrt(var + EPS) * retw_ref[:, hs] * _silu(rg[:, hs]))
    mix = jnp.concatenate(outs, axis=1).astype(BF16)
    h_new = x + jnp.dot(mix, wout_ref[...], preferred_element_type=F32)
    h_ref[...] = h_new.reshape(sb, tt, D_MODEL)
    a_ref[...] = _rmsnorm(xnext_ref[...].reshape(rows, D_MODEL), anw_ref[...]).astype(BF16)


def _ffn_kernel(cfg, h_ref, p_ref, fst_ref, fnw_ref, wup_ref, fcw_ref, fcb_ref, wdown_ref,
                pnw_ref, wgate_ref, wple_ref, finw_ref,
                y_ref, fnew_ref, e_ref):
    sb, tt, final_norm = cfg
    rows = sb * tt
    l_idx = pl.program_id(1)
    base = SUBLANES - (FFN_CONV - 1)

    @pl.when(l_idx == 0)
    def _():
        e_ref[:, base:SUBLANES, :] = fst_ref[...]

    h = h_ref[...].reshape(rows, D_MODEL)
    m = _rmsnorm(h, fnw_ref[...]).astype(BF16)
    for half in (0, D_FF):
        for c0, cw in FF_SLABS:
            cs = slice(half + c0, half + c0 + cw)
            e_ref[:, SUBLANES:, cs] = jnp.dot(m, wup_ref[:, cs],
                                              preferred_element_type=F32).reshape(sb, tt, cw)

    def conv(cs):
        acc = e_ref[:, base:base + tt, cs] * fcw_ref[0:1, cs]
        for j in range(1, FFN_CONV):
            acc = acc + e_ref[:, base + j:base + j + tt, cs] * fcw_ref[j:j + 1, cs]
        return (acc + fcb_ref[:, cs]).reshape(rows, cs.stop - cs.start)

    acc = h
    for c0, cw in FF_SLABS:
        ug = conv(slice(c0, c0 + cw))
        uv = conv(slice(D_FF + c0, D_FF + c0 + cw))
        act = (_silu(ug) * uv).astype(BF16)
        acc = acc + jnp.dot(act, wdown_ref[c0:c0 + cw, :], preferred_element_type=F32)
    tail = e_ref[:, tt + base:tt + SUBLANES, :]
    fnew_ref[...] = tail
    e_ref[:, base:SUBLANES, :] = tail

    h = acc
    gate = jax.nn.sigmoid(jnp.dot(_rmsnorm(h, pnw_ref[...]).astype(BF16), wgate_ref[...],
                                  preferred_element_type=F32))
    pp = jnp.dot(p_ref[...].reshape(rows, PLE_DIM).astype(BF16), wple_ref[...],
                 preferred_element_type=F32)
    h = h + gate * pp
    if final_norm:
        h = _rmsnorm(h, finw_ref[...])
    y_ref[...] = h.reshape(sb, tt, D_MODEL)


def _const_spec(shape):
    zeros = (0,) * len(shape)
    return pl.BlockSpec(shape, lambda b, l: zeros, pipeline_mode=pl.Buffered(1))


def _retention_tables(chunk, rows):
    hh = np.arange(N_HEADS, dtype=np.float64)
    log_gamma = np.log(1.0 - 2.0 ** (-5.0 - hh))
    bpos = (np.arange(chunk, dtype=np.float64) + 1.0)[None, :] * log_gamma[:, None]
    causal = np.tril(np.ones((chunk, chunk), dtype=bool))
    dmat = np.where(causal, np.exp(np.where(causal, bpos[:, :, None] - bpos[:, None, :], 0.0)), 0.0)
    n = BLK // chunk
    dblk = np.einsum('ab,hij->haibj', np.eye(n), dmat).reshape(N_HEADS, BLK, BLK)
    qscale = np.exp(bpos)
    kscale = np.exp(bpos[:, -1:] - bpos)
    expand = lambda t: np.repeat(np.tile(t.T, (rows // chunk, 1)), HEAD_DIM, axis=1)
    rscale = np.stack([expand(qscale), expand(kscale)])
    cdec = np.broadcast_to(np.exp(bpos[:, -1]).reshape(N_HEADS, 1, 1), (N_HEADS, 1, HEAD_DIM))
    return tuple(jnp.asarray(t, F32) for t in (dblk, rscale, cdec))


def _chunk_sum_matrix(chunk):
    r = np.arange(BLK)
    same = (r[:, None] // chunk) == (r[None, :] // chunk)
    causal = same & (r[None, :] <= r[:, None])
    return jnp.asarray(np.tile(np.concatenate([causal, same], axis=0), (1, 3)), BF16)


def _rope_tables(pos0, tt, sb, n_l):
    inv = ROPE_BASE ** (-np.arange(0, HEAD_DIM, 2, dtype=np.float64) / HEAD_DIM)
    ang = (pos0 + np.arange(n_l * tt, dtype=np.float64))[:, None] * inv[None, :]
    cos = np.repeat(np.cos(ang), 2, axis=1)
    sin = np.stack([-np.sin(ang), np.sin(ang)], axis=-1).reshape(n_l * tt, HEAD_DIM)
    tile = lambda t: jnp.asarray(
        np.tile(t.reshape(n_l, 1, tt, HEAD_DIM), (1, sb, 1, 1)).reshape(n_l, sb * tt, HEAD_DIM), F32)
    return tile(cos), tile(sin)


def _mixer(x, cstate, s0, r0, pos0, wts, sb, tt):
    n_b, seq, _ = x.shape
    chunk = CHUNK if seq % CHUNK == 0 else seq
    chained = chunk == BLK
    rows = sb * tt
    n_l = seq // tt
    assert n_b % sb == 0 and seq % tt == 0 and rows % BLK == 0 and BLK % chunk == 0
    assert (sb == 1 and tt % chunk == 0) if chained else (tt == seq == chunk)
    cos, sin = _rope_tables(pos0, tt, sb, n_l)
    dmat, rscale, cdec = _retention_tables(chunk, rows)
    tri = _chunk_sum_matrix(chunk)
    state_spec = pl.BlockSpec((sb, N_HEADS, HEAD_DIM, HEAD_DIM), lambda b, l: (b, 0, 0, 0))
    x_spec = pl.BlockSpec((sb, tt, D_MODEL), lambda b, l: (b, l, 0))
    xnext_spec = pl.BlockSpec(
        (sb, tt, D_MODEL),
        lambda b, l: (jnp.minimum(b + (l + 1) // n_l, n_b // sb - 1), (l + 1) % n_l, 0))
    conv_spec = pl.BlockSpec((sb, DN_CONV - 1, 3 * QK), lambda b, l: (b, 0, 0))
    rope_spec = pl.BlockSpec((1, rows, HEAD_DIM), lambda b, l: (l, 0, 0))
    consts = (tri, dmat, rscale, cdec) + tuple(wts)
    return pl.pallas_call(
        functools.partial(_mixer_kernel, (sb, tt, chunk, chained)),
        grid=(n_b // sb, n_l),
        in_specs=[x_spec, xnext_spec, conv_spec, state_spec, state_spec, rope_spec, rope_spec]
        + [_const_spec(c.shape) for c in consts],
        out_specs=[x_spec, conv_spec, state_spec, state_spec],
        out_shape=[jax.ShapeDtypeStruct(x.shape, F32),
                   jax.ShapeDtypeStruct(cstate.shape, F32),
                   jax.ShapeDtypeStruct(s0.shape, F32),
                   jax.ShapeDtypeStruct(r0.shape, F32)],
        scratch_shapes=[pltpu.VMEM((sb, SUBLANES + tt, 3 * QK), F32),
                        pltpu.VMEM((rows, D_MODEL), BF16)],
        compiler_params=pltpu.CompilerParams(
            dimension_semantics=("arbitrary", "arbitrary"), vmem_limit_bytes=VMEM_LIMIT),
        name="mixer",
    )(x, x, cstate, s0, r0, cos, sin, *consts)


def _ffn(h, p, fstate, wts, final_norm, sb, tt):
    n_b, seq, _ = h.shape
    n_l = seq // tt
    assert n_b % sb == 0 and seq % tt == 0
    x_spec = pl.BlockSpec((sb, tt, D_MODEL), lambda b, l: (b, l, 0))
    p_spec = pl.BlockSpec((sb, tt, PLE_DIM), lambda b, l: (b, l, 0))
    st_spec = pl.BlockSpec((sb, FFN_CONV - 1, 2 * D_FF), lambda b, l: (b, 0, 0))
    return pl.pallas_call(
        functools.partial(_ffn_kernel, (sb, tt, final_norm)),
        grid=(n_b // sb, n_l),
        in_specs=[x_spec, p_spec, st_spec] + [_const_spec(w.shape) for w in wts],
        out_specs=[x_spec, st_spec],
        out_shape=[jax.ShapeDtypeStruct(h.shape, F32), jax.ShapeDtypeStruct(fstate.shape, F32)],
        scratch_shapes=[pltpu.VMEM((sb, SUBLANES + tt, 2 * D_FF), F32)],
        compiler_params=pltpu.CompilerParams(
            dimension_semantics=("arbitrary", "arbitrary"), vmem_limit_bytes=VMEM_LIMIT),
        name="ffn",
    )(h, p, fstate, *wts)


def kernel(x_prompt, x_sample, p_prompt, p_sample, state_dn_conv, state_dn, state_ret, state_ffn_conv, attn_norm_w, w_in, dn_conv_w, dn_A_log, dn_dt_bias, dn_norm_w, ret_norm_w, w_out, ffn_norm_w, w_up, ffn_conv_w, ffn_conv_b, w_down, ple_norm_w, w_ple_gate, w_ple, final_norm_w):
    depth = w_in.shape[0]
    n_bp = x_prompt.shape[0]
    dec_len = x_sample.shape[1]
    row = lambda v: v.reshape(1, -1).astype(F32)
    hp, hs = x_prompt, x_sample
    outs_p = ([], [], [], [])
    outs_s = ([], [], [], [])
    for i in range(depth):
        wi = w_in[i]
        wdn = wi[:, :4 * QK].astype(BF16)
        wret = wi[:, 4 * QK + 2 * N_HEADS:].astype(BF16)
        wba = jnp.pad(wi[:, 4 * QK:4 * QK + 2 * N_HEADS].astype(BF16),
                      ((0, 0), (0, LANES - 2 * N_HEADS)))
        gpar = jnp.zeros((2, LANES), F32)
        gpar = gpar.at[0, GATE_LANE0:GATE_LANE0 + N_HEADS].set(dn_A_log[i])
        gpar = gpar.at[1, GATE_LANE0:GATE_LANE0 + N_HEADS].set(dn_dt_bias[i])
        mixer_w = (row(attn_norm_w[i]), wdn, wret, wba, dn_conv_w[i], gpar, row(dn_norm_w[i]),
                   row(ret_norm_w[i]), w_out[i].astype(BF16))
        ffn_w = (row(ffn_norm_w[i]), w_up[i].astype(BF16), ffn_conv_w[i], row(ffn_conv_b[i]),
                 w_down[i].astype(BF16), row(ple_norm_w[i]), w_ple_gate[i].astype(BF16),
                 w_ple[i].astype(BF16), row(final_norm_w))
        last = i == depth - 1
        zc = jnp.zeros((n_bp, DN_CONV - 1, 3 * QK), F32)
        zs = jnp.zeros((n_bp, N_HEADS, HEAD_DIM, HEAD_DIM), F32)
        zf = jnp.zeros((n_bp, FFN_CONV - 1, 2 * D_FF), F32)
        hp, c1, s1, r1 = _mixer(hp, zc, zs, zs, 0, mixer_w, *PROMPT_MIXER_TILE)
        hp, f1 = _ffn(hp, p_prompt[i], zf, ffn_w, last, *PROMPT_FFN_TILE)
        hs, c2, s2, r2 = _mixer(hs, state_dn_conv[i], state_dn[i], state_ret[i], PAST_LEN,
                                mixer_w, SAMPLE_MIXER_SEQS, dec_len)
        hs, f2 = _ffn(hs, p_sample[i], state_ffn_conv[i], ffn_w, last, SAMPLE_FFN_SEQS, dec_len)
        for lst, val in zip(outs_p, (c1, s1, r1, f1)):
            lst.append(val)
        for lst, val in zip(outs_s, (c2, s2, r2, f2)):
            lst.append(val)
    return (hp, hs,
            jnp.stack(outs_p[0]), jnp.stack(outs_p[1]), jnp.stack(outs_p[2]), jnp.stack(outs_p[3]),
            jnp.stack(outs_s[0]), jnp.stack(outs_s[1]), jnp.stack(outs_s[2]), jnp.stack(outs_s[3]))
```

```python
import functools
import math

import jax
import jax.numpy as jnp
import numpy as np
from jax import lax
from jax.experimental import pallas as pl
from jax.experimental.pallas import tpu as pltpu

F32 = jnp.float32
BF16 = jnp.bfloat16

D_MODEL = 1024
HEAD_DIM = 128
N_HEADS = 4
QK = N_HEADS * HEAD_DIM
DN_CONV = 4
FFN_CONV = 3
D_FF = 2816
PLE_DIM = 256
CHUNK = 64
PAST_LEN = 16384
ROPE_BASE = 10000.0
EPS = 1e-6
LANES = 128
SUBLANES = 8
MXU_DIM = 256
BLK = 64
GATE_LANE0 = N_HEADS
FF_SLABS = tuple((c, MXU_DIM) for c in range(0, D_FF, MXU_DIM))
VMEM_LIMIT = 56 * 1024 * 1024
PROMPT_MIXER_TILE = (1, 512)
PROMPT_FFN_TILE = (1, 512)
SAMPLE_MIXER_SEQS = 16
SAMPLE_FFN_SEQS = 32


def _split2(x):
    hi = x.astype(BF16)
    lo = (x - hi.astype(F32)).astype(BF16)
    return hi, lo


def _rmsnorm(x, w):
    return x * lax.rsqrt(jnp.mean(x * x, axis=-1, keepdims=True) + EPS) * w


def _silu(x):
    return x * jax.nn.sigmoid(x)


def _softplus(x):
    return jnp.maximum(x, 0.0) + jnp.log(1.0 + jnp.exp(-jnp.abs(x)))


def _lane_bcast(x, lane, width=LANES):
    return jnp.broadcast_to(x[:, lane:lane + 1], (x.shape[0], width))


def _bmm(a, b):
    return lax.dot_general(a, b, (((2,), (1,)), ((0,), (0,))), preferred_element_type=F32)


def _bmm_nt(a, b):
    return lax.dot_general(a, b, (((2,), (2,)), ((0,), (0,))), preferred_element_type=F32)


def _bmm_tn(a, b):
    return lax.dot_general(a, b, (((1,), (1,)), ((0,), (0,))), preferred_element_type=F32)


def _pack_lhs(m):
    hi = m.astype(BF16).astype(F32)
    return jnp.concatenate([hi, m - hi, hi], axis=-1).astype(BF16)


def _pack_rhs(m):
    hi, lo = _split2(m)
    return jnp.concatenate([hi, hi, lo], axis=-2)


def _unit_lower_inverse_packed(a, eye, n_steps):
    c = a.shape[-1]
    upper = lax.broadcasted_iota(jnp.int32, a.shape[:-1] + (2 * c,), a.ndim - 1) >= c
    a2 = _bmm(_pack_lhs(a), _pack_rhs(a))
    mp = jnp.concatenate([a2, eye - a], axis=-1)
    for _ in range(n_steps):
        res = _bmm(_pack_lhs(mp[..., :c]), _pack_rhs(mp))
        mp = res + jnp.where(upper, mp, 0.0)
    return mp[..., c:]


def _mixer_kernel(cfg, x_ref, cst_ref, s0_ref, r0_ref, cos_ref, sin_ref, tri_ref, dmat_ref,
                  rscale_ref, cdec_ref, anw_ref, wdn_ref, wret_ref, wba_ref, convw_ref, gpar_ref,
                  dnw_ref, retw_ref, wout_ref, h_ref, cnew_ref, s_ref, r_ref, e_ref):
    sb, tt, chunk, chained = cfg
    rows = sb * tt
    nblk = rows // BLK
    n_bat = N_HEADS * nblk
    shift = int(math.log2(chunk))
    n_steps = shift - 1
    base = SUBLANES - (DN_CONV - 1)
    l_idx = pl.program_id(1)

    @pl.when(l_idx == 0)
    def _():
        e_ref[:, base:SUBLANES, :] = cst_ref[...]
        if chained:
            s_ref[...] = s0_ref[...]
            r_ref[...] = r0_ref[...]

    x = x_ref[...].reshape(rows, D_MODEL)
    a = _rmsnorm(x, anw_ref[...]).astype(BF16)
    e_ref[:, SUBLANES:, :] = jnp.dot(a, wdn_ref[:, 0:3 * QK],
                                     preferred_element_type=F32).reshape(sb, tt, 3 * QK)
    slabs = [wdn_ref.at[:, 3 * QK:4 * QK]] + [wret_ref.at[:, i * QK:(i + 1) * QK] for i in range(4)]
    proj = lambda i: jnp.dot(a, slabs[i][...], preferred_element_type=F32)
    pba = jnp.dot(a, wba_ref[...], preferred_element_type=F32)

    conv = e_ref[:, base:base + tt, :] * convw_ref[0:1, :]
    for j in range(1, DN_CONV):
        conv = conv + e_ref[:, base + j:base + j + tt, :] * convw_ref[j:j + 1, :]
    tail = e_ref[:, tt + base:tt + SUBLANES, :]
    cnew_ref[...] = tail
    e_ref[:, base:SUBLANES, :] = tail
    rq = proj(1)
    qkv = _silu(conv).reshape(rows, 3 * QK)
    rk = proj(2)

    beta_all = jax.nn.sigmoid(pba)
    g_all = -jnp.exp(gpar_ref[0:1, :]) * _softplus(pba + gpar_ref[1:2, :])
    g_hi = g_all.astype(BF16)
    g_r1 = g_all - g_hi.astype(F32)
    g_mid = g_r1.astype(BF16)
    g_lo = (g_r1 - g_mid.astype(F32)).astype(BF16)
    g_split = jnp.concatenate([t.reshape(nblk, BLK, LANES) for t in (g_hi, g_mid, g_lo)], axis=1)
    sums = _bmm(jnp.broadcast_to(tri_ref[...][None], (nblk, 2 * BLK, 3 * BLK)), g_split)
    gc_all = sums[:, :BLK].reshape(rows, LANES)
    gl_all = sums[:, BLK:].reshape(rows, LANES)
    gct_all = gc_all.T

    ri = lax.broadcasted_iota(jnp.int32, (BLK, BLK), 0)
    ci = lax.broadcasted_iota(jnp.int32, (BLK, BLK), 1)
    same = (ri >> shift) == (ci >> shift)
    causal = same & (ci <= ri)
    strict = same & (ci < ri)
    eye = (ri == ci).astype(F32)

    cos = cos_ref[l_idx]
    sin = sin_ref[l_idx]
    even = (lax.broadcasted_iota(jnp.int32, (rows, LANES), 1) & 1) == 0

    def rotary(t):
        swapped = jnp.where(even, pltpu.roll(t, LANES - 1, 1), pltpu.roll(t, 1, 1))
        return t * cos + swapped * sin

    per = {k: [] for k in ("qb", "kb", "kbeta", "rhs", "qd", "kt", "gcc", "gct", "egl",
                           "rq", "rk", "rqd", "rkt", "rv")}
    blocks = lambda t: t.reshape(nblk, BLK, t.shape[-1])
    for h in range(N_HEADS):
        qh = qkv[:, h * HEAD_DIM:(h + 1) * HEAD_DIM]
        kh = qkv[:, QK + h * HEAD_DIM:QK + (h + 1) * HEAD_DIM]
        vh = qkv[:, 2 * QK + h * HEAD_DIM:2 * QK + (h + 1) * HEAD_DIM]
        qb = qh * lax.rsqrt(jnp.sum(qh * qh, axis=-1, keepdims=True) + EPS) * (HEAD_DIM ** -0.5)
        kb = kh * lax.rsqrt(jnp.sum(kh * kh, axis=-1, keepdims=True) + EPS)
        bcol = _lane_bcast(beta_all, h)
        gcc = _lane_bcast(gc_all, GATE_LANE0 + h)
        glc = _lane_bcast(gl_all, GATE_LANE0 + h)
        egc = jnp.exp(gcc)
        kbeta = kb * bcol
        per["qb"].append(blocks(qb))
        per["kb"].append(blocks(kb))
        per["kbeta"].append(blocks(kbeta))
        per["rhs"].append(blocks(jnp.concatenate([kbeta * egc, vh * bcol], axis=1)))
        per["qd"].append(blocks(qb * egc))
        per["kt"].append(blocks(kb * jnp.exp(glc - gcc)))
        per["gcc"].append(blocks(gcc))
        per["egl"].append(jnp.exp(glc).reshape(rows // chunk, chunk, HEAD_DIM)[:, 0:1, :])
        per["gct"].extend(gct_all[GATE_LANE0 + h:GATE_LANE0 + h + 1, b * BLK:(b + 1) * BLK][None]
                          for b in range(nblk))
    bat = {k: jnp.concatenate(v, axis=0) for k, v in per.items() if v}
    bf = lambda k: bat[k].astype(BF16)
    rv = proj(3)

    delta = bat["gcc"][:, :, :BLK] - bat["gct"]
    decay = jnp.exp(jnp.where(causal, delta, -jnp.inf))
    kb16 = bf("kb")
    kq = _bmm_nt(jnp.concatenate([bat["kbeta"], bat["qb"]], axis=1).astype(BF16), kb16)
    amat = jnp.where(strict, kq[:, :BLK] * decay, 0.0)
    tinv = _unit_lower_inverse_packed(amat, eye, n_steps)
    z = proj(0)
    for h in range(N_HEADS):
        hs = slice(h * HEAD_DIM, (h + 1) * HEAD_DIM)
        rqh = rotary(rq[:, hs])
        rkh = rotary(rk[:, hs]) * (HEAD_DIM ** -0.5)
        per["rq"].append(blocks(rqh))
        per["rk"].append(blocks(rkh))
        per["rqd"].append(blocks(rqh * rscale_ref[0, :, hs]))
        per["rkt"].append(blocks(rkh * rscale_ref[1, :, hs]))
        per["rv"].append(blocks(rv[:, hs]))
    bat.update({k: jnp.concatenate(per[k], axis=0) for k in ("rq", "rk", "rqd", "rkt", "rv")})
    rg = proj(4)
    wu = _bmm(_pack_lhs(tinv), _pack_rhs(bat["rhs"]))
    qk = (kq[:, BLK:] * decay).astype(BF16)
    rv16 = bf("rv")
    dmat = jnp.broadcast_to(dmat_ref[...][:, None], (N_HEADS, nblk, BLK, BLK)).reshape(n_bat, BLK, BLK)
    rqk = (_bmm_nt(bf("rq"), bf("rk")) * dmat).astype(BF16)

    if chained:
        by_head = lambda t: t.reshape((N_HEADS, nblk) + t.shape[1:])
        transposed16 = lambda k: jnp.swapaxes(bat[k], 1, 2).astype(BF16)
        qg_wu = _bmm(jnp.concatenate([qk, transposed16("kt")], axis=1), wu.astype(BF16))
        qo = qg_wu[:, :BLK]
        gh = by_head(qg_wu[:, BLK:])
        qp = by_head((bat["qd"] - qo[:, :, :HEAD_DIM]).astype(BF16))
        op = by_head(qo[:, :, HEAD_DIM:])
        egl = by_head(bat["egl"])
        ro_rh = _bmm(jnp.concatenate([rqk, transposed16("rkt")], axis=1), rv16)
        ro = by_head(ro_rh[:, :BLK])
        rh = by_head(ro_rh[:, BLK:])
        rqd = by_head(bf("rqd"))
        s_cur = s_ref[0]
        r_cur = r_ref[0]
        o_dn = []
        o_ret = []
        for c in range(nblk):
            s16 = s_cur.astype(BF16)
            qg = jnp.concatenate([qp[:, c], gh[:, c, :, :HEAD_DIM].astype(BF16)], axis=1)
            qgs = _bmm(qg, s16)
            o_dn.append(op[:, c] + qgs[:, :BLK])
            s_cur = s_cur * egl[:, c] - qgs[:, BLK:] + gh[:, c, :, HEAD_DIM:]
            o_ret.append(ro[:, c] + _bmm(rqd[:, c], r_cur.astype(BF16)))
            r_cur = r_cur * cdec_ref[...] + rh[:, c]
        s_ref[0] = s_cur
        r_ref[0] = r_cur
        head_rows = lambda parts, h: jnp.concatenate([p[h] for p in parts], axis=0)
        o_dn = [head_rows(o_dn, h) for h in range(N_HEADS)]
        o_ret = [head_rows(o_ret, h) for h in range(N_HEADS)]
    else:
        n_seq = N_HEADS * sb
        seqs = lambda t: t.reshape(n_seq, tt, t.shape[-1])
        s_old = jnp.concatenate([s0_ref[:, h] for h in range(N_HEADS)], axis=0)
        lhs = jnp.concatenate([seqs(wu[:, :, :HEAD_DIM]), seqs(bat["qd"])], axis=1).astype(BF16)
        res = _bmm(lhs, s_old.astype(BF16))
        v_new = seqs(wu[:, :, HEAD_DIM:]) - res[:, :tt]
        v16 = v_new.astype(BF16)
        s_new = s_old * bat["egl"] + _bmm_tn(seqs(bat["kt"]).astype(BF16), v16)
        o = res[:, tt:].reshape(n_bat, BLK, HEAD_DIM) + _bmm(qk, v16.reshape(n_bat, BLK, HEAD_DIM))
        r_old = jnp.concatenate([r0_ref[:, h] for h in range(N_HEADS)], axis=0)
        ro = _bmm(rqk, rv16) + _bmm(seqs(bat["rqd"]).astype(BF16),
                                    r_old.astype(BF16)).reshape(n_bat, BLK, HEAD_DIM)
        cdec = jnp.broadcast_to(cdec_ref[...][:, None], (N_HEADS, sb, 1, HEAD_DIM))
        r_new = (r_old * cdec.reshape(n_seq, 1, HEAD_DIM)
                 + _bmm_tn(seqs(bat["rkt"]).astype(BF16), seqs(bat["rv"]).astype(BF16)))
        for h in range(N_HEADS):
            s_ref[:, h] = s_new[h * sb:(h + 1) * sb]
            r_ref[:, h] = r_new[h * sb:(h + 1) * sb]
        o_dn = [o[h * nblk:(h + 1) * nblk].reshape(rows, HEAD_DIM) for h in range(N_HEADS)]
        o_ret = [ro[h * nblk:(h + 1) * nblk].reshape(rows, HEAD_DIM) for h in range(N_HEADS)]

    outs = []
    for h in range(N_HEADS):
        hs = slice(h * HEAD_DIM, (h + 1) * HEAD_DIM)
        outs.append(_rmsnorm(o_dn[h], dnw_ref[...]) * _silu(z[:, hs]))
    for h in range(N_HEADS):
        hs = slice(h * HEAD_DIM, (h + 1) * HEAD_DIM)
        o = o_ret[h]
        mu = jnp.mean(o, axis=-1, keepdims=True)
        var = jnp.mean(jnp.square(o - mu), axis=-1, keepdims=True)
        outs.append((o - mu) * lax.rsqrt(var + EPS) * retw_ref[:, hs] * _silu(rg[:, hs]))
    mix = jnp.concatenate(outs, axis=1).astype(BF16)
    h_new = x + jnp.dot(mix, wout_ref[...], preferred_element_type=F32)
    h_ref[...] = h_new.reshape(sb, tt, D_MODEL)


def _ffn_kernel(cfg, h_ref, p_ref, fst_ref, fnw_ref, wup_ref, fcw_ref, fcb_ref, wdown_ref,
                pnw_ref, wgate_ref, wple_ref, finw_ref,
                y_ref, fnew_ref, e_ref):
    sb, tt, final_norm = cfg
    rows = sb * tt
    l_idx = pl.program_id(1)
    base = SUBLANES - (FFN_CONV - 1)

    @pl.when(l_idx == 0)
    def _():
        e_ref[:, base:SUBLANES, :] = fst_ref[...]

    h = h_ref[...].reshape(rows, D_MODEL)
    m = _rmsnorm(h, fnw_ref[...]).astype(BF16)
    for half in (0, D_FF):
        for c0, cw in FF_SLABS:
            cs = slice(half + c0, half + c0 + cw)
            e_ref[:, SUBLANES:, cs] = jnp.dot(m, wup_ref[:, cs],
                                              preferred_element_type=F32).reshape(sb, tt, cw)

    def conv(cs):
        acc = e_ref[:, base:base + tt, cs] * fcw_ref[0:1, cs]
        for j in range(1, FFN_CONV):
            acc = acc + e_ref[:, base + j:base + j + tt, cs] * fcw_ref[j:j + 1, cs]
        return (acc + fcb_ref[:, cs]).reshape(rows, cs.stop - cs.start)

    acc = h
    for c0, cw in FF_SLABS:
        ug = conv(slice(c0, c0 + cw))
        uv = conv(slice(D_FF + c0, D_FF + c0 + cw))
        act = (_silu(ug) * uv).astype(BF16)
        acc = acc + jnp.dot(act, wdown_ref[c0:c0 + cw, :], preferred_element_type=F32)
    tail = e_ref[:, tt + base:tt + SUBLANES, :]
    fnew_ref[...] = tail
    e_ref[:, base:SUBLANES, :] = tail

    h = acc
    gate = jax.nn.sigmoid(jnp.dot(_rmsnorm(h, pnw_ref[...]).astype(BF16), wgate_ref[...],
                                  preferred_element_type=F32))
    pp = jnp.dot(p_ref[...].reshape(rows, PLE_DIM).astype(BF16), wple_ref[...],
                 preferred_element_type=F32)
    h = h + gate * pp
    if final_norm:
        h = _rmsnorm(h, finw_ref[...])
    y_ref[...] = h.reshape(sb, tt, D_MODEL)


def _const_spec(shape):
    zeros = (0,) * len(shape)
    return pl.BlockSpec(shape, lambda b, l: zeros, pipeline_mode=pl.Buffered(1))


def _retention_tables(chunk, rows):
    hh = np.arange(N_HEADS, dtype=np.float64)
    log_gamma = np.log(1.0 - 2.0 ** (-5.0 - hh))
    bpos = (np.arange(chunk, dtype=np.float64) + 1.0)[None, :] * log_gamma[:, None]
    causal = np.tril(np.ones((chunk, chunk), dtype=bool))
    dmat = np.where(causal, np.exp(np.where(causal, bpos[:, :, None] - bpos[:, None, :], 0.0)), 0.0)
    n = BLK // chunk
    dblk = np.einsum('ab,hij->haibj', np.eye(n), dmat).reshape(N_HEADS, BLK, BLK)
    qscale = np.exp(bpos)
    kscale = np.exp(bpos[:, -1:] - bpos)
    expand = lambda t: np.repeat(np.tile(t.T, (rows // chunk, 1)), HEAD_DIM, axis=1)
    rscale = np.stack([expand(qscale), expand(kscale)])
    cdec = np.broadcast_to(np.exp(bpos[:, -1]).reshape(N_HEADS, 1, 1), (N_HEADS, 1, HEAD_DIM))
    return tuple(jnp.asarray(t, F32) for t in (dblk, rscale, cdec))


def _chunk_sum_matrix(chunk):
    r = np.arange(BLK)
    same = (r[:, None] // chunk) == (r[None, :] // chunk)
    causal = same & (r[None, :] <= r[:, None])
    return jnp.asarray(np.tile(np.concatenate([causal, same], axis=0), (1, 3)), BF16)


def _rope_tables(pos0, tt, sb, n_l):
    inv = ROPE_BASE ** (-np.arange(0, HEAD_DIM, 2, dtype=np.float64) / HEAD_DIM)
    ang = (pos0 + np.arange(n_l * tt, dtype=np.float64))[:, None] * inv[None, :]
    cos = np.repeat(np.cos(ang), 2, axis=1)
    sin = np.stack([-np.sin(ang), np.sin(ang)], axis=-1).reshape(n_l * tt, HEAD_DIM)
    tile = lambda t: jnp.asarray(
        np.tile(t.reshape(n_l, 1, tt, HEAD_DIM), (1, sb, 1, 1)).reshape(n_l, sb * tt, HEAD_DIM), F32)
    return tile(cos), tile(sin)


def _mixer(x, cstate, s0, r0, pos0, wts, sb, tt):
    n_b, seq, _ = x.shape
    chunk = CHUNK if seq % CHUNK == 0 else seq
    chained = chunk == BLK
    rows = sb * tt
    n_l = seq // tt
    assert n_b % sb == 0 and seq % tt == 0 and rows % BLK == 0 and BLK % chunk == 0
    assert (sb == 1 and tt % chunk == 0) if chained else (tt == seq == chunk)
    cos, sin = _rope_tables(pos0, tt, sb, n_l)
    dmat, rscale, cdec = _retention_tables(chunk, rows)
    tri = _chunk_sum_matrix(chunk)
    state_spec = pl.BlockSpec((sb, N_HEADS, HEAD_DIM, HEAD_DIM), lambda b, l: (b, 0, 0, 0))
    x_spec = pl.BlockSpec((sb, tt, D_MODEL), lambda b, l: (b, l, 0))
    conv_spec = pl.BlockSpec((sb, DN_CONV - 1, 3 * QK), lambda b, l: (b, 0, 0))
    rope_spec = _const_spec(cos.shape)
    consts = (tri, dmat, rscale, cdec) + tuple(wts)
    return pl.pallas_call(
        functools.partial(_mixer_kernel, (sb, tt, chunk, chained)),
        grid=(n_b // sb, n_l),
        in_specs=[x_spec, conv_spec, state_spec, state_spec, rope_spec, rope_spec]
        + [_const_spec(c.shape) for c in consts],
        out_specs=[x_spec, conv_spec, state_spec, state_spec],
        out_shape=[jax.ShapeDtypeStruct(x.shape, F32),
                   jax.ShapeDtypeStruct(cstate.shape, F32),
                   jax.ShapeDtypeStruct(s0.shape, F32),
                   jax.ShapeDtypeStruct(r0.shape, F32)],
        scratch_shapes=[pltpu.VMEM((sb, SUBLANES + tt, 3 * QK), F32)],
        compiler_params=pltpu.CompilerParams(
            dimension_semantics=("arbitrary", "arbitrary"), vmem_limit_bytes=VMEM_LIMIT),
        name="mixer",
    )(x, cstate, s0, r0, cos, sin, *consts)


def _ffn(h, p, fstate, wts, final_norm, sb, tt):
    n_b, seq, _ = h.shape
    n_l = seq // tt
    assert n_b % sb == 0 and seq % tt == 0
    x_spec = pl.BlockSpec((sb, tt, D_MODEL), lambda b, l: (b, l, 0))
    p_spec = pl.BlockSpec((sb, tt, PLE_DIM), lambda b, l: (b, l, 0))
    st_spec = pl.BlockSpec((sb, FFN_CONV - 1, 2 * D_FF), lambda b, l: (b, 0, 0))
    return pl.pallas_call(
        functools.partial(_ffn_kernel, (sb, tt, final_norm)),
        grid=(n_b // sb, n_l),
        in_specs=[x_spec, p_spec, st_spec] + [_const_spec(w.shape) for w in wts],
        out_specs=[x_spec, st_spec],
        out_shape=[jax.ShapeDtypeStruct(h.shape, F32), jax.ShapeDtypeStruct(fstate.shape, F32)],
        scratch_shapes=[pltpu.VMEM((sb, SUBLANES + tt, 2 * D_FF), F32)],
        compiler_params=pltpu.CompilerParams(
            dimension_semantics=("arbitrary", "arbitrary"), vmem_limit_bytes=VMEM_LIMIT),
        name="ffn",
    )(h, p, fstate, *wts)


def kernel(x_prompt, x_sample, p_prompt, p_sample, state_dn_conv, state_dn, state_ret, state_ffn_conv, attn_norm_w, w_in, dn_conv_w, dn_A_log, dn_dt_bias, dn_norm_w, ret_norm_w, w_out, ffn_norm_w, w_up, ffn_conv_w, ffn_conv_b, w_down, ple_norm_w, w_ple_gate, w_ple, final_norm_w):
    depth = w_in.shape[0]
    n_bp = x_prompt.shape[0]
    dec_len = x_sample.shape[1]
    row = lambda v: v.reshape(1, -1).astype(F32)
    hp, hs = x_prompt, x_sample
    outs_p = ([], [], [], [])
    outs_s = ([], [], [], [])
    for i in range(depth):
        wi = w_in[i]
        wdn = wi[:, :4 * QK].astype(BF16)
        wret = wi[:, 4 * QK + 2 * N_HEADS:].astype(BF16)
        wba = jnp.pad(wi[:, 4 * QK:4 * QK + 2 * N_HEADS].astype(BF16),
                      ((0, 0), (0, LANES - 2 * N_HEADS)))
        gpar = jnp.zeros((2, LANES), F32)
        gpar = gpar.at[0, GATE_LANE0:GATE_LANE0 + N_HEADS].set(dn_A_log[i])
        gpar = gpar.at[1, GATE_LANE0:GATE_LANE0 + N_HEADS].set(dn_dt_bias[i])
        mixer_w = (row(attn_norm_w[i]), wdn, wret, wba, dn_conv_w[i], gpar, row(dn_norm_w[i]),
                   row(ret_norm_w[i]), w_out[i].astype(BF16))
        ffn_w = (row(ffn_norm_w[i]), w_up[i].astype(BF16), ffn_conv_w[i], row(ffn_conv_b[i]),
                 w_down[i].astype(BF16), row(ple_norm_w[i]), w_ple_gate[i].astype(BF16),
                 w_ple[i].astype(BF16), row(final_norm_w))
        last = i == depth - 1
        zc = jnp.zeros((n_bp, DN_CONV - 1, 3 * QK), F32)
        zs = jnp.zeros((n_bp, N_HEADS, HEAD_DIM, HEAD_DIM), F32)
        zf = jnp.zeros((n_bp, FFN_CONV - 1, 2 * D_FF), F32)
        hp, c1, s1, r1 = _mixer(hp, zc, zs, zs, 0, mixer_w, *PROMPT_MIXER_TILE)
        hp, f1 = _ffn(hp, p_prompt[i], zf, ffn_w, last, *PROMPT_FFN_TILE)
        hs, c2, s2, r2 = _mixer(hs, state_dn_conv[i], state_dn[i], state_ret[i], PAST_LEN,
                                mixer_w, SAMPLE_MIXER_SEQS, dec_len)
        hs, f2 = _ffn(hs, p_sample[i], state_ffn_conv[i], ffn_w, last, SAMPLE_FFN_SEQS, dec_len)
        for lst, val in zip(outs_p, (c1, s1, r1, f1)):
            lst.append(val)
        for lst, val in zip(outs_s, (c2, s2, r2, f2)):
            lst.append(val)
    return (hp, hs,
            jnp.stack(outs_p[0]), jnp.stack(outs_p[1]), jnp.stack(outs_p[2]), jnp.stack(outs_p[3]),
            jnp.stack(outs_s[0]), jnp.stack(outs_s[1]), jnp.stack(outs_s[2]), jnp.stack(outs_s[3]))
```

```python
import functools
import math

import jax
import jax.numpy as jnp
import numpy as np
from jax import lax
from jax.experimental import pallas as pl
from jax.experimental.pallas import tpu as pltpu

F32 = jnp.float32
BF16 = jnp.bfloat16

D_MODEL = 1024
HEAD_DIM = 128
N_HEADS = 4
QK = N_HEADS * HEAD_DIM
DN_CONV = 4
FFN_CONV = 3
D_FF = 2816
PLE_DIM = 256
CHUNK = 64
PAST_LEN = 16384
ROPE_BASE = 10000.0
EPS = 1e-6
LANES = 128
SUBLANES = 8
MXU_DIM = 256
BLK = 64
GATE_LANE0 = N_HEADS
FF_SLABS = tuple((c, MXU_DIM) for c in range(0, D_FF, MXU_DIM))
VMEM_LIMIT = 56 * 1024 * 1024
PROMPT_MIXER_TILE = (1, 512)
PROMPT_FFN_TILE = (1, 512)
SAMPLE_MIXER_SEQS = 16
SAMPLE_FFN_SEQS = 32


def _split2(x):
    hi = x.astype(BF16)
    lo = (x - hi.astype(F32)).astype(BF16)
    return hi, lo


def _rmsnorm(x, w):
    return x * lax.rsqrt(jnp.mean(x * x, axis=-1, keepdims=True) + EPS) * w


def _silu(x):
    return x * jax.nn.sigmoid(x)


def _softplus(x):
    return jnp.maximum(x, 0.0) + jnp.log(1.0 + jnp.exp(-jnp.abs(x)))


def _lane_bcast(x, lane, width=LANES):
    return jnp.broadcast_to(x[:, lane:lane + 1], (x.shape[0], width))


def _bmm(a, b):
    return lax.dot_general(a, b, (((2,), (1,)), ((0,), (0,))), preferred_element_type=F32)


def _bmm_nt(a, b):
    return lax.dot_general(a, b, (((2,), (2,)), ((0,), (0,))), preferred_element_type=F32)


def _bmm_tn(a, b):
    return lax.dot_general(a, b, (((1,), (1,)), ((0,), (0,))), preferred_element_type=F32)


def _pack_lhs(m):
    hi = m.astype(BF16).astype(F32)
    return jnp.concatenate([hi, m - hi, hi], axis=-1).astype(BF16)


def _pack_rhs(m):
    hi, lo = _split2(m)
    return jnp.concatenate([hi, hi, lo], axis=-2)


def _unit_lower_inverse_packed(a, eye, n_steps):
    c = a.shape[-1]
    upper = lax.broadcasted_iota(jnp.int32, a.shape[:-1] + (2 * c,), a.ndim - 1) >= c
    a2 = _bmm(_pack_lhs(a), _pack_rhs(a))
    mp = jnp.concatenate([a2, eye - a], axis=-1)
    for _ in range(n_steps):
        m = mp[..., :c]
        hi = m.astype(BF16).astype(F32)
        lhs = jnp.concatenate([hi, m - hi], axis=-1).astype(BF16)
        r16 = mp.astype(BF16)
        res = _bmm(lhs, jnp.concatenate([r16, r16], axis=-2))
        mp = res + jnp.where(upper, mp, 0.0)
    return mp[..., c:]


def _mixer_kernel(cfg, x_ref, cst_ref, s0_ref, r0_ref, cos_ref, sin_ref, tri_ref, dmat_ref,
                  rscale_ref, cdec_ref, anw_ref, wdn_ref, wret_ref, wba_ref, convw_ref, gpar_ref,
                  dnw_ref, retw_ref, wout_ref, h_ref, cnew_ref, s_ref, r_ref, e_ref):
    sb, tt, chunk, chained = cfg
    rows = sb * tt
    nblk = rows // BLK
    n_bat = N_HEADS * nblk
    shift = int(math.log2(chunk))
    n_steps = shift - 1
    base = SUBLANES - (DN_CONV - 1)
    l_idx = pl.program_id(1)

    @pl.when(l_idx == 0)
    def _():
        e_ref[:, base:SUBLANES, :] = cst_ref[...]
        if chained:
            s_ref[...] = s0_ref[...]
            r_ref[...] = r0_ref[...]

    x = x_ref[...].reshape(rows, D_MODEL)
    a = _rmsnorm(x, anw_ref[...]).astype(BF16)
    e_ref[:, SUBLANES:, :] = jnp.dot(a, wdn_ref[:, 0:3 * QK],
                                     preferred_element_type=F32).reshape(sb, tt, 3 * QK)
    slabs = [wdn_ref.at[:, 3 * QK:4 * QK]] + [wret_ref.at[:, i * QK:(i + 1) * QK] for i in range(4)]
    proj = lambda i: jnp.dot(a, slabs[i][...], preferred_element_type=F32)
    pba = jnp.dot(a, wba_ref[...], preferred_element_type=F32)

    conv = e_ref[:, base:base + tt, :] * convw_ref[0:1, :]
    for j in range(1, DN_CONV):
        conv = conv + e_ref[:, base + j:base + j + tt, :] * convw_ref[j:j + 1, :]
    tail = e_ref[:, tt + base:tt + SUBLANES, :]
    cnew_ref[...] = tail
    e_ref[:, base:SUBLANES, :] = tail
    rq = proj(1)
    qkv = _silu(conv).reshape(rows, 3 * QK)
    rk = proj(2)

    beta_all = jax.nn.sigmoid(pba)
    g_all = -jnp.exp(gpar_ref[0:1, :]) * _softplus(pba + gpar_ref[1:2, :])
    g_hi = g_all.astype(BF16)
    g_r1 = g_all - g_hi.astype(F32)
    g_mid = g_r1.astype(BF16)
    g_lo = (g_r1 - g_mid.astype(F32)).astype(BF16)
    g_split = jnp.concatenate([t.reshape(nblk, BLK, LANES) for t in (g_hi, g_mid, g_lo)], axis=1)
    sums = _bmm(jnp.broadcast_to(tri_ref[...][None], (nblk, 2 * BLK, 3 * BLK)), g_split)
    gc_all = sums[:, :BLK].reshape(rows, LANES)
    gl_all = sums[:, BLK:].reshape(rows, LANES)
    gct_all = gc_all.T

    ri = lax.broadcasted_iota(jnp.int32, (BLK, BLK), 0)
    ci = lax.broadcasted_iota(jnp.int32, (BLK, BLK), 1)
    same = (ri >> shift) == (ci >> shift)
    causal = same & (ci <= ri)
    strict = same & (ci < ri)
    eye = (ri == ci).astype(F32)

    cos = cos_ref[0]
    sin = sin_ref[0]
    even = (lax.broadcasted_iota(jnp.int32, (rows, LANES), 1) & 1) == 0

    def rotary(t):
        swapped = jnp.where(even, pltpu.roll(t, LANES - 1, 1), pltpu.roll(t, 1, 1))
        return t * cos + swapped * sin

    per = {k: [] for k in ("qb", "kb", "kbeta", "rhs", "qd", "kt", "gcc", "gct", "egl",
                           "rq", "rk", "rqd", "rkt", "rv")}
    blocks = lambda t: t.reshape(nblk, BLK, t.shape[-1])
    for h in range(N_HEADS):
        qh = qkv[:, h * HEAD_DIM:(h + 1) * HEAD_DIM]
        kh = qkv[:, QK + h * HEAD_DIM:QK + (h + 1) * HEAD_DIM]
        vh = qkv[:, 2 * QK + h * HEAD_DIM:2 * QK + (h + 1) * HEAD_DIM]
        qb = qh * lax.rsqrt(jnp.sum(qh * qh, axis=-1, keepdims=True) + EPS) * (HEAD_DIM ** -0.5)
        kb = kh * lax.rsqrt(jnp.sum(kh * kh, axis=-1, keepdims=True) + EPS)
        bcol = _lane_bcast(beta_all, h)
        gcc = _lane_bcast(gc_all, GATE_LANE0 + h)
        glc = _lane_bcast(gl_all, GATE_LANE0 + h)
        egc = jnp.exp(gcc)
        kbeta = kb * bcol
        per["qb"].append(blocks(qb))
        per["kb"].append(blocks(kb))
        per["kbeta"].append(blocks(kbeta))
        per["rhs"].append(blocks(jnp.concatenate([kbeta * egc, vh * bcol], axis=1)))
        per["qd"].append(blocks(qb * egc))
        per["kt"].append(blocks(kb * jnp.exp(glc - gcc)))
        per["gcc"].append(blocks(gcc))
        per["egl"].append(jnp.exp(glc).reshape(rows // chunk, chunk, HEAD_DIM)[:, 0:1, :])
        per["gct"].extend(gct_all[GATE_LANE0 + h:GATE_LANE0 + h + 1, b * BLK:(b + 1) * BLK][None]
                          for b in range(nblk))
    bat = {k: jnp.concatenate(v, axis=0) for k, v in per.items() if v}
    bf = lambda k: bat[k].astype(BF16)
    rv = proj(3)

    delta = bat["gcc"][:, :, :BLK] - bat["gct"]
    decay = jnp.exp(jnp.where(causal, delta, -jnp.inf))
    kb16 = bf("kb")
    kq = _bmm_nt(jnp.concatenate([bat["kbeta"], bat["qb"]], axis=1).astype(BF16), kb16)
    amat = jnp.where(strict, kq[:, :BLK] * decay, 0.0)
    tinv = _unit_lower_inverse_packed(amat, eye, n_steps)
    z = proj(0)
    for h in range(N_HEADS):
        hs = slice(h * HEAD_DIM, (h + 1) * HEAD_DIM)
        rqh = rotary(rq[:, hs])
        rkh = rotary(rk[:, hs]) * (HEAD_DIM ** -0.5)
        per["rq"].append(blocks(rqh))
        per["rk"].append(blocks(rkh))
        per["rqd"].append(blocks(rqh * rscale_ref[0, :, hs]))
        per["rkt"].append(blocks(rkh * rscale_ref[1, :, hs]))
        per["rv"].append(blocks(rv[:, hs]))
    bat.update({k: jnp.concatenate(per[k], axis=0) for k in ("rq", "rk", "rqd", "rkt", "rv")})
    rg = proj(4)
    wu = _bmm(_pack_lhs(tinv), _pack_rhs(bat["rhs"]))
    qk = (kq[:, BLK:] * decay).astype(BF16)
    rv16 = bf("rv")
    dmat = jnp.broadcast_to(dmat_ref[...][:, None], (N_HEADS, nblk, BLK, BLK)).reshape(n_bat, BLK, BLK)
    rqk = (_bmm_nt(bf("rq"), bf("rk")) * dmat).astype(BF16)

    if chained:
        by_head = lambda t: t.reshape((N_HEADS, nblk) + t.shape[1:])
        transposed16 = lambda k: jnp.swapaxes(bat[k], 1, 2).astype(BF16)
        qg_wu = _bmm(jnp.concatenate([qk, transposed16("kt")], axis=1), wu.astype(BF16))
        qo = qg_wu[:, :BLK]
        gh = by_head(qg_wu[:, BLK:])
        qp = by_head((bat["qd"] - qo[:, :, :HEAD_DIM]).astype(BF16))
        op = by_head(qo[:, :, HEAD_DIM:])
        egl = by_head(bat["egl"])
        ro_rh = _bmm(jnp.concatenate([rqk, transposed16("rkt")], axis=1), rv16)
        ro = by_head(ro_rh[:, :BLK])
        rh = by_head(ro_rh[:, BLK:])
        rqd = by_head(bf("rqd"))
        s_cur = s_ref[0]
        r_cur = r_ref[0]
        o_dn = []
        o_ret = []
        for c in range(nblk):
            s16 = s_cur.astype(BF16)
            qg = jnp.concatenate([qp[:, c], gh[:, c, :, :HEAD_DIM].astype(BF16)], axis=1)
            qgs = _bmm(qg, s16)
            o_dn.append(op[:, c] + qgs[:, :BLK])
            s_cur = s_cur * egl[:, c] - qgs[:, BLK:] + gh[:, c, :, HEAD_DIM:]
            o_ret.append(ro[:, c] + _bmm(rqd[:, c], r_cur.astype(BF16)))
            r_cur = r_cur * cdec_ref[...] + rh[:, c]
        s_ref[0] = s_cur
        r_ref[0] = r_cur
        head_rows = lambda parts, h: jnp.concatenate([p[h] for p in parts], axis=0)
        o_dn = [head_rows(o_dn, h) for h in range(N_HEADS)]
        o_ret = [head_rows(o_ret, h) for h in range(N_HEADS)]
    else:
        n_seq = N_HEADS * sb
        seqs = lambda t: t.reshape(n_seq, tt, t.shape[-1])
        s_old = jnp.concatenate([s0_ref[:, h] for h in range(N_HEADS)], axis=0)
        lhs = jnp.concatenate([seqs(wu[:, :, :HEAD_DIM]), seqs(bat["qd"])], axis=1).astype(BF16)
        res = _bmm(lhs, s_old.astype(BF16))
        v_new = seqs(wu[:, :, HEAD_DIM:]) - res[:, :tt]
        v16 = v_new.astype(BF16)
        s_new = s_old * bat["egl"] + _bmm_tn(seqs(bat["kt"]).astype(BF16), v16)
        o = res[:, tt:].reshape(n_bat, BLK, HEAD_DIM) + _bmm(qk, v16.reshape(n_bat, BLK, HEAD_DIM))
        r_old = jnp.concatenate([r0_ref[:, h] for h in range(N_HEADS)], axis=0)
        ro = _bmm(rqk, rv16) + _bmm(seqs(bat["rqd"]).astype(BF16),
                                    r_old.astype(BF16)).reshape(n_bat, BLK, HEAD_DIM)
        cdec = jnp.broadcast_to(cdec_ref[...][:, None], (N_HEADS, sb, 1, HEAD_DIM))
        r_new = (r_old * cdec.reshape(n_seq, 1, HEAD_DIM)
                 + _bmm_tn(seqs(bat["rkt"]).astype(BF16), seqs(bat["rv"]).astype(BF16)))
        for h in range(N_HEADS):
            s_ref[:, h] = s_new[h * sb:(h + 1) * sb]
            r_ref[:, h] = r_new[h * sb:(h + 1) * sb]
        o_dn = [o[h * nblk:(h + 1) * nblk].reshape(rows, HEAD_DIM) for h in range(N_HEADS)]
        o_ret = [ro[h * nblk:(h + 1) * nblk].reshape(rows, HEAD_DIM) for h in range(N_HEADS)]

    outs = []
    for h in range(N_HEADS):
        hs = slice(h * HEAD_DIM, (h + 1) * HEAD_DIM)
        outs.append(_rmsnorm(o_dn[h], dnw_ref[...]) * _silu(z[:, hs]))
    for h in range(N_HEADS):
        hs = slice(h * HEAD_DIM, (h + 1) * HEAD_DIM)
        o = o_ret[h]
        mu = jnp.mean(o, axis=-1, keepdims=True)
        var = jnp.mean(jnp.square(o - mu), axis=-1, keepdims=True)
        outs.append((o - mu) * lax.rsqrt(var + EPS) * retw_ref[:, hs] * _silu(rg[:, hs]))
    mix = jnp.concatenate(outs, axis=1).astype(BF16)
    h_new = x + jnp.dot(mix, wout_ref[...], preferred_element_type=F32)
    h_ref[...] = h_new.reshape(sb, tt, D_MODEL)


def _ffn_kernel(cfg, h_ref, p_ref, fst_ref, fnw_ref, wup_ref, fcw_ref, fcb_ref, wdown_ref,
                pnw_ref, wgate_ref, wple_ref, finw_ref,
                y_ref, fnew_ref, e_ref):
    sb, tt, final_norm = cfg
    rows = sb * tt
    l_idx = pl.program_id(1)
    base = SUBLANES - (FFN_CONV - 1)

    @pl.when(l_idx == 0)
    def _():
        e_ref[:, base:SUBLANES, :] = fst_ref[...]

    h = h_ref[...].reshape(rows, D_MODEL)
    m = _rmsnorm(h, fnw_ref[...]).astype(BF16)
    for half in (0, D_FF):
        for c0, cw in FF_SLABS:
            cs = slice(half + c0, half + c0 + cw)
            e_ref[:, SUBLANES:, cs] = jnp.dot(m, wup_ref[:, cs],
                                              preferred_element_type=F32).reshape(sb, tt, cw)

    def conv(cs):
        acc = e_ref[:, base:base + tt, cs] * fcw_ref[0:1, cs]
        for j in range(1, FFN_CONV):
            acc = acc + e_ref[:, base + j:base + j + tt, cs] * fcw_ref[j:j + 1, cs]
        return (acc + fcb_ref[:, cs]).reshape(rows, cs.stop - cs.start)

    acc = h
    for c0, cw in FF_SLABS:
        ug = conv(slice(c0, c0 + cw))
        uv = conv(slice(D_FF + c0, D_FF + c0 + cw))
        act = (_silu(ug) * uv).astype(BF16)
        acc = acc + jnp.dot(act, wdown_ref[c0:c0 + cw, :], preferred_element_type=F32)
    tail = e_ref[:, tt + base:tt + SUBLANES, :]
    fnew_ref[...] = tail
    e_ref[:, base:SUBLANES, :] = tail

    h = acc
    gate = jax.nn.sigmoid(jnp.dot(_rmsnorm(h, pnw_ref[...]).astype(BF16), wgate_ref[...],
                                  preferred_element_type=F32))
    pp = jnp.dot(p_ref[...].reshape(rows, PLE_DIM).astype(BF16), wple_ref[...],
                 preferred_element_type=F32)
    h = h + gate * pp
    if final_norm:
        h = _rmsnorm(h, finw_ref[...])
    y_ref[...] = h.reshape(sb, tt, D_MODEL)


def _const_spec(shape):
    zeros = (0,) * len(shape)
    return pl.BlockSpec(shape, lambda b, l: zeros, pipeline_mode=pl.Buffered(1))


def _retention_tables(chunk, rows):
    hh = np.arange(N_HEADS, dtype=np.float64)
    log_gamma = np.log(1.0 - 2.0 ** (-5.0 - hh))
    bpos = (np.arange(chunk, dtype=np.float64) + 1.0)[None, :] * log_gamma[:, None]
    causal = np.tril(np.ones((chunk, chunk), dtype=bool))
    dmat = np.where(causal, np.exp(np.where(causal, bpos[:, :, None] - bpos[:, None, :], 0.0)), 0.0)
    n = BLK // chunk
    dblk = np.einsum('ab,hij->haibj', np.eye(n), dmat).reshape(N_HEADS, BLK, BLK)
    qscale = np.exp(bpos)
    kscale = np.exp(bpos[:, -1:] - bpos)
    expand = lambda t: np.repeat(np.tile(t.T, (rows // chunk, 1)), HEAD_DIM, axis=1)
    rscale = np.stack([expand(qscale), expand(kscale)])
    cdec = np.broadcast_to(np.exp(bpos[:, -1]).reshape(N_HEADS, 1, 1), (N_HEADS, 1, HEAD_DIM))
    return tuple(jnp.asarray(t, F32) for t in (dblk, rscale, cdec))


def _chunk_sum_matrix(chunk):
    r = np.arange(BLK)
    same = (r[:, None] // chunk) == (r[None, :] // chunk)
    causal = same & (r[None, :] <= r[:, None])
    return jnp.asarray(np.tile(np.concatenate([causal, same], axis=0), (1, 3)), BF16)


def _rope_tables(pos0, tt, sb, n_l):
    inv = ROPE_BASE ** (-np.arange(0, HEAD_DIM, 2, dtype=np.float64) / HEAD_DIM)
    ang = (pos0 + np.arange(n_l * tt, dtype=np.float64))[:, None] * inv[None, :]
    cos = np.repeat(np.cos(ang), 2, axis=1)
    sin = np.stack([-np.sin(ang), np.sin(ang)], axis=-1).reshape(n_l * tt, HEAD_DIM)
    tile = lambda t: jnp.asarray(
        np.tile(t.reshape(n_l, 1, tt, HEAD_DIM), (1, sb, 1, 1)).reshape(n_l, sb * tt, HEAD_DIM), F32)
    return tile(cos), tile(sin)


def _mixer(x, cstate, s0, r0, pos0, wts, sb, tt):
    n_b, seq, _ = x.shape
    chunk = CHUNK if seq % CHUNK == 0 else seq
    chained = chunk == BLK
    rows = sb * tt
    n_l = seq // tt
    assert n_b % sb == 0 and seq % tt == 0 and rows % BLK == 0 and BLK % chunk == 0
    assert (sb == 1 and tt % chunk == 0) if chained else (tt == seq == chunk)
    cos, sin = _rope_tables(pos0, tt, sb, n_l)
    dmat, rscale, cdec = _retention_tables(chunk, rows)
    tri = _chunk_sum_matrix(chunk)
    state_spec = pl.BlockSpec((sb, N_HEADS, HEAD_DIM, HEAD_DIM), lambda b, l: (b, 0, 0, 0))
    x_spec = pl.BlockSpec((sb, tt, D_MODEL), lambda b, l: (b, l, 0))
    conv_spec = pl.BlockSpec((sb, DN_CONV - 1, 3 * QK), lambda b, l: (b, 0, 0))
    rope_spec = pl.BlockSpec((1, rows, HEAD_DIM), lambda b, l: (l, 0, 0))
    consts = (tri, dmat, rscale, cdec) + tuple(wts)
    return pl.pallas_call(
        functools.partial(_mixer_kernel, (sb, tt, chunk, chained)),
        grid=(n_b // sb, n_l),
        in_specs=[x_spec, conv_spec, state_spec, state_spec, rope_spec, rope_spec]
        + [_const_spec(c.shape) for c in consts],
        out_specs=[x_spec, conv_spec, state_spec, state_spec],
        out_shape=[jax.ShapeDtypeStruct(x.shape, F32),
                   jax.ShapeDtypeStruct(cstate.shape, F32),
                   jax.ShapeDtypeStruct(s0.shape, F32),
                   jax.ShapeDtypeStruct(r0.shape, F32)],
        scratch_shapes=[pltpu.VMEM((sb, SUBLANES + tt, 3 * QK), F32)],
        compiler_params=pltpu.CompilerParams(
            dimension_semantics=("arbitrary", "arbitrary"), vmem_limit_bytes=VMEM_LIMIT),
        name="mixer",
    )(x, cstate, s0, r0, cos, sin, *consts)


def _ffn(h, p, fstate, wts, final_norm, sb, tt):
    n_b, seq, _ = h.shape
    n_l = seq // tt
    assert n_b % sb == 0 and seq % tt == 0
    x_spec = pl.BlockSpec((sb, tt, D_MODEL), lambda b, l: (b, l, 0))
    p_spec = pl.BlockSpec((sb, tt, PLE_DIM), lambda b, l: (b, l, 0))
    st_spec = pl.BlockSpec((sb, FFN_CONV - 1, 2 * D_FF), lambda b, l: (b, 0, 0))
    return pl.pallas_call(
        functools.partial(_ffn_kernel, (sb, tt, final_norm)),
        grid=(n_b // sb, n_l),
        in_specs=[x_spec, p_spec, st_spec] + [_const_spec(w.shape) for w in wts],
        out_specs=[x_spec, st_spec],
        out_shape=[jax.ShapeDtypeStruct(h.shape, F32), jax.ShapeDtypeStruct(fstate.shape, F32)],
        scratch_shapes=[pltpu.VMEM((sb, SUBLANES + tt, 2 * D_FF), F32)],
        compiler_params=pltpu.CompilerParams(
            dimension_semantics=("arbitrary", "arbitrary"), vmem_limit_bytes=VMEM_LIMIT),
        name="ffn",
    )(h, p, fstate, *wts)


def kernel(x_prompt, x_sample, p_prompt, p_sample, state_dn_conv, state_dn, state_ret, state_ffn_conv, attn_norm_w, w_in, dn_conv_w, dn_A_log, dn_dt_bias, dn_norm_w, ret_norm_w, w_out, ffn_norm_w, w_up, ffn_conv_w, ffn_conv_b, w_down, ple_norm_w, w_ple_gate, w_ple, final_norm_w):
    depth = w_in.shape[0]
    n_bp = x_prompt.shape[0]
    dec_len = x_sample.shape[1]
    row = lambda v: v.reshape(1, -1).astype(F32)
    hp, hs = x_prompt, x_sample
    outs_p = ([], [], [], [])
    outs_s = ([], [], [], [])
    for i in range(depth):
        wi = w_in[i]
        wdn = wi[:, :4 * QK].astype(BF16)
        wret = wi[:, 4 * QK + 2 * N_HEADS:].astype(BF16)
        wba = jnp.pad(wi[:, 4 * QK:4 * QK + 2 * N_HEADS].astype(BF16),
                      ((0, 0), (0, LANES - 2 * N_HEADS)))
        gpar = jnp.zeros((2, LANES), F32)
        gpar = gpar.at[0, GATE_LANE0:GATE_LANE0 + N_HEADS].set(dn_A_log[i])
        gpar = gpar.at[1, GATE_LANE0:GATE_LANE0 + N_HEADS].set(dn_dt_bias[i])
        mixer_w = (row(attn_norm_w[i]), wdn, wret, wba, dn_conv_w[i], gpar, row(dn_norm_w[i]),
                   row(ret_norm_w[i]), w_out[i].astype(BF16))
        ffn_w = (row(ffn_norm_w[i]), w_up[i].astype(BF16), ffn_conv_w[i], row(ffn_conv_b[i]),
                 w_down[i].astype(BF16), row(ple_norm_w[i]), w_ple_gate[i].astype(BF16),
                 w_ple[i].astype(BF16), row(final_norm_w))
        last = i == depth - 1
        zc = jnp.zeros((n_bp, DN_CONV - 1, 3 * QK), F32)
        zs = jnp.zeros((n_bp, N_HEADS, HEAD_DIM, HEAD_DIM), F32)
        zf = jnp.zeros((n_bp, FFN_CONV - 1, 2 * D_FF), F32)
        hp, c1, s1, r1 = _mixer(hp, zc, zs, zs, 0, mixer_w, *PROMPT_MIXER_TILE)
        hp, f1 = _ffn(hp, p_prompt[i], zf, ffn_w, last, *PROMPT_FFN_TILE)
        hs, c2, s2, r2 = _mixer(hs, state_dn_conv[i], state_dn[i], state_ret[i], PAST_LEN,
                                mixer_w, SAMPLE_MIXER_SEQS, dec_len)
        hs, f2 = _ffn(hs, p_sample[i], state_ffn_conv[i], ffn_w, last, SAMPLE_FFN_SEQS, dec_len)
        for lst, val in zip(outs_p, (c1, s1, r1, f1)):
            lst.append(val)
        for lst, val in zip(outs_s, (c2, s2, r2, f2)):
            lst.append(val)
    return (hp, hs,
            jnp.stack(outs_p[0]), jnp.stack(outs_p[1]), jnp.stack(outs_p[2]), jnp.stack(outs_p[3]),
            jnp.stack(outs_s[0]), jnp.stack(outs_s[1]), jnp.stack(outs_s[2]), jnp.stack(outs_s[3]))
```

```python
import functools
import math

import jax
import jax.numpy as jnp
import numpy as np
from jax import lax
from jax.experimental import pallas as pl
from jax.experimental.pallas import tpu as pltpu

F32 = jnp.float32
BF16 = jnp.bfloat16

D_MODEL = 1024
HEAD_DIM = 128
N_HEADS = 4
QK = N_HEADS * HEAD_DIM
DN_CONV = 4
FFN_CONV = 3
D_FF = 2816
PLE_DIM = 256
CHUNK = 64
PAST_LEN = 16384
ROPE_BASE = 10000.0
EPS = 1e-6
LANES = 128
SUBLANES = 8
MXU_DIM = 256
BLK = 64
GATE_LANE0 = N_HEADS
FF_SLABS = tuple((c, MXU_DIM) for c in range(0, D_FF, MXU_DIM))
VMEM_LIMIT = 56 * 1024 * 1024
PROMPT_MIXER_TILE = (1, 512)
PROMPT_FFN_TILE = (1, 512)
SAMPLE_MIXER_SEQS = 16
SAMPLE_FFN_SEQS = 32


def _split2(x):
    hi = x.astype(BF16)
    lo = (x - hi.astype(F32)).astype(BF16)
    return hi, lo


def _rmsnorm(x, w):
    return x * lax.rsqrt(jnp.mean(x * x, axis=-1, keepdims=True) + EPS) * w


def _silu(x):
    return x * jax.nn.sigmoid(x)


def _softplus(x):
    return jnp.maximum(x, 0.0) + jnp.log(1.0 + jnp.exp(-jnp.abs(x)))


def _lane_bcast(x, lane, width=LANES):
    return jnp.broadcast_to(x[:, lane:lane + 1], (x.shape[0], width))


def _bmm(a, b):
    return lax.dot_general(a, b, (((2,), (1,)), ((0,), (0,))), preferred_element_type=F32)


def _bmm_nt(a, b):
    return lax.dot_general(a, b, (((2,), (2,)), ((0,), (0,))), preferred_element_type=F32)


def _bmm_tn(a, b):
    return lax.dot_general(a, b, (((1,), (1,)), ((0,), (0,))), preferred_element_type=F32)


def _pack_lhs(m):
    hi = m.astype(BF16).astype(F32)
    return jnp.concatenate([hi, m - hi, hi], axis=-1).astype(BF16)


def _pack_rhs(m):
    hi, lo = _split2(m)
    return jnp.concatenate([hi, hi, lo], axis=-2)


def _unit_lower_inverse_packed(a, eye, n_steps):
    c = a.shape[-1]
    upper = lax.broadcasted_iota(jnp.int32, a.shape[:-1] + (2 * c,), a.ndim - 1) >= c
    a2 = _bmm(_pack_lhs(a), _pack_rhs(a))
    mp = jnp.concatenate([a2, eye - a], axis=-1)
    for _ in range(n_steps):
        m = mp[..., :c]
        hi = m.astype(BF16).astype(F32)
        lhs = jnp.concatenate([hi, m - hi], axis=-1).astype(BF16)
        r16 = mp.astype(BF16)
        res = _bmm(lhs, jnp.concatenate([r16, r16], axis=-2))
        mp = res + jnp.where(upper, mp, 0.0)
    return mp[..., c:]


def _mixer_kernel(cfg, x_ref, cst_ref, s0_ref, r0_ref, cos_ref, sin_ref, tri_ref, dmat_ref,
                  rscale_ref, cdec_ref, anw_ref, wdn_ref, wret_ref, wba_ref, convw_ref, gpar_ref,
                  dnw_ref, retw_ref, wout_ref, h_ref, cnew_ref, s_ref, r_ref, e_ref):
    sb, tt, chunk, chained = cfg
    rows = sb * tt
    nblk = rows // BLK
    n_bat = N_HEADS * nblk
    shift = int(math.log2(chunk))
    n_steps = shift - 1
    base = SUBLANES - (DN_CONV - 1)
    l_idx = pl.program_id(1)

    @pl.when(l_idx == 0)
    def _():
        e_ref[:, base:SUBLANES, :] = cst_ref[...]
        if chained:
            s_ref[...] = s0_ref[...]
            r_ref[...] = r0_ref[...]

    x = x_ref[...].reshape(rows, D_MODEL)
    a = _rmsnorm(x, anw_ref[...]).astype(BF16)
    e_ref[:, SUBLANES:, :] = jnp.dot(a, wdn_ref[:, 0:3 * QK],
                                     preferred_element_type=F32).reshape(sb, tt, 3 * QK)
    slabs = [wdn_ref.at[:, 3 * QK:4 * QK]] + [wret_ref.at[:, i * QK:(i + 1) * QK] for i in range(4)]
    proj = lambda i: jnp.dot(a, slabs[i][...], preferred_element_type=F32)
    pba = jnp.dot(a, wba_ref[...], preferred_element_type=F32)

    conv = e_ref[:, base:base + tt, :] * convw_ref[0:1, :]
    for j in range(1, DN_CONV):
        conv = conv + e_ref[:, base + j:base + j + tt, :] * convw_ref[j:j + 1, :]
    tail = e_ref[:, tt + base:tt + SUBLANES, :]
    cnew_ref[...] = tail
    e_ref[:, base:SUBLANES, :] = tail
    rq = proj(1)
    qkv = _silu(conv).reshape(rows, 3 * QK)
    rk = proj(2)

    beta_all = jax.nn.sigmoid(pba)
    g_all = -jnp.exp(gpar_ref[0:1, :]) * _softplus(pba + gpar_ref[1:2, :])
    g_hi = g_all.astype(BF16)
    g_r1 = g_all - g_hi.astype(F32)
    g_mid = g_r1.astype(BF16)
    g_lo = (g_r1 - g_mid.astype(F32)).astype(BF16)
    g_split = jnp.concatenate([t.reshape(nblk, BLK, LANES) for t in (g_hi, g_mid, g_lo)], axis=1)
    sums = _bmm(jnp.broadcast_to(tri_ref[...][None], (nblk, 2 * BLK, 3 * BLK)), g_split)
    gc_all = sums[:, :BLK].reshape(rows, LANES)
    gl_all = sums[:, BLK:].reshape(rows, LANES)
    gct_all = gc_all.T

    ri = lax.broadcasted_iota(jnp.int32, (BLK, BLK), 0)
    ci = lax.broadcasted_iota(jnp.int32, (BLK, BLK), 1)
    same = (ri >> shift) == (ci >> shift)
    causal = same & (ci <= ri)
    strict = same & (ci < ri)
    eye = (ri == ci).astype(F32)

    cos = cos_ref[0]
    sin = sin_ref[0]
    even = (lax.broadcasted_iota(jnp.int32, (rows, LANES), 1) & 1) == 0

    def rotary(t):
        swapped = jnp.where(even, pltpu.roll(t, LANES - 1, 1), pltpu.roll(t, 1, 1))
        return t * cos + swapped * sin

    per = {k: [] for k in ("qb", "kb", "kbeta", "rhs", "qd", "kt", "gcc", "gct", "egl",
                           "rq", "rk", "rqd", "rkt", "rv")}
    blocks = lambda t: t.reshape(nblk, BLK, t.shape[-1])
    for h in range(N_HEADS):
        qh = qkv[:, h * HEAD_DIM:(h + 1) * HEAD_DIM]
        kh = qkv[:, QK + h * HEAD_DIM:QK + (h + 1) * HEAD_DIM]
        vh = qkv[:, 2 * QK + h * HEAD_DIM:2 * QK + (h + 1) * HEAD_DIM]
        qb = qh * lax.rsqrt(jnp.sum(qh * qh, axis=-1, keepdims=True) + EPS) * (HEAD_DIM ** -0.5)
        kb = kh * lax.rsqrt(jnp.sum(kh * kh, axis=-1, keepdims=True) + EPS)
        bcol = _lane_bcast(beta_all, h)
        gcc = _lane_bcast(gc_all, GATE_LANE0 + h)
        glc = _lane_bcast(gl_all, GATE_LANE0 + h)
        egc = jnp.exp(gcc)
        kbeta = kb * bcol
        per["qb"].append(blocks(qb))
        per["kb"].append(blocks(kb))
        per["kbeta"].append(blocks(kbeta))
        per["rhs"].append(blocks(jnp.concatenate([kbeta * egc, vh * bcol], axis=1)))
        per["qd"].append(blocks(qb * egc))
        per["kt"].append(blocks(kb * jnp.exp(glc - gcc)))
        per["gcc"].append(blocks(gcc))
        per["egl"].append(jnp.exp(glc).reshape(rows // chunk, chunk, HEAD_DIM)[:, 0:1, :])
        per["gct"].extend(gct_all[GATE_LANE0 + h:GATE_LANE0 + h + 1, b * BLK:(b + 1) * BLK][None]
                          for b in range(nblk))
    bat = {k: jnp.concatenate(v, axis=0) for k, v in per.items() if v}
    bf = lambda k: bat[k].astype(BF16)
    rv = proj(3)

    delta = bat["gcc"][:, :, :BLK] - bat["gct"]
    decay = jnp.exp(jnp.where(causal, delta, -jnp.inf))
    kb16 = bf("kb")
    kq = _bmm_nt(jnp.concatenate([bat["kbeta"], bat["qb"]], axis=1).astype(BF16), kb16)
    amat = jnp.where(strict, kq[:, :BLK] * decay, 0.0)
    halves = (slice(0, n_bat // 2), slice(n_bat // 2, n_bat)) if chained else (slice(0, n_bat),)
    tinv = jnp.concatenate([_unit_lower_inverse_packed(amat[hb], eye, n_steps) for hb in halves], axis=0)
    z = proj(0)
    for h in range(N_HEADS):
        hs = slice(h * HEAD_DIM, (h + 1) * HEAD_DIM)
        rqh = rotary(rq[:, hs])
        rkh = rotary(rk[:, hs]) * (HEAD_DIM ** -0.5)
        per["rq"].append(blocks(rqh))
        per["rk"].append(blocks(rkh))
        per["rqd"].append(blocks(rqh * rscale_ref[0, :, hs]))
        per["rkt"].append(blocks(rkh * rscale_ref[1, :, hs]))
        per["rv"].append(blocks(rv[:, hs]))
    bat.update({k: jnp.concatenate(per[k], axis=0) for k in ("rq", "rk", "rqd", "rkt", "rv")})
    rg = proj(4)
    wu = _bmm(_pack_lhs(tinv), _pack_rhs(bat["rhs"]))
    qk = (kq[:, BLK:] * decay).astype(BF16)
    rv16 = bf("rv")
    dmat = jnp.broadcast_to(dmat_ref[...][:, None], (N_HEADS, nblk, BLK, BLK)).reshape(n_bat, BLK, BLK)
    rqk = (_bmm_nt(bf("rq"), bf("rk")) * dmat).astype(BF16)

    if chained:
        by_head = lambda t: t.reshape((N_HEADS, nblk) + t.shape[1:])
        transposed16 = lambda k: jnp.swapaxes(bat[k], 1, 2).astype(BF16)
        qg_wu = _bmm(jnp.concatenate([qk, transposed16("kt")], axis=1), wu.astype(BF16))
        qo = qg_wu[:, :BLK]
        gh = by_head(qg_wu[:, BLK:])
        qp = by_head((bat["qd"] - qo[:, :, :HEAD_DIM]).astype(BF16))
        op = by_head(qo[:, :, HEAD_DIM:])
        egl = by_head(bat["egl"])
        ro_rh = _bmm(jnp.concatenate([rqk, transposed16("rkt")], axis=1), rv16)
        ro = by_head(ro_rh[:, :BLK])
        rh = by_head(ro_rh[:, BLK:])
        rqd = by_head(bf("rqd"))
        s_cur = s_ref[0]
        r_cur = r_ref[0]
        o_dn = []
        o_ret = []
        for c in range(nblk):
            s16 = s_cur.astype(BF16)
            qg = jnp.concatenate([qp[:, c], gh[:, c, :, :HEAD_DIM].astype(BF16)], axis=1)
            qgs = _bmm(qg, s16)
            o_dn.append(op[:, c] + qgs[:, :BLK])
            s_cur = s_cur * egl[:, c] - qgs[:, BLK:] + gh[:, c, :, HEAD_DIM:]
            o_ret.append(ro[:, c] + _bmm(rqd[:, c], r_cur.astype(BF16)))
            r_cur = r_cur * cdec_ref[...] + rh[:, c]
        s_ref[0] = s_cur
        r_ref[0] = r_cur
        head_rows = lambda parts, h: jnp.concatenate([p[h] for p in parts], axis=0)
        o_dn = [head_rows(o_dn, h) for h in range(N_HEADS)]
        o_ret = [head_rows(o_ret, h) for h in range(N_HEADS)]
    else:
        n_seq = N_HEADS * sb
        seqs = lambda t: t.reshape(n_seq, tt, t.shape[-1])
        s_old = jnp.concatenate([s0_ref[:, h] for h in range(N_HEADS)], axis=0)
        lhs = jnp.concatenate([seqs(wu[:, :, :HEAD_DIM]), seqs(bat["qd"])], axis=1).astype(BF16)
        res = _bmm(lhs, s_old.astype(BF16))
        v_new = seqs(wu[:, :, HEAD_DIM:]) - res[:, :tt]
        v16 = v_new.astype(BF16)
        s_new = s_old * bat["egl"] + _bmm_tn(seqs(bat["kt"]).astype(BF16), v16)
        o = res[:, tt:].reshape(n_bat, BLK, HEAD_DIM) + _bmm(qk, v16.reshape(n_bat, BLK, HEAD_DIM))
        r_old = jnp.concatenate([r0_ref[:, h] for h in range(N_HEADS)], axis=0)
        ro = _bmm(rqk, rv16) + _bmm(seqs(bat["rqd"]).astype(BF16),
                                    r_old.astype(BF16)).reshape(n_bat, BLK, HEAD_DIM)
        cdec = jnp.broadcast_to(cdec_ref[...][:, None], (N_HEADS, sb, 1, HEAD_DIM))
        r_new = (r_old * cdec.reshape(n_seq, 1, HEAD_DIM)
                 + _bmm_tn(seqs(bat["rkt"]).astype(BF16), seqs(bat["rv"]).astype(BF16)))
        for h in range(N_HEADS):
            s_ref[:, h] = s_new[h * sb:(h + 1) * sb]
            r_ref[:, h] = r_new[h * sb:(h + 1) * sb]
        o_dn = [o[h * nblk:(h + 1) * nblk].reshape(rows, HEAD_DIM) for h in range(N_HEADS)]
        o_ret = [ro[h * nblk:(h + 1) * nblk].reshape(rows, HEAD_DIM) for h in range(N_HEADS)]

    outs = []
    for h in range(N_HEADS):
        hs = slice(h * HEAD_DIM, (h + 1) * HEAD_DIM)
        outs.append(_rmsnorm(o_dn[h], dnw_ref[...]) * _silu(z[:, hs]))
    for h in range(N_HEADS):
        hs = slice(h * HEAD_DIM, (h + 1) * HEAD_DIM)
        o = o_ret[h]
        mu = jnp.mean(o, axis=-1, keepdims=True)
        var = jnp.mean(jnp.square(o - mu), axis=-1, keepdims=True)
        outs.append((o - mu) * lax.rsqrt(var + EPS) * retw_ref[:, hs] * _silu(rg[:, hs]))
    mix = jnp.concatenate(outs, axis=1).astype(BF16)
    h_new = x + jnp.dot(mix, wout_ref[...], preferred_element_type=F32)
    h_ref[...] = h_new.reshape(sb, tt, D_MODEL)


def _ffn_kernel(cfg, h_ref, p_ref, fst_ref, fnw_ref, wup_ref, fcw_ref, fcb_ref, wdown_ref,
                pnw_ref, wgate_ref, wple_ref, finw_ref,
                y_ref, fnew_ref, e_ref):
    sb, tt, final_norm = cfg
    rows = sb * tt
    l_idx = pl.program_id(1)
    base = SUBLANES - (FFN_CONV - 1)

    @pl.when(l_idx == 0)
    def _():
        e_ref[:, base:SUBLANES, :] = fst_ref[...]

    h = h_ref[...].reshape(rows, D_MODEL)
    m = _rmsnorm(h, fnw_ref[...]).astype(BF16)
    for half in (0, D_FF):
        for c0, cw in FF_SLABS:
            cs = slice(half + c0, half + c0 + cw)
            e_ref[:, SUBLANES:, cs] = jnp.dot(m, wup_ref[:, cs],
                                              preferred_element_type=F32).reshape(sb, tt, cw)

    def conv(cs):
        acc = e_ref[:, base:base + tt, cs] * fcw_ref[0:1, cs]
        for j in range(1, FFN_CONV):
            acc = acc + e_ref[:, base + j:base + j + tt, cs] * fcw_ref[j:j + 1, cs]
        return (acc + fcb_ref[:, cs]).reshape(rows, cs.stop - cs.start)

    acc = h
    for c0, cw in FF_SLABS:
        ug = conv(slice(c0, c0 + cw))
        uv = conv(slice(D_FF + c0, D_FF + c0 + cw))
        act = (_silu(ug) * uv).astype(BF16)
        acc = acc + jnp.dot(act, wdown_ref[c0:c0 + cw, :], preferred_element_type=F32)
    tail = e_ref[:, tt + base:tt + SUBLANES, :]
    fnew_ref[...] = tail
    e_ref[:, base:SUBLANES, :] = tail

    h = acc
    gate = jax.nn.sigmoid(jnp.dot(_rmsnorm(h, pnw_ref[...]).astype(BF16), wgate_ref[...],
                                  preferred_element_type=F32))
    pp = jnp.dot(p_ref[...].reshape(rows, PLE_DIM).astype(BF16), wple_ref[...],
                 preferred_element_type=F32)
    h = h + gate * pp
    if final_norm:
        h = _rmsnorm(h, finw_ref[...])
    y_ref[...] = h.reshape(sb, tt, D_MODEL)


def _const_spec(shape):
    zeros = (0,) * len(shape)
    return pl.BlockSpec(shape, lambda b, l: zeros, pipeline_mode=pl.Buffered(1))


def _retention_tables(chunk, rows):
    hh = np.arange(N_HEADS, dtype=np.float64)
    log_gamma = np.log(1.0 - 2.0 ** (-5.0 - hh))
    bpos = (np.arange(chunk, dtype=np.float64) + 1.0)[None, :] * log_gamma[:, None]
    causal = np.tril(np.ones((chunk, chunk), dtype=bool))
    dmat = np.where(causal, np.exp(np.where(causal, bpos[:, :, None] - bpos[:, None, :], 0.0)), 0.0)
    n = BLK // chunk
    dblk = np.einsum('ab,hij->haibj', np.eye(n), dmat).reshape(N_HEADS, BLK, BLK)
    qscale = np.exp(bpos)
    kscale = np.exp(bpos[:, -1:] - bpos)
    expand = lambda t: np.repeat(np.tile(t.T, (rows // chunk, 1)), HEAD_DIM, axis=1)
    rscale = np.stack([expand(qscale), expand(kscale)])
    cdec = np.broadcast_to(np.exp(bpos[:, -1]).reshape(N_HEADS, 1, 1), (N_HEADS, 1, HEAD_DIM))
    return tuple(jnp.asarray(t, F32) for t in (dblk, rscale, cdec))


def _chunk_sum_matrix(chunk):
    r = np.arange(BLK)
    same = (r[:, None] // chunk) == (r[None, :] // chunk)
    causal = same & (r[None, :] <= r[:, None])
    return jnp.asarray(np.tile(np.concatenate([causal, same], axis=0), (1, 3)), BF16)


def _rope_tables(pos0, tt, sb, n_l):
    inv = ROPE_BASE ** (-np.arange(0, HEAD_DIM, 2, dtype=np.float64) / HEAD_DIM)
    ang = (pos0 + np.arange(n_l * tt, dtype=np.float64))[:, None] * inv[None, :]
    cos = np.repeat(np.cos(ang), 2, axis=1)
    sin = np.stack([-np.sin(ang), np.sin(ang)], axis=-1).reshape(n_l * tt, HEAD_DIM)
    tile = lambda t: jnp.asarray(
        np.tile(t.reshape(n_l, 1, tt, HEAD_DIM), (1, sb, 1, 1)).reshape(n_l, sb * tt, HEAD_DIM), F32)
    return tile(cos), tile(sin)


def _mixer(x, cstate, s0, r0, pos0, wts, sb, tt):
    n_b, seq, _ = x.shape
    chunk = CHUNK if seq % CHUNK == 0 else seq
    chained = chunk == BLK
    rows = sb * tt
    n_l = seq // tt
    assert n_b % sb == 0 and seq % tt == 0 and rows % BLK == 0 and BLK % chunk == 0
    assert (sb == 1 and tt % chunk == 0) if chained else (tt == seq == chunk)
    cos, sin = _rope_tables(pos0, tt, sb, n_l)
    dmat, rscale, cdec = _retention_tables(chunk, rows)
    tri = _chunk_sum_matrix(chunk)
    state_spec = pl.BlockSpec((sb, N_HEADS, HEAD_DIM, HEAD_DIM), lambda b, l: (b, 0, 0, 0))
    x_spec = pl.BlockSpec((sb, tt, D_MODEL), lambda b, l: (b, l, 0))
    conv_spec = pl.BlockSpec((sb, DN_CONV - 1, 3 * QK), lambda b, l: (b, 0, 0))
    rope_spec = pl.BlockSpec((1, rows, HEAD_DIM), lambda b, l: (l, 0, 0))
    consts = (tri, dmat, rscale, cdec) + tuple(wts)
    return pl.pallas_call(
        functools.partial(_mixer_kernel, (sb, tt, chunk, chained)),
        grid=(n_b // sb, n_l),
        in_specs=[x_spec, conv_spec, state_spec, state_spec, rope_spec, rope_spec]
        + [_const_spec(c.shape) for c in consts],
        out_specs=[x_spec, conv_spec, state_spec, state_spec],
        out_shape=[jax.ShapeDtypeStruct(x.shape, F32),
                   jax.ShapeDtypeStruct(cstate.shape, F32),
                   jax.ShapeDtypeStruct(s0.shape, F32),
                   jax.ShapeDtypeStruct(r0.shape, F32)],
        scratch_shapes=[pltpu.VMEM((sb, SUBLANES + tt, 3 * QK), F32)],
        compiler_params=pltpu.CompilerParams(
            dimension_semantics=("arbitrary", "arbitrary"), vmem_limit_bytes=VMEM_LIMIT),
        name="mixer",
    )(x, cstate, s0, r0, cos, sin, *consts)


def _ffn(h, p, fstate, wts, final_norm, sb, tt):
    n_b, seq, _ = h.shape
    n_l = seq // tt
    assert n_b % sb == 0 and seq % tt == 0
    x_spec = pl.BlockSpec((sb, tt, D_MODEL), lambda b, l: (b, l, 0))
    p_spec = pl.BlockSpec((sb, tt, PLE_DIM), lambda b, l: (b, l, 0))
    st_spec = pl.BlockSpec((sb, FFN_CONV - 1, 2 * D_FF), lambda b, l: (b, 0, 0))
    return pl.pallas_call(
        functools.partial(_ffn_kernel, (sb, tt, final_norm)),
        grid=(n_b // sb, n_l),
        in_specs=[x_spec, p_spec, st_spec] + [_const_spec(w.shape) for w in wts],
        out_specs=[x_spec, st_spec],
        out_shape=[jax.ShapeDtypeStruct(h.shape, F32), jax.ShapeDtypeStruct(fstate.shape, F32)],
        scratch_shapes=[pltpu.VMEM((sb, SUBLANES + tt, 2 * D_FF), F32)],
        compiler_params=pltpu.CompilerParams(
            dimension_semantics=("arbitrary", "arbitrary"), vmem_limit_bytes=VMEM_LIMIT),
        name="ffn",
    )(h, p, fstate, *wts)


def kernel(x_prompt, x_sample, p_prompt, p_sample, state_dn_conv, state_dn, state_ret, state_ffn_conv, attn_norm_w, w_in, dn_conv_w, dn_A_log, dn_dt_bias, dn_norm_w, ret_norm_w, w_out, ffn_norm_w, w_up, ffn_conv_w, ffn_conv_b, w_down, ple_norm_w, w_ple_gate, w_ple, final_norm_w):
    depth = w_in.shape[0]
    n_bp = x_prompt.shape[0]
    dec_len = x_sample.shape[1]
    row = lambda v: v.reshape(1, -1).astype(F32)
    hp, hs = x_prompt, x_sample
    outs_p = ([], [], [], [])
    outs_s = ([], [], [], [])
    for i in range(depth):
        wi = w_in[i]
        wdn = wi[:, :4 * QK].astype(BF16)
        wret = wi[:, 4 * QK + 2 * N_HEADS:].astype(BF16)
        wba = jnp.pad(wi[:, 4 * QK:4 * QK + 2 * N_HEADS].astype(BF16),
                      ((0, 0), (0, LANES - 2 * N_HEADS)))
        gpar = jnp.zeros((2, LANES), F32)
        gpar = gpar.at[0, GATE_LANE0:GATE_LANE0 + N_HEADS].set(dn_A_log[i])
        gpar = gpar.at[1, GATE_LANE0:GATE_LANE0 + N_HEADS].set(dn_dt_bias[i])
        mixer_w = (row(attn_norm_w[i]), wdn, wret, wba, dn_conv_w[i], gpar, row(dn_norm_w[i]),
                   row(ret_norm_w[i]), w_out[i].astype(BF16))
        ffn_w = (row(ffn_norm_w[i]), w_up[i].astype(BF16), ffn_conv_w[i], row(ffn_conv_b[i]),
                 w_down[i].astype(BF16), row(ple_norm_w[i]), w_ple_gate[i].astype(BF16),
                 w_ple[i].astype(BF16), row(final_norm_w))
        last = i == depth - 1
        zc = jnp.zeros((n_bp, DN_CONV - 1, 3 * QK), F32)
        zs = jnp.zeros((n_bp, N_HEADS, HEAD_DIM, HEAD_DIM), F32)
        zf = jnp.zeros((n_bp, FFN_CONV - 1, 2 * D_FF), F32)
        hp, c1, s1, r1 = _mixer(hp, zc, zs, zs, 0, mixer_w, *PROMPT_MIXER_TILE)
        hp, f1 = _ffn(hp, p_prompt[i], zf, ffn_w, last, *PROMPT_FFN_TILE)
        hs, c2, s2, r2 = _mixer(hs, state_dn_conv[i], state_dn[i], state_ret[i], PAST_LEN,
                                mixer_w, SAMPLE_MIXER_SEQS, dec_len)
        hs, f2 = _ffn(hs, p_sample[i], state_ffn_conv[i], ffn_w, last, SAMPLE_FFN_SEQS, dec_len)
        for lst, val in zip(outs_p, (c1, s1, r1, f1)):
            lst.append(val)
        for lst, val in zip(outs_s, (c2, s2, r2, f2)):
            lst.append(val)
    return (hp, hs,
            jnp.stack(outs_p[0]), jnp.stack(outs_p[1]), jnp.stack(outs_p[2]), jnp.stack(outs_p[3]),
            jnp.stack(outs_s[0]), jnp.stack(outs_s[1]), jnp.stack(outs_s[2]), jnp.stack(outs_s[3]))
```
